```python
import math
import jax, jax.numpy as jnp
from jax import lax
import numpy as np

D_MODEL = 2048
BATCH = 2
SEQ = 8192
DEPTH = 1

MEM_LEN = 256
LRU_WIDTH = D_MODEL // 2
LRU_BLOCKS = 8
LRU_BLOCK_DIM = LRU_WIDTH // LRU_BLOCKS
LRU_CONV_WIDTH = 4
LRU_C = 8.0
NSA_HEADS = 8
NSA_KV_HEADS = 2
NSA_HEAD_DIM = (D_MODEL // 4) // NSA_HEADS
CMP_STRIDE = 16
CMP_BLOCK = 2 * CMP_STRIDE
SLC_BLOCK = 64
N_SELECT = 16
WINDOW = 512
Q_BLOCK = 128
MEM_HEADS = 4
MEM_HEAD_DIM = (D_MODEL // 4) // MEM_HEADS
D_FF = ((8 * D_MODEL // 3) + 255) // 256 * 256
FFN_CONV_WIDTH = 3

LN_EPS = 1e-5
NEG_INF = -1e30
FORCE_SCORE = 1e9

kernel_name = 'hymba_style_lru_nsa_memxattn_convffn_deepnorm'


def layer_norm(x, g, b):
    xf = x.astype(jnp.float32)
    mu = jnp.mean(xf, axis=-1, keepdims=True)
    var = jnp.mean(jnp.square(xf - mu), axis=-1, keepdims=True)
    y = (xf - mu) * lax.rsqrt(var + LN_EPS)
    return (y * g + b).astype(x.dtype)


def causal_dwconv(x, w, b):
    K = w.shape[0]
    S = x.shape[1]
    xp = jnp.pad(x, ((0, 0), (K - 1, 0), (0, 0)))
    y = b
    for k in range(K):
        y = y + xp[:, k:k + S] * w[k]
    return y


def masked_softmax(s, mask):
    p = jax.nn.softmax(jnp.where(mask, s, NEG_INF), axis=-1)
    return jnp.where(mask, p, 0.0)


def rg_lru_group(xr, yg, conv_w, conv_b, wa, ba, wx, bx, lam):
    B, S, C = xr.shape
    f32 = jnp.float32
    xc = causal_dwconv(xr, conv_w, conv_b).astype(f32)
    xb = xc.reshape(B, S, LRU_BLOCKS, LRU_BLOCK_DIM)
    r = jax.nn.sigmoid(jnp.einsum('bsnc,ncd->bsnd', xb, wa.astype(f32)) + ba).reshape(B, S, C)
    i = jax.nn.sigmoid(jnp.einsum('bsnc,ncd->bsnd', xb, wx.astype(f32)) + bx).reshape(B, S, C)
    log_a = -LRU_C * r * jax.nn.softplus(-lam.astype(f32))
    a = jnp.exp(log_a)
    u = jnp.sqrt(-jnp.expm1(2.0 * log_a)) * (i * xc)

    def combine(left, right):
        a1, b1 = left
        a2, b2 = right
        return a1 * a2, a2 * b1 + b2

    _, h = lax.associative_scan(combine, (a, u), axis=1)
    return (jax.nn.gelu(yg.astype(f32)) * h).astype(xr.dtype)


def compress_blocks(kv, pe, w1, w2):
    B, S, G, Dh = kv.shape
    sub = kv.reshape(B, S // CMP_STRIDE, CMP_STRIDE, G, Dh)
    blocks = jnp.concatenate([sub[:, :-1], sub[:, 1:]], axis=2)
    blocks = blocks + pe[None, None, :, None, :]
    hid = jax.nn.gelu(jnp.einsum('bnlgd,lde->bgne', blocks, w1))
    return jnp.einsum('bgne,ef->bgnf', hid, w2)


def nsa_group(q, kv6, gate_logits, pe_k, w1_k, w2_k, pe_v, w1_v, w2_v):
    B, S, H, Dh = q.shape
    G = NSA_KV_HEADS
    HPG = H // G
    f32 = jnp.float32
    kc_raw, vc_raw, ks, vs, kw, vw = [t.reshape(B, S, G, Dh) for t in kv6]
    k_cmp = compress_blocks(kc_raw, pe_k, w1_k, w2_k)
    v_cmp = compress_blocks(vc_raw, pe_v, w1_v, w2_v)
    n_cmp = k_cmp.shape[2]
    n_slc = S // SLC_BLOCK
    n_top = min(N_SELECT, n_slc)
    k_blk = ks.reshape(B, n_slc, SLC_BLOCK, G, Dh).transpose(0, 3, 1, 2, 4)
    v_blk = vs.reshape(B, n_slc, SLC_BLOCK, G, Dh).transpose(0, 3, 1, 2, 4)
    k_win = jnp.pad(kw, ((0, 0), (WINDOW, 0), (0, 0), (0, 0)))
    v_win = jnp.pad(vw, ((0, 0), (WINDOW, 0), (0, 0), (0, 0)))
    n_q = S // Q_BLOCK
    q_chunks = q.reshape(B, n_q, Q_BLOCK, G, HPG, Dh).transpose(1, 0, 3, 4, 2, 5)
    g_chunks = jax.nn.sigmoid(gate_logits.astype(f32)).reshape(B, n_q, Q_BLOCK, G, HPG, 3).transpose(1, 0, 3, 4, 2, 5)

    cmp_start = jnp.arange(n_cmp) * CMP_STRIDE
    cmp_end = cmp_start + CMP_BLOCK - 1
    blk = jnp.arange(n_slc)
    slc_start = blk * SLC_BLOCK
    cover = ((cmp_start[:, None] <= slc_start[None, :] + SLC_BLOCK - 1)
             & (cmp_end[:, None] >= slc_start[None, :])).astype(f32)
    b_idx = jnp.arange(B)[:, None, None, None]
    g_idx = jnp.arange(G)[None, :, None, None]
    scale = Dh ** -0.5

    def block_fn(args):
        ci, qc, gc = args
        q0 = ci * Q_BLOCK
        t = q0 + jnp.arange(Q_BLOCK)
        s_c = jnp.einsum('bghqd,bgnd->bghqn', qc, k_cmp).astype(f32) * scale
        p_c = masked_softmax(s_c, cmp_end[None, :] <= t[:, None])
        o_c = jnp.einsum('bghqn,bgnd->bghqd', p_c.astype(v_cmp.dtype), v_cmp)
        imp = jnp.einsum('bghqn,ns->bgqs', p_c, cover)
        cur = (t // SLC_BLOCK)[:, None]
        forced = (blk[None, :] == 0) | (blk[None, :] == cur) | (blk[None, :] == cur - 1)
        visible = slc_start[None, :] <= t[:, None]
        imp = jnp.where(forced, FORCE_SCORE, jnp.where(visible, imp, NEG_INF))
        _, idx = lax.top_k(imp, n_top)
        k_sel = k_blk[b_idx, g_idx, idx]
        v_sel = v_blk[b_idx, g_idx, idx]
        s_s = jnp.einsum('bghqd,bgqnkd->bghqnk', qc, k_sel).astype(f32) * scale
        kpos = idx[..., None] * SLC_BLOCK + jnp.arange(SLC_BLOCK)
        m_s = (kpos <= t[None, None, :, None, None]).reshape(B, G, 1, Q_BLOCK, -1)
        p_s = masked_softmax(s_s.reshape(B, G, HPG, Q_BLOCK, -1), m_s)
        o_s = jnp.einsum('bghqm,bgqmd->bghqd', p_s.astype(v_sel.dtype),
                         v_sel.reshape(B, G, Q_BLOCK, -1, Dh))
        k_w = lax.dynamic_slice_in_dim(k_win, q0, Q_BLOCK + WINDOW, axis=1)
        v_w = lax.dynamic_slice_in_dim(v_win, q0, Q_BLOCK + WINDOW, axis=1)
        s_w = jnp.einsum('bghqd,bkgd->bghqk', qc, k_w).astype(f32) * scale
        kp = q0 - WINDOW + jnp.arange(Q_BLOCK + WINDOW)
        m_w = (kp[None, :] <= t[:, None]) & (kp[None, :] > t[:, None] - WINDOW) & (kp[None, :] >= 0)
        p_w = masked_softmax(s_w, m_w)
        o_w = jnp.einsum('bghqk,bkgd->bghqd', p_w.astype(v_w.dtype), v_w)
        gc = gc.astype(f32)
        o = gc[..., 0:1] * o_c + gc[..., 1:2] * o_s + gc[..., 2:3] * o_w
        return o.astype(qc.dtype)

    out = lax.map(block_fn, (jnp.arange(n_q), q_chunks, g_chunks))
    return out.transpose(1, 0, 4, 2, 3, 5).reshape(B, S, H * Dh)


def memory_group(q, mem, w_mem_kv):
    B, S, _ = q.shape
    k, v = jnp.split(mem @ w_mem_kv, 2, axis=-1)
    k = k.reshape(B, -1, MEM_HEADS, MEM_HEAD_DIM)
    v = v.reshape(B, -1, MEM_HEADS, MEM_HEAD_DIM)
    qh = q.reshape(B, S, MEM_HEADS, MEM_HEAD_DIM)
    s = jnp.einsum('bshd,bmhd->bhsm', qh, k).astype(jnp.float32) * (MEM_HEAD_DIM ** -0.5)
    p = jax.nn.softmax(s, axis=-1).astype(v.dtype)
    return jnp.einsum('bhsm,bmhd->bshd', p, v).reshape(B, S, MEM_HEADS * MEM_HEAD_DIM)


def conv_ffn(h, w_up, conv_w, conv_b, w_down):
    up = causal_dwconv(h @ w_up, conv_w, conv_b)
    g, u = jnp.split(up, 2, axis=-1)
    return (jax.nn.gelu(g) * u) @ w_down


def hybrid_layer(h, mem, w_in, lru_conv_w, lru_conv_b, lru_wa, lru_ba, lru_wx, lru_bx, lru_lam,
                 cmp_pe_k, cmp_w1_k, cmp_w2_k, cmp_pe_v, cmp_w1_v, cmp_w2_v, w_mem_kv, w_out,
                 ln1_g, ln1_b, ffn_w_up, ffn_conv_w, ffn_conv_b, ffn_w_down, ln2_g, ln2_b):
    alpha = (2.0 * DEPTH) ** 0.25
    B, S, _ = h.shape
    G, Dh = NSA_KV_HEADS, NSA_HEAD_DIM
    sizes = [LRU_WIDTH, LRU_WIDTH, NSA_HEADS * Dh] + [G * Dh] * 6 + [3 * NSA_HEADS, MEM_HEADS * MEM_HEAD_DIM]
    parts = jnp.split(h @ w_in, np.cumsum(sizes)[:-1].tolist(), axis=-1)
    lru_x, lru_y, nsa_q = parts[0], parts[1], parts[2]
    nsa_kv = parts[3:9]
    nsa_gate, mem_q = parts[9], parts[10]
    o_lru = rg_lru_group(lru_x, lru_y, lru_conv_w, lru_conv_b, lru_wa, lru_ba, lru_wx, lru_bx, lru_lam)
    o_nsa = nsa_group(nsa_q.reshape(B, S, NSA_HEADS, Dh), nsa_kv, nsa_gate.reshape(B, S, NSA_HEADS, 3),
                      cmp_pe_k, cmp_w1_k, cmp_w2_k, cmp_pe_v, cmp_w1_v, cmp_w2_v)
    o_mem = memory_group(mem_q, mem, w_mem_kv)
    mixed = jnp.concatenate([o_lru, o_nsa.astype(o_lru.dtype), o_mem.astype(o_lru.dtype)], axis=-1) @ w_out
    h = layer_norm(alpha * h + mixed, ln1_g, ln1_b)
    h = layer_norm(alpha * h + conv_ffn(h, ffn_w_up, ffn_conv_w, ffn_conv_b, ffn_w_down), ln2_g, ln2_b)
    return h


def setup_inputs(seed: int = 0) -> dict:
    key = jax.random.key(seed)
    ks = jax.random.split(key, 32)
    f32 = jnp.float32
    L = DEPTH
    beta = (8.0 * DEPTH) ** -0.25
    G, Dh = NSA_KV_HEADS, NSA_HEAD_DIM
    n_in = 2 * LRU_WIDTH + NSA_HEADS * Dh + 6 * G * Dh + 3 * NSA_HEADS + MEM_HEADS * MEM_HEAD_DIM
    mix_width = LRU_WIDTH + NSA_HEADS * Dh + MEM_HEADS * MEM_HEAD_DIM

    def nrm(k, shape, scale):
        return jax.random.normal(k, shape, f32) * scale

    u = jax.random.uniform(ks[7], (L, LRU_WIDTH), f32, 0.9, 0.999)
    s = u ** (1.0 / LRU_C)
    lru_lam = jnp.log(s) - jnp.log1p(-s)
    return {
        'x': nrm(ks[0], (BATCH, SEQ, D_MODEL), 1.0),
        'mem': nrm(ks[1], (BATCH, MEM_LEN, D_MODEL), 1.0),
        'ln_in_g': 1.0 + nrm(ks[2], (D_MODEL,), 0.02),
        'ln_in_b': nrm(ks[3], (D_MODEL,), 0.02),
        'w_in': nrm(ks[4], (L, D_MODEL, n_in), D_MODEL ** -0.5),
        'lru_conv_w': nrm(ks[5], (L, LRU_CONV_WIDTH, LRU_WIDTH), LRU_CONV_WIDTH ** -0.5),
        'lru_conv_b': nrm(ks[6], (L, LRU_WIDTH), 0.01),
        'lru_wa': nrm(ks[8], (L, LRU_BLOCKS, LRU_BLOCK_DIM, LRU_BLOCK_DIM), LRU_BLOCK_DIM ** -0.5),
        'lru_ba': nrm(ks[9], (L, LRU_BLOCKS, LRU_BLOCK_DIM), 0.01),
        'lru_wx': nrm(ks[10], (L, LRU_BLOCKS, LRU_BLOCK_DIM, LRU_BLOCK_DIM), LRU_BLOCK_DIM ** -0.5),
        'lru_bx': nrm(ks[11], (L, LRU_BLOCKS, LRU_BLOCK_DIM), 0.01),
        'lru_lam': lru_lam,
        'cmp_pe_k': nrm(ks[12], (L, CMP_BLOCK, Dh), 0.02),
        'cmp_w1_k': nrm(ks[13], (L, CMP_BLOCK, Dh, Dh), (CMP_BLOCK * Dh) ** -0.5),
        'cmp_w2_k': nrm(ks[14], (L, Dh, Dh), Dh ** -0.5),
        'cmp_pe_v': nrm(ks[15], (L, CMP_BLOCK, Dh), 0.02),
        'cmp_w1_v': nrm(ks[16], (L, CMP_BLOCK, Dh, Dh), (CMP_BLOCK * Dh) ** -0.5),
        'cmp_w2_v': nrm(ks[17], (L, Dh, Dh), Dh ** -0.5),
        'w_mem_kv': nrm(ks[18], (L, D_MODEL, 2 * MEM_HEADS * MEM_HEAD_DIM), D_MODEL ** -0.5),
        'w_out': nrm(ks[19], (L, mix_width, D_MODEL), beta * mix_width ** -0.5),
        'ln1_g': 1.0 + nrm(ks[20], (L, D_MODEL), 0.02),
        'ln1_b': nrm(ks[21], (L, D_MODEL), 0.02),
        'ffn_w_up': nrm(ks[22], (L, D_MODEL, 2 * D_FF), D_MODEL ** -0.5),
        'ffn_conv_w': nrm(ks[23], (L, FFN_CONV_WIDTH, 2 * D_FF), FFN_CONV_WIDTH ** -0.5),
        'ffn_conv_b': nrm(ks[24], (L, 2 * D_FF), 0.01),
        'ffn_w_down': nrm(ks[25], (L, D_FF, D_MODEL), beta * D_FF ** -0.5),
        'ln2_g': 1.0 + nrm(ks[26], (L, D_MODEL), 0.02),
        'ln2_b': nrm(ks[27], (L, D_MODEL), 0.02),
    }


def reference(x, mem, ln_in_g, ln_in_b, w_in, lru_conv_w, lru_conv_b, lru_wa, lru_ba, lru_wx, lru_bx,
              lru_lam, cmp_pe_k, cmp_w1_k, cmp_w2_k, cmp_pe_v, cmp_w1_v, cmp_w2_v, w_mem_kv, w_out,
              ln1_g, ln1_b, ffn_w_up, ffn_conv_w, ffn_conv_b, ffn_w_down, ln2_g, ln2_b):
    h = layer_norm(x, ln_in_g, ln_in_b)
    for l in range(DEPTH):
        h = hybrid_layer(h, mem, w_in[l], lru_conv_w[l], lru_conv_b[l], lru_wa[l], lru_ba[l],
                         lru_wx[l], lru_bx[l], lru_lam[l], cmp_pe_k[l], cmp_w1_k[l], cmp_w2_k[l],
                         cmp_pe_v[l], cmp_w1_v[l], cmp_w2_v[l], w_mem_kv[l], w_out[l],
                         ln1_g[l], ln1_b[l], ffn_w_up[l], ffn_conv_w[l], ffn_conv_b[l],
                         ffn_w_down[l], ln2_g[l], ln2_b[l])
    return h
```

```python
import functools
import math

import jax
import jax.numpy as jnp
from jax import lax
from jax.experimental import pallas as pl
from jax.experimental.pallas import tpu as pltpu

f32 = jnp.float32
bf16 = jnp.bfloat16

D_MODEL = 2048
LRU_WIDTH = 1024
LRU_BLOCKS = 8
LRU_BLOCK_DIM = 128
LRU_CONV_WIDTH = 4
LRU_C = 8.0
NSA_HEADS = 8
NSA_KV_HEADS = 2
NSA_HPG = NSA_HEADS // NSA_KV_HEADS
NSA_HEAD_DIM = 64
CMP_STRIDE = 16
CMP_BLOCK = 32
SLC_BLOCK = 64
N_SELECT = 16
WINDOW = 512
Q_BLOCK = 128
MEM_HEADS = 4
MEM_HEAD_DIM = 128
D_FF = 5632
FFN_CONV_WIDTH = 3
LN_EPS = 1e-5
NEG_INF = -1e30
FORCE_SCORE = 1e9
TAKEN_SCORE = -3e38
ALPHA = 2.0 ** 0.25

V7X_VMEM_LIMIT_BYTES = 56 * 1024 * 1024

_TQ0, _TVS0, _TVW0, _TMQ0, _TG0, _TROWS = 0, 512, 640, 768, 1280, 1312
_NX0, _NY0, _NKC0, _NVC0, _NKS0, _NKW0, _NCOLS = 0, 1024, 2048, 2176, 2304, 2432, 2560

SEL_CHUNK = 512
WIN_KEYS = WINDOW + Q_BLOCK


def _ln(x, g, b):
    mu = jnp.mean(x, axis=-1, keepdims=True)
    xc = x - mu
    var = jnp.mean(xc * xc, axis=-1, keepdims=True)
    return xc * lax.rsqrt(var + LN_EPS) * g + b


def _gelu(x):
    return jax.nn.gelu(x)


def _sigmoid(x):
    return 1.0 / (1.0 + jnp.exp(-x))


def _nt(a, b):
    return lax.dot_general(a, b, (((1,), (1,)), ((), ())), preferred_element_type=f32)


def _tn(a, b):
    return lax.dot_general(a, b, (((0,), (0,)), ((), ())), preferred_element_type=f32)


def _params(*sem):
    return pltpu.CompilerParams(dimension_semantics=sem, vmem_limit_bytes=V7X_VMEM_LIMIT_BYTES)


def _const_spec(shape):
    nd = len(shape)
    return pl.BlockSpec(shape, lambda *_: (0,) * nd, pipeline_mode=pl.Buffered(1))


def _proj_kernel(x_ref, g_ref, b_ref, wn_ref, wt_ref,
                 lx_ref, ly_ref, kc_ref, vc_ref, ks_ref, kw_ref,
                 qT_ref, vsT_ref, vwT_ref, mqT_ref, gT_ref):
    h = _ln(x_ref[...], g_ref[...], b_ref[...]).astype(bf16)

    def nat(c0, c1):
        return jnp.dot(h, wn_ref[:, c0:c1], preferred_element_type=f32)

    lx_ref[...] = nat(_NX0, _NY0)
    ly_ref[...] = nat(_NY0, _NKC0)
    kc_ref[...] = nat(_NKC0, _NVC0)
    vc_ref[...] = nat(_NVC0, _NKS0)
    ks_ref[...] = nat(_NKS0, _NKW0).astype(bf16)
    kw_ref[...] = nat(_NKW0, _NCOLS).astype(bf16)

    def tr(r0, r1):
        return _nt(wt_ref[r0:r1, :], h)

    qT_ref[...] = (tr(_TQ0, _TVS0) * (NSA_HEAD_DIM ** -0.5)).astype(bf16)
    vsT_ref[...] = tr(_TVS0, _TVW0).astype(bf16)
    vwT_ref[...] = tr(_TVW0, _TMQ0).astype(bf16)
    mqT_ref[...] = tr(_TMQ0, _TG0).astype(bf16)
    gT_ref[...] = tr(_TG0, _TROWS)


def _proj(x2, g, b, wn, wt, tm):
    T = x2.shape[0]
    nat = lambda w: pl.BlockSpec((tm, w), lambda i: (i, 0))
    trs = lambda r: pl.BlockSpec((r, tm), lambda i: (0, i))
    return pl.pallas_call(
        _proj_kernel,
        grid=(T // tm,),
        in_specs=[pl.BlockSpec((tm, D_MODEL), lambda i: (i, 0)),
                  _const_spec((1, D_MODEL)), _const_spec((1, D_MODEL)),
                  _const_spec((D_MODEL, _NCOLS)), _const_spec((_TROWS, D_MODEL))],
        out_specs=[nat(1024), nat(1024), nat(128), nat(128), nat(128), nat(128),
                   trs(512), trs(128), trs(128), trs(512), trs(32)],
        out_shape=[jax.ShapeDtypeStruct((T, 1024), f32), jax.ShapeDtypeStruct((T, 1024), f32),
                   jax.ShapeDtypeStruct((T, 128), f32), jax.ShapeDtypeStruct((T, 128), f32),
                   jax.ShapeDtypeStruct((T, 128), bf16), jax.ShapeDtypeStruct((T, 128), bf16),
                   jax.ShapeDtypeStruct((512, T), bf16), jax.ShapeDtypeStruct((128, T), bf16),
                   jax.ShapeDtypeStruct((128, T), bf16), jax.ShapeDtypeStruct((512, T), bf16),
                   jax.ShapeDtypeStruct((32, T), f32)],
        compiler_params=_params("parallel"),
        name="proj",
    )(x2, g, b, wn, wt)


def _lru_kernel(x_ref, y_ref, cw_ref, cb_ref, wax_ref, ba_ref, bx_ref, lam_ref, o_ref,
                xext_ref, a_ref, u_ref, h_ref, carry_ref, *, tb):
    s = pl.program_id(1)

    @pl.when(s == 0)
    def _():
        xext_ref[0:8, :] = jnp.zeros((8, LRU_WIDTH), f32)
        carry_ref[...] = jnp.zeros((8, LRU_WIDTH), f32)

    x = x_ref[0]
    xext_ref[8:8 + tb, :] = x
    xc = cb_ref[...]
    for k in range(LRU_CONV_WIDTH):
        off = 8 - (LRU_CONV_WIDTH - 1) + k
        xc = xc + xext_ref[off:off + tb, :] * cw_ref[k:k + 1, :]
    xext_ref[0:8, :] = x[tb - 8:tb, :]

    xb = xc.astype(bf16)
    gates = [jnp.dot(xb[:, n * 128:(n + 1) * 128], wax_ref[n], preferred_element_type=f32)
             for n in range(LRU_BLOCKS)]
    r = _sigmoid(jnp.concatenate([gt[:, 0:128] for gt in gates], axis=1) + ba_ref[...])
    i = _sigmoid(jnp.concatenate([gt[:, 128:256] for gt in gates], axis=1) + bx_ref[...])
    nl = -lam_ref[...]
    softplus = jnp.maximum(nl, 0.0) + jnp.log1p(jnp.exp(-jnp.abs(nl)))
    log_a = (-LRU_C) * r * softplus
    a = jnp.exp(log_a)
    th = jnp.tanh(log_a)
    u = jnp.sqrt(-2.0 * th / (1.0 - th)) * (i * xc)

    row = lax.broadcasted_iota(jnp.int32, (tb, LRU_WIDTH), 0) & 7
    for sh in (1, 2, 4):
        a_sh = pltpu.roll(a, sh, 0)
        u_sh = pltpu.roll(u, sh, 0)
        m = row >= sh
        u = jnp.where(m, a * u_sh + u, u)
        a = jnp.where(m, a * a_sh, a)
    a_ref[...] = a
    u_ref[...] = u

    def body(k, c):
        r0 = pl.multiple_of(k * 8, 8)
        hk = a_ref[pl.ds(r0, 8), :] * c + u_ref[pl.ds(r0, 8), :]
        h_ref[pl.ds(r0, 8), :] = hk
        return jnp.broadcast_to(hk[7:8, :], (8, LRU_WIDTH))

    carry_ref[...] = lax.fori_loop(0, tb // 8, body, carry_ref[...], unroll=4)
    o_ref[0] = (_gelu(y_ref[0]) * h_ref[...]).astype(bf16)


def _lru(lx, ly, cw, cb, wax, ba, bx, lam, tb):
    B, S, _ = lx.shape
    blk = pl.BlockSpec((1, tb, LRU_WIDTH), lambda b, s: (b, s, 0))
    return pl.pallas_call(
        functools.partial(_lru_kernel, tb=tb),
        grid=(B, S // tb),
        in_specs=[blk, blk, _const_spec((LRU_CONV_WIDTH, LRU_WIDTH)), _const_spec((1, LRU_WIDTH)),
                  _const_spec((LRU_BLOCKS, 128, 256)), _const_spec((1, LRU_WIDTH)),
                  _const_spec((1, LRU_WIDTH)), _const_spec((1, LRU_WIDTH))],
        out_specs=blk,
        out_shape=jax.ShapeDtypeStruct((B, S, LRU_WIDTH), bf16),
        scratch_shapes=[pltpu.VMEM((tb + 8, LRU_WIDTH), f32), pltpu.VMEM((tb, LRU_WIDTH), f32),
                        pltpu.VMEM((tb, LRU_WIDTH), f32), pltpu.VMEM((tb, LRU_WIDTH), f32),
                        pltpu.VMEM((8, LRU_WIDTH), f32)],
        compiler_params=_params("parallel", "arbitrary"),
        name="lru",
    )(lx, ly, cw, cb, wax, ba, bx, lam)


def _cmp_kernel(kc_ref, vc_ref, pek_ref, w1k_ref, w2k_ref, pev_ref, w1v_ref, w2vT_ref,
                kcmp_ref, vcmpT_ref):
    nsub = kc_ref.shape[1]

    def hidden(sub_ref, pe_ref, w1_ref):
        sub = sub_ref[0]
        y0 = jnp.dot((sub + pe_ref[0:1, :]).astype(bf16), w1_ref[0], preferred_element_type=f32)
        y1 = jnp.dot((sub + pe_ref[1:2, :]).astype(bf16), w1_ref[1], preferred_element_type=f32)
        return _gelu(y0 + pltpu.roll(y1, nsub - 1, 0)).astype(bf16)

    kcmp_ref[0] = jnp.dot(hidden(kc_ref, pek_ref, w1k_ref), w2k_ref[...],
                          preferred_element_type=f32).astype(bf16)
    vcmpT_ref[0] = _nt(w2vT_ref[...], hidden(vc_ref, pev_ref, w1v_ref)).astype(bf16)


def _compress(kc, vc, pek, w1k, w2k, pev, w1v, w2vT):
    B, nsub, width = kc.shape
    blk = pl.BlockSpec((1, nsub, width), lambda b: (b, 0, 0))
    return pl.pallas_call(
        _cmp_kernel,
        grid=(B,),
        in_specs=[blk, blk, _const_spec((2, width)), _const_spec((2, width, 128)), _const_spec((128, 128)),
                  _const_spec((2, width)), _const_spec((2, width, 128)), _const_spec((128, 128))],
        out_specs=[pl.BlockSpec((1, nsub, 128), lambda b: (b, 0, 0)),
                   pl.BlockSpec((1, 128, nsub), lambda b: (b, 0, 0))],
        out_shape=[jax.ShapeDtypeStruct((B, nsub, 128), bf16), jax.ShapeDtypeStruct((B, 128, nsub), bf16)],
        compiler_params=_params("parallel"),
        name="nsa_compress",
    )(kc, vc, pek, w1k, w2k, pev, w1v, w2vT)


def _nsa_kernel(qT_ref, gT_ref, kcmp_ref, vcmpT_ref, ks_ref, vsT_ref, kw_ref, vwT_ref, covT_ref,
                o_ref, sel_ref, *, n_slc, n_top):
    g = pl.program_id(1)
    i = pl.program_id(2)
    q0 = i * Q_BLOCK
    grow = pl.multiple_of(g * NSA_HEAD_DIM, NSA_HEAD_DIM)
    HQ = NSA_HPG * Q_BLOCK

    qh = jnp.concatenate([qT_ref[hp * 64:(hp + 1) * 64, :] for hp in range(NSA_HPG)], axis=1)
    zq = jnp.zeros_like(qh)
    qpad = jnp.concatenate([jnp.where(g == 0, qh, zq), jnp.where(g == 1, qh, zq)], axis=0)
    t_row = q0 + lax.broadcasted_iota(jnp.int32, (1, Q_BLOCK), 1)
    t_all = q0 + (lax.broadcasted_iota(jnp.int32, (1, HQ), 1) & (Q_BLOCK - 1))

    ncmp = kcmp_ref.shape[1]
    sc = jnp.dot(kcmp_ref[0], qpad, preferred_element_type=f32)
    n_iota = lax.broadcasted_iota(jnp.int32, (ncmp, HQ), 0)
    valid = (n_iota * CMP_STRIDE + (CMP_BLOCK - 1)) <= t_all
    sm = jnp.where(valid, sc, NEG_INF)
    mc = jnp.max(sm, axis=0, keepdims=True)
    pc = jnp.where(valid, jnp.exp(sm - mc), 0.0)
    lc = jnp.sum(pc, axis=0, keepdims=True)
    pc = pc * jnp.where(lc > 0.0, 1.0 / lc, 0.0)
    oc = jnp.dot(vcmpT_ref[0, pl.ds(grow, 64), :], pc.astype(bf16), preferred_element_type=f32)

    psum = pc[:, 0:Q_BLOCK]
    for hp in range(1, NSA_HPG):
        psum = psum + pc[:, hp * Q_BLOCK:(hp + 1) * Q_BLOCK]
    hi = psum.astype(bf16)
    r1 = psum - hi.astype(f32)
    mid = r1.astype(bf16)
    lo = (r1 - mid.astype(f32)).astype(bf16)
    cov = covT_ref[...]
    imp = (jnp.dot(cov, hi, preferred_element_type=f32) + jnp.dot(cov, mid, preferred_element_type=f32)
           + jnp.dot(cov, lo, preferred_element_type=f32))
    j_i = lax.broadcasted_iota(jnp.int32, (n_slc, Q_BLOCK), 0)
    j_f = j_i.astype(f32)
    cur = t_row >> 6
    forced = (j_i == 0) | (j_i == cur) | (j_i == cur - 1)
    visible = (j_i * SLC_BLOCK) <= t_row
    score = jnp.where(forced, FORCE_SCORE, jnp.where(visible, imp, NEG_INF))
    sel = jnp.zeros((n_slc, Q_BLOCK), f32)
    for _ in range(n_top):
        mx = jnp.max(score, axis=0, keepdims=True)
        first = jnp.min(jnp.where(score == mx, j_f, float(n_slc)), axis=0, keepdims=True)
        hit = j_f == first
        sel = jnp.where(hit, 1.0, sel)
        score = jnp.where(hit, TAKEN_SCORE, score)
    sel_ref[...] = sel

    row_c = lax.broadcasted_iota(jnp.int32, (SEL_CHUNK, HQ), 0)
    blocks_per_chunk = SEL_CHUNK // SLC_BLOCK

    def sel_body(c, carry):
        m, l, acc = carry
        k0 = pl.multiple_of(c * SEL_CHUNK, SEL_CHUNK)
        s = jnp.dot(ks_ref[0, pl.ds(k0, SEL_CHUNK), :], qpad, preferred_element_type=f32)
        rows = sel_ref[pl.ds(pl.multiple_of(c * blocks_per_chunk, blocks_per_chunk), blocks_per_chunk), :]
        rows = jnp.concatenate([rows] * NSA_HPG, axis=1)
        chosen = jnp.concatenate([jnp.broadcast_to(rows[jj:jj + 1, :], (SLC_BLOCK, HQ))
                                  for jj in range(blocks_per_chunk)], axis=0)
        mask = (chosen > 0.0) & ((k0 + row_c) <= t_all)
        s = jnp.where(mask, s, NEG_INF)
        m_new = jnp.maximum(m, jnp.max(s, axis=0, keepdims=True))
        scale = jnp.exp(m - m_new)
        p = jnp.exp(s - m_new)
        l = scale * l + jnp.sum(p, axis=0, keepdims=True)
        pv = jnp.dot(vsT_ref[pl.ds(grow, 64), pl.ds(k0, SEL_CHUNK)], p.astype(bf16),
                     preferred_element_type=f32)
        return m_new, l, scale * acc + pv

    n_chunks = (q0 + Q_BLOCK + SEL_CHUNK - 1) // SEL_CHUNK
    _, ls, accs = lax.fori_loop(
        0, n_chunks, sel_body,
        (jnp.full((1, HQ), NEG_INF, f32), jnp.zeros((1, HQ), f32), jnp.zeros((NSA_HEAD_DIM, HQ), f32)))
    osel = accs * (1.0 / ls)

    w0 = pl.multiple_of(jnp.maximum(q0 - WINDOW, 0), Q_BLOCK)
    sw = jnp.dot(kw_ref[0, pl.ds(w0, WIN_KEYS), :], qpad, preferred_element_type=f32)
    kp = w0 + lax.broadcasted_iota(jnp.int32, (WIN_KEYS, HQ), 0)
    maskw = (kp <= t_all) & (kp > (t_all - WINDOW))
    sw = jnp.where(maskw, sw, NEG_INF)
    pw = jnp.exp(sw - jnp.max(sw, axis=0, keepdims=True))
    lw = jnp.sum(pw, axis=0, keepdims=True)
    ow = jnp.dot(vwT_ref[pl.ds(grow, 64), pl.ds(w0, WIN_KEYS)], pw.astype(bf16),
                 preferred_element_type=f32) * (1.0 / lw)

    gate = _sigmoid(gT_ref[...])
    for hp in range(NSA_HPG):
        cols = slice(hp * Q_BLOCK, (hp + 1) * Q_BLOCK)
        o = (gate[hp:hp + 1, :] * oc[:, cols] + gate[4 + hp:5 + hp, :] * osel[:, cols]
             + gate[8 + hp:9 + hp, :] * ow[:, cols])
        o_ref[hp * 64:(hp + 1) * 64, :] = o.astype(bf16)


def _nsa(qT, gT, kcmp, vcmpT, ks, vsT, kw, vwT, covT, B, S):
    n_q = S // Q_BLOCK
    n_slc = S // SLC_BLOCK
    ncmp = S // CMP_STRIDE
    tok = lambda r: pl.BlockSpec((r, Q_BLOCK), lambda b, g, i: (g, b * n_q + i))
    per_b_nat = pl.BlockSpec((1, S, 128), lambda b, g, i: (b, 0, 0))
    per_b_tr = pl.BlockSpec((128, S), lambda b, g, i: (0, b))
    return pl.pallas_call(
        functools.partial(_nsa_kernel, n_slc=n_slc, n_top=min(N_SELECT, n_slc)),
        grid=(B, NSA_KV_HEADS, n_q),
        in_specs=[tok(256), tok(16),
                  pl.BlockSpec((1, ncmp, 128), lambda b, g, i: (b, 0, 0)),
                  pl.BlockSpec((1, 128, ncmp), lambda b, g, i: (b, 0, 0)),
                  per_b_nat, per_b_tr, per_b_nat, per_b_tr,
                  pl.BlockSpec((n_slc, ncmp), lambda b, g, i: (0, 0))],
        out_specs=tok(256),
        out_shape=jax.ShapeDtypeStruct((512, B * S), bf16),
        scratch_shapes=[pltpu.VMEM((n_slc, Q_BLOCK), f32)],
        compiler_params=_params("parallel", "parallel", "arbitrary"),
        name="nsa_attn",
    )(qT, gT, kcmp, vcmpT, ks, vsT, kw, vwT, covT)


def _memkv_kernel(mem_ref, wk_ref, wvT_ref, k_ref, vT_ref):
    m = mem_ref[0].astype(bf16)
    k_ref[0] = jnp.dot(m, wk_ref[...], preferred_element_type=f32).astype(bf16)
    vT_ref[0] = _nt(wvT_ref[...], m).astype(bf16)


def _memkv(mem, wk, wvT):
    B, M, _ = mem.shape
    return pl.pallas_call(
        _memkv_kernel,
        grid=(B,),
        in_specs=[pl.BlockSpec((1, M, D_MODEL), lambda b: (b, 0, 0)),
                  _const_spec((D_MODEL, 512)), _const_spec((512, D_MODEL))],
        out_specs=[pl.BlockSpec((1, M, 512), lambda b: (b, 0, 0)),
                   pl.BlockSpec((1, 512, M), lambda b: (b, 0, 0))],
        out_shape=[jax.ShapeDtypeStruct((B, M, 512), bf16), jax.ShapeDtypeStruct((B, 512, M), bf16)],
        compiler_params=_params("parallel"),
        name="mem_kv",
    )(mem, wk, wvT)


def _memattn_kernel(qT_ref, k_ref, vT_ref, o_ref):
    for h in range(MEM_HEADS):
        rows = slice(h * MEM_HEAD_DIM, (h + 1) * MEM_HEAD_DIM)
        s = jnp.dot(k_ref[0, :, rows], qT_ref[rows, :], preferred_element_type=f32) * (MEM_HEAD_DIM ** -0.5)
        p = jnp.exp(s - jnp.max(s, axis=0, keepdims=True))
        l = jnp.sum(p, axis=0, keepdims=True)
        o = jnp.dot(vT_ref[0, rows, :], p.astype(bf16), preferred_element_type=f32) * (1.0 / l)
        o_ref[rows, :] = o.astype(bf16)


def _memattn(mqT, mk, mvT, B, S, tq):
    M = mk.shape[1]
    nq = S // tq
    return pl.pallas_call(
        _memattn_kernel,
        grid=(B, nq),
        in_specs=[pl.BlockSpec((512, tq), lambda b, i: (0, b * nq + i)),
                  pl.BlockSpec((1, M, 512), lambda b, i: (b, 0, 0)),
                  pl.BlockSpec((1, 512, M), lambda b, i: (b, 0, 0))],
        out_specs=pl.BlockSpec((512, tq), lambda b, i: (0, b * nq + i)),
        out_shape=jax.ShapeDtypeStruct((512, B * S), bf16),
        compiler_params=_params("parallel", "parallel"),
        name="mem_attn",
    )(mqT, mk, mvT)


def _outproj_kernel(x_ref, lng_ref, lnb_ref, ol_ref, onT_ref, omT_ref, wo_ref, g1_ref, b1_ref, h1_ref):
    h0 = _ln(x_ref[...], lng_ref[...], lnb_ref[...])
    mixed = jnp.dot(ol_ref[...], wo_ref[0:1024, :], preferred_element_type=f32)
    mixed = mixed + _tn(onT_ref[...], wo_ref[1024:1536, :])
    mixed = mixed + _tn(omT_ref[...], wo_ref[1536:2048, :])
    h1_ref[...] = _ln(ALPHA * h0 + mixed, g1_ref[...], b1_ref[...])


def _outproj(x2, lng, lnb, ol, onT, omT, wo, g1, b1, tm):
    T = x2.shape[0]
    row = lambda w: pl.BlockSpec((tm, w), lambda i: (i, 0))
    trs = pl.BlockSpec((512, tm), lambda i: (0, i))
    vec = _const_spec((1, D_MODEL))
    return pl.pallas_call(
        _outproj_kernel,
        grid=(T // tm,),
        in_specs=[row(D_MODEL), vec, vec, row(1024), trs, trs, _const_spec((D_MODEL, D_MODEL)), vec, vec],
        out_specs=row(D_MODEL),
        out_shape=jax.ShapeDtypeStruct((T, D_MODEL), f32),
        compiler_params=_params("parallel"),
        name="out_proj_ln1",
    )(x2, lng, lnb, ol, onT, omT, wo, g1, b1)


def _ffn_kernel(h_ref, halo_ref, wg_ref, wu_ref, cwg_ref, cwu_ref, cbg_ref, cbu_ref, wd_ref,
                g2_ref, b2_ref, o_ref, lhs_ref, acc_ref, *, tm, blocks_per_seq):
    i = pl.program_id(0)
    j = pl.program_id(1)

    @pl.when(j == 0)
    def _():
        first = (i % blocks_per_seq) == 0
        lhs_ref[0:8, :] = jnp.where(first, 0.0, halo_ref[...]).astype(bf16)
        lhs_ref[8:8 + tm, :] = h_ref[...].astype(bf16)
        acc_ref[...] = jnp.zeros_like(acc_ref)

    lhs = lhs_ref[...]

    def conv(w_ref, cw_ref, cb_ref):
        up = jnp.dot(lhs, w_ref[...], preferred_element_type=f32)
        y = cb_ref[...] + pltpu.roll(up, 2, 0)[8:8 + tm, :] * cw_ref[0:1, :]
        y = y + pltpu.roll(up, 1, 0)[8:8 + tm, :] * cw_ref[1:2, :]
        return y + up[8:8 + tm, :] * cw_ref[2:3, :]

    act = (_gelu(conv(wg_ref, cwg_ref, cbg_ref)) * conv(wu_ref, cwu_ref, cbu_ref)).astype(bf16)
    acc_ref[...] += jnp.dot(act, wd_ref[...], preferred_element_type=f32)

    @pl.when(j == pl.num_programs(1) - 1)
    def _():
        o_ref[...] = _ln(ALPHA * h_ref[...] + acc_ref[...], g2_ref[...], b2_ref[...])


def _ffn(h1, wup, cw, cb, wd, g2, b2, tm, tf, blocks_per_seq):
    T = h1.shape[0]
    nf = D_FF // tf
    vec = _const_spec((1, D_MODEL))
    return pl.pallas_call(
        functools.partial(_ffn_kernel, tm=tm, blocks_per_seq=blocks_per_seq),
        grid=(T // tm, nf),
        in_specs=[pl.BlockSpec((tm, D_MODEL), lambda i, j: (i, 0)),
                  pl.BlockSpec((8, D_MODEL), lambda i, j: (jnp.maximum(i * (tm // 8) - 1, 0), 0)),
                  pl.BlockSpec((D_MODEL, tf), lambda i, j: (0, j)),
                  pl.BlockSpec((D_MODEL, tf), lambda i, j: (0, nf + j)),
                  pl.BlockSpec((FFN_CONV_WIDTH, tf), lambda i, j: (0, j)),
                  pl.BlockSpec((FFN_CONV_WIDTH, tf), lambda i, j: (0, nf + j)),
                  pl.BlockSpec((1, tf), lambda i, j: (0, j)),
                  pl.BlockSpec((1, tf), lambda i, j: (0, nf + j)),
                  pl.BlockSpec((tf, D_MODEL), lambda i, j: (j, 0)),
                  vec, vec],
        out_specs=pl.BlockSpec((tm, D_MODEL), lambda i, j: (i, 0)),
        out_shape=jax.ShapeDtypeStruct((T, D_MODEL), f32),
        scratch_shapes=[pltpu.VMEM((tm + 8, D_MODEL), bf16), pltpu.VMEM((tm, D_MODEL), f32)],
        compiler_params=_params("parallel", "arbitrary"),
        name="conv_ffn_ln2",
    )(h1, h1, wup, wup, cw, cw, cb, cb, wd, g2, b2)


def _layer(x, mem, ln_in_g, ln_in_b, w_in, lru_conv_w, lru_conv_b, lru_wa, lru_ba, lru_wx, lru_bx,
           lru_lam, cmp_pe_k, cmp_w1_k, cmp_w2_k, cmp_pe_v, cmp_w1_v, cmp_w2_v, w_mem_kv, w_out,
           ln1_g, ln1_b, ffn_w_up, ffn_conv_w, ffn_conv_b, ffn_w_down, ln2_g, ln2_b,
           *, tm_proj, tb_lru, tq_mem, tm_out, tm_ffn, tf_ffn):
    B, S, _ = x.shape
    T = B * S
    G, Dh = NSA_KV_HEADS, NSA_HEAD_DIM
    row = lambda v: v.reshape(1, -1)

    c_q = 2 * LRU_WIDTH
    c_kv = c_q + NSA_HEADS * Dh
    c_gate = c_kv + 6 * G * Dh
    c_mq = c_gate + 3 * NSA_HEADS
    kv_cols = [w_in[:, c_kv + n * G * Dh: c_kv + (n + 1) * G * Dh] for n in range(6)]
    wn = jnp.concatenate([w_in[:, :c_q], kv_cols[0], kv_cols[1], kv_cols[2], kv_cols[4]], axis=1).astype(bf16)
    w_gate = w_in[:, c_gate:c_mq].reshape(D_MODEL, G, NSA_HPG, 3)
    w_gate = jnp.pad(w_gate.transpose(1, 3, 2, 0), ((0, 0), (0, 1), (0, 0), (0, 0)))
    wt = jnp.concatenate([w_in[:, c_q:c_kv].T, kv_cols[3].T, kv_cols[5].T, w_in[:, c_mq:].T,
                          w_gate.reshape(2 * 16, D_MODEL)], axis=0).astype(bf16)

    wax = jnp.concatenate([lru_wa, lru_wx], axis=-1).astype(bf16)

    def cmp_weights(pe, w1, w2):
        pe2 = jnp.broadcast_to(pe.reshape(2, CMP_STRIDE, 1, Dh), (2, CMP_STRIDE, G, Dh)).reshape(2, -1)
        eye = jnp.eye(G, dtype=f32)
        w1e = jnp.einsum('hlde,gk->hlgdke', w1.reshape(2, CMP_STRIDE, Dh, Dh), eye)
        w1e = w1e.reshape(2, CMP_STRIDE * G * Dh, G * Dh).astype(bf16)
        w2e = jnp.einsum('ef,gk->gekf', w2, eye).reshape(G * Dh, G * Dh)
        return pe2, w1e, w2e

    pek, w1k, w2k = cmp_weights(cmp_pe_k, cmp_w1_k, cmp_w2_k)
    pev, w1v, w2v = cmp_weights(cmp_pe_v, cmp_w1_v, cmp_w2_v)

    n_slc, ncmp = S // SLC_BLOCK, S // CMP_STRIDE
    ci = jnp.arange(ncmp)[None, :] * CMP_STRIDE
    sj = jnp.arange(n_slc)[:, None] * SLC_BLOCK
    covT = ((ci <= sj + SLC_BLOCK - 1) & (ci + CMP_BLOCK - 1 >= sj)).astype(bf16)

    wk_mem = w_mem_kv[:, :512].astype(bf16)
    wvT_mem = w_mem_kv[:, 512:].T.astype(bf16)
    wo = w_out.astype(bf16)
    wup = ffn_w_up.astype(bf16)
    wd = ffn_w_down.astype(bf16)

    x2 = x.reshape(T, D_MODEL)
    lx, ly, kc, vc, ks, kw, qT, vsT, vwT, mqT, gT = _proj(x2, row(ln_in_g), row(ln_in_b), wn, wt, tm_proj)

    o_lru = _lru(lx.reshape(B, S, LRU_WIDTH), ly.reshape(B, S, LRU_WIDTH), lru_conv_w, row(lru_conv_b),
                 wax, row(lru_ba), row(lru_bx), row(lru_lam), tb_lru)

    sub = lambda a: a.reshape(B, ncmp, CMP_STRIDE * G * Dh)
    kcmp, vcmpT = _compress(sub(kc), sub(vc), pek, w1k, w2k.astype(bf16), pev, w1v, w2v.T.astype(bf16))
    o_nsaT = _nsa(qT, gT, kcmp, vcmpT, ks.reshape(B, S, 128), vsT, kw.reshape(B, S, 128), vwT, covT, B, S)

    mk, mvT = _memkv(mem, wk_mem, wvT_mem)
    o_memT = _memattn(mqT, mk, mvT, B, S, tq_mem)

    h1 = _outproj(x2, row(ln_in_g), row(ln_in_b), o_lru.reshape(T, LRU_WIDTH), o_nsaT, o_memT, wo,
                  row(ln1_g), row(ln1_b), tm_out)
    out = _ffn(h1, wup, ffn_conv_w, row(ffn_conv_b), wd, row(ln2_g), row(ln2_b), tm_ffn, tf_ffn, S // tm_ffn)
    return out.reshape(B, S, D_MODEL)


def kernel(x, mem, ln_in_g, ln_in_b, w_in, lru_conv_w, lru_conv_b, lru_wa, lru_ba, lru_wx, lru_bx,
           lru_lam, cmp_pe_k, cmp_w1_k, cmp_w2_k, cmp_pe_v, cmp_w1_v, cmp_w2_v, w_mem_kv, w_out,
           ln1_g, ln1_b, ffn_w_up, ffn_conv_w, ffn_conv_b, ffn_w_down, ln2_g, ln2_b):
    return _layer(x, mem, ln_in_g, ln_in_b, w_in[0], lru_conv_w[0], lru_conv_b[0], lru_wa[0], lru_ba[0],
                  lru_wx[0], lru_bx[0], lru_lam[0], cmp_pe_k[0], cmp_w1_k[0], cmp_w2_k[0], cmp_pe_v[0],
                  cmp_w1_v[0], cmp_w2_v[0], w_mem_kv[0], w_out[0], ln1_g[0], ln1_b[0], ffn_w_up[0],
                  ffn_conv_w[0], ffn_conv_b[0], ffn_w_down[0], ln2_g[0], ln2_b[0],
                  tm_proj=512, tb_lru=256, tq_mem=512, tm_out=512, tm_ffn=512, tf_ffn=512)
```

```python
import functools
import math

import jax
import jax.numpy as jnp
from jax import lax
from jax.experimental import pallas as pl
from jax.experimental.pallas import tpu as pltpu

f32 = jnp.float32
bf16 = jnp.bfloat16

D_MODEL = 2048
LRU_WIDTH = 1024
LRU_BLOCKS = 8
LRU_BLOCK_DIM = 128
LRU_CONV_WIDTH = 4
LRU_C = 8.0
NSA_HEADS = 8
NSA_KV_HEADS = 2
NSA_HPG = NSA_HEADS // NSA_KV_HEADS
NSA_HEAD_DIM = 64
CMP_STRIDE = 16
CMP_BLOCK = 32
SLC_BLOCK = 64
N_SELECT = 16
WINDOW = 512
Q_BLOCK = 128
MEM_HEADS = 4
MEM_HEAD_DIM = 128
D_FF = 5632
FFN_CONV_WIDTH = 3
LN_EPS = 1e-5
NEG_INF = -1e30
FORCE_SCORE = 1e9
TAKEN_SCORE = -3e38
ALPHA = 2.0 ** 0.25

V7X_VMEM_LIMIT_BYTES = 56 * 1024 * 1024

_TQ0, _TVS0, _TVW0, _TMQ0, _TG0, _TROWS = 0, 512, 640, 768, 1280, 1312
_NX0, _NY0, _NKC0, _NVC0, _NKS0, _NKW0, _NCOLS = 0, 1024, 2048, 2176, 2304, 2432, 2560

SEL_CHUNK = 512
WIN_KEYS = WINDOW + Q_BLOCK


def _ln(x, g, b):
    mu = jnp.mean(x, axis=-1, keepdims=True)
    xc = x - mu
    var = jnp.mean(xc * xc, axis=-1, keepdims=True)
    return xc * lax.rsqrt(var + LN_EPS) * g + b


def _gelu(x):
    return jax.nn.gelu(x)


def _sigmoid(x):
    return 1.0 / (1.0 + jnp.exp(-x))


def _nt(a, b):
    return lax.dot_general(a, b, (((1,), (1,)), ((), ())), preferred_element_type=f32)


def _tn(a, b):
    return lax.dot_general(a, b, (((0,), (0,)), ((), ())), preferred_element_type=f32)


def _params(*sem):
    return pltpu.CompilerParams(dimension_semantics=sem, vmem_limit_bytes=V7X_VMEM_LIMIT_BYTES)


def _const_spec(shape):
    nd = len(shape)
    return pl.BlockSpec(shape, lambda *_: (0,) * nd, pipeline_mode=pl.Buffered(1))


def _proj_kernel(x_ref, g_ref, b_ref, wn_ref, wt_ref,
                 lx_ref, ly_ref, kc_ref, vc_ref, ks_ref, kw_ref,
                 qT_ref, vsT_ref, vwT_ref, mqT_ref, gT_ref):
    h = _ln(x_ref[...], g_ref[...], b_ref[...]).astype(bf16)

    def nat(c0, c1):
        return jnp.dot(h, wn_ref[:, c0:c1], preferred_element_type=f32)

    lx_ref[...] = nat(_NX0, _NY0)
    ly_ref[...] = nat(_NY0, _NKC0)
    kc_ref[...] = nat(_NKC0, _NVC0)
    vc_ref[...] = nat(_NVC0, _NKS0)
    ks_ref[...] = nat(_NKS0, _NKW0).astype(bf16)
    kw_ref[...] = nat(_NKW0, _NCOLS).astype(bf16)

    def tr(r0, r1):
        return _nt(wt_ref[r0:r1, :], h)

    qT_ref[...] = (tr(_TQ0, _TVS0) * (NSA_HEAD_DIM ** -0.5)).astype(bf16)
    vsT_ref[...] = tr(_TVS0, _TVW0).astype(bf16)
    vwT_ref[...] = tr(_TVW0, _TMQ0).astype(bf16)
    mqT_ref[...] = tr(_TMQ0, _TG0).astype(bf16)
    gT_ref[...] = tr(_TG0, _TROWS)


def _proj(x2, g, b, wn, wt, tm):
    T = x2.shape[0]
    nat = lambda w: pl.BlockSpec((tm, w), lambda i: (i, 0))
    trs = lambda r: pl.BlockSpec((r, tm), lambda i: (0, i))
    return pl.pallas_call(
        _proj_kernel,
        grid=(T // tm,),
        in_specs=[pl.BlockSpec((tm, D_MODEL), lambda i: (i, 0)),
                  _const_spec((1, D_MODEL)), _const_spec((1, D_MODEL)),
                  _const_spec((D_MODEL, _NCOLS)), _const_spec((_TROWS, D_MODEL))],
        out_specs=[nat(1024), nat(1024), nat(128), nat(128), nat(128), nat(128),
                   trs(512), trs(128), trs(128), trs(512), trs(32)],
        out_shape=[jax.ShapeDtypeStruct((T, 1024), f32), jax.ShapeDtypeStruct((T, 1024), f32),
                   jax.ShapeDtypeStruct((T, 128), f32), jax.ShapeDtypeStruct((T, 128), f32),
                   jax.ShapeDtypeStruct((T, 128), bf16), jax.ShapeDtypeStruct((T, 128), bf16),
                   jax.ShapeDtypeStruct((512, T), bf16), jax.ShapeDtypeStruct((128, T), bf16),
                   jax.ShapeDtypeStruct((128, T), bf16), jax.ShapeDtypeStruct((512, T), bf16),
                   jax.ShapeDtypeStruct((32, T), f32)],
        compiler_params=_params("parallel"),
        name="proj",
    )(x2, g, b, wn, wt)


def _lru_kernel(x_ref, y_ref, cw_ref, cb_ref, wax_ref, ba_ref, bx_ref, lam_ref, o_ref,
                xext_ref, a_ref, u_ref, h_ref, carry_ref, *, tb):
    s = pl.program_id(1)

    @pl.when(s == 0)
    def _():
        xext_ref[0:8, :] = jnp.zeros((8, LRU_WIDTH), f32)
        carry_ref[...] = jnp.zeros((8, LRU_WIDTH), f32)

    x = x_ref[0]
    xext_ref[8:8 + tb, :] = x
    xc = cb_ref[...]
    for k in range(LRU_CONV_WIDTH):
        off = 8 - (LRU_CONV_WIDTH - 1) + k
        xc = xc + xext_ref[off:off + tb, :] * cw_ref[k:k + 1, :]
    xext_ref[0:8, :] = x[tb - 8:tb, :]

    xb = xc.astype(bf16)
    gates = [jnp.dot(xb[:, n * 128:(n + 1) * 128], wax_ref[n], preferred_element_type=f32)
             for n in range(LRU_BLOCKS)]
    r = _sigmoid(jnp.concatenate([gt[:, 0:128] for gt in gates], axis=1) + ba_ref[...])
    i = _sigmoid(jnp.concatenate([gt[:, 128:256] for gt in gates], axis=1) + bx_ref[...])
    nl = -lam_ref[...]
    softplus = jnp.maximum(nl, 0.0) + jnp.log1p(jnp.exp(-jnp.abs(nl)))
    log_a = (-LRU_C) * r * softplus
    a = jnp.exp(log_a)
    th = jnp.tanh(log_a)
    u = jnp.sqrt(-2.0 * th / (1.0 - th)) * (i * xc)

    row = lax.broadcasted_iota(jnp.int32, (tb, LRU_WIDTH), 0) & 7
    for sh in (1, 2, 4):
        a_sh = pltpu.roll(a, sh, 0)
        u_sh = pltpu.roll(u, sh, 0)
        m = row >= sh
        u = jnp.where(m, a * u_sh + u, u)
        a = jnp.where(m, a * a_sh, a)
    a_ref[...] = a
    u_ref[...] = u

    def body(k, c):
        r0 = pl.multiple_of(k * 8, 8)
        hk = a_ref[pl.ds(r0, 8), :] * c + u_ref[pl.ds(r0, 8), :]
        h_ref[pl.ds(r0, 8), :] = hk
        return jnp.broadcast_to(hk[7:8, :], (8, LRU_WIDTH))

    carry_ref[...] = lax.fori_loop(0, tb // 8, body, carry_ref[...], unroll=4)
    o_ref[0] = (_gelu(y_ref[0]) * h_ref[...]).astype(bf16)


def _lru(lx, ly, cw, cb, wax, ba, bx, lam, tb):
    B, S, _ = lx.shape
    blk = pl.BlockSpec((1, tb, LRU_WIDTH), lambda b, s: (b, s, 0))
    return pl.pallas_call(
        functools.partial(_lru_kernel, tb=tb),
        grid=(B, S // tb),
        in_specs=[blk, blk, _const_spec((LRU_CONV_WIDTH, LRU_WIDTH)), _const_spec((1, LRU_WIDTH)),
                  _const_spec((LRU_BLOCKS, 128, 256)), _const_spec((1, LRU_WIDTH)),
                  _const_spec((1, LRU_WIDTH)), _const_spec((1, LRU_WIDTH))],
        out_specs=blk,
        out_shape=jax.ShapeDtypeStruct((B, S, LRU_WIDTH), bf16),
        scratch_shapes=[pltpu.VMEM((tb + 8, LRU_WIDTH), f32), pltpu.VMEM((tb, LRU_WIDTH), f32),
                        pltpu.VMEM((tb, LRU_WIDTH), f32), pltpu.VMEM((tb, LRU_WIDTH), f32),
                        pltpu.VMEM((8, LRU_WIDTH), f32)],
        compiler_params=_params("parallel", "arbitrary"),
        name="lru",
    )(lx, ly, cw, cb, wax, ba, bx, lam)


def _cmp_kernel(kc_ref, vc_ref, pek_ref, w1k_ref, w2k_ref, pev_ref, w1v_ref, w2vT_ref,
                kcmp_ref, vcmpT_ref):
    nsub = kc_ref.shape[1]

    def hidden(sub_ref, pe_ref, w1_ref):
        sub = sub_ref[0]
        y0 = jnp.dot((sub + pe_ref[0:1, :]).astype(bf16), w1_ref[0], preferred_element_type=f32)
        y1 = jnp.dot((sub + pe_ref[1:2, :]).astype(bf16), w1_ref[1], preferred_element_type=f32)
        return _gelu(y0 + pltpu.roll(y1, nsub - 1, 0)).astype(bf16)

    kcmp_ref[0] = jnp.dot(hidden(kc_ref, pek_ref, w1k_ref), w2k_ref[...],
                          preferred_element_type=f32).astype(bf16)
    vcmpT_ref[0] = _nt(w2vT_ref[...], hidden(vc_ref, pev_ref, w1v_ref)).astype(bf16)


def _compress(kc, vc, pek, w1k, w2k, pev, w1v, w2vT):
    B, nsub, width = kc.shape
    blk = pl.BlockSpec((1, nsub, width), lambda b: (b, 0, 0))
    return pl.pallas_call(
        _cmp_kernel,
        grid=(B,),
        in_specs=[blk, blk, _const_spec((2, width)), _const_spec((2, width, 128)), _const_spec((128, 128)),
                  _const_spec((2, width)), _const_spec((2, width, 128)), _const_spec((128, 128))],
        out_specs=[pl.BlockSpec((1, nsub, 128), lambda b: (b, 0, 0)),
                   pl.BlockSpec((1, 128, nsub), lambda b: (b, 0, 0))],
        out_shape=[jax.ShapeDtypeStruct((B, nsub, 128), bf16), jax.ShapeDtypeStruct((B, 128, nsub), bf16)],
        compiler_params=_params("parallel"),
        name="nsa_compress",
    )(kc, vc, pek, w1k, w2k, pev, w1v, w2vT)


def _nsa_kernel(qT_ref, gT_ref, kcmp_ref, vcmpT_ref, ks_ref, vsT_ref, kw_ref, vwT_ref, covT_ref, eblk_ref,
                o_ref, qaug_ref, s_ref, p_ref, acc_ref, m_ref, l_ref, pend_ref, smax_ref, *, n_slc, n_top):
    g = pl.program_id(1)
    i = pl.program_id(2)
    q0 = i * Q_BLOCK
    grow = pl.multiple_of(g * NSA_HEAD_DIM, NSA_HEAD_DIM)
    HQ = NSA_HPG * Q_BLOCK

    def per_head(m):
        return jnp.concatenate([m] * NSA_HPG, axis=1)

    qh = jnp.concatenate([qT_ref[hp * 64:(hp + 1) * 64, :] for hp in range(NSA_HPG)], axis=1)
    zq = jnp.zeros_like(qh)
    qpad = jnp.concatenate([jnp.where(g == 0, qh, zq), jnp.where(g == 1, qh, zq)], axis=0)
    t_row = q0 + lax.broadcasted_iota(jnp.int32, (1, Q_BLOCK), 1)

    ncmp = kcmp_ref.shape[1]
    n_iota = lax.broadcasted_iota(jnp.int32, (ncmp, Q_BLOCK), 0)
    bias_c = jnp.where((n_iota * CMP_STRIDE + (CMP_BLOCK - 1)) <= t_row, 0.0, NEG_INF)
    sc = jnp.dot(kcmp_ref[0], qpad, preferred_element_type=f32) + per_head(bias_c)
    pc = jnp.exp(sc - jnp.max(sc, axis=0, keepdims=True))
    lc = jnp.sum(pc, axis=0, keepdims=True)
    pc = pc * jnp.where(per_head(t_row) >= CMP_BLOCK - 1, 1.0 / lc, 0.0)
    oc = jnp.dot(vcmpT_ref[0, pl.ds(grow, 64), :], pc.astype(bf16), preferred_element_type=f32)

    psum = pc[:, 0:Q_BLOCK]
    for hp in range(1, NSA_HPG):
        psum = psum + pc[:, hp * Q_BLOCK:(hp + 1) * Q_BLOCK]
    hi = psum.astype(bf16)
    r1 = psum - hi.astype(f32)
    mid = r1.astype(bf16)
    lo = (r1 - mid.astype(f32)).astype(bf16)
    cov = covT_ref[...]
    imp = (jnp.dot(cov, hi, preferred_element_type=f32) + jnp.dot(cov, mid, preferred_element_type=f32)
           + jnp.dot(cov, lo, preferred_element_type=f32))
    j_i = lax.broadcasted_iota(jnp.int32, (n_slc, Q_BLOCK), 0)
    j_f = j_i.astype(f32)
    cur = t_row >> 6
    forced = (j_i == 0) | (j_i == cur) | (j_i == cur - 1)
    visible = (j_i * SLC_BLOCK) <= t_row
    score = jnp.where(forced, FORCE_SCORE, jnp.where(visible, imp, NEG_INF))
    sel = jnp.zeros((n_slc, Q_BLOCK), f32)
    for _ in range(n_top):
        mx = jnp.max(score, axis=0, keepdims=True)
        first = jnp.min(jnp.where(score == mx, j_f, float(n_slc)), axis=0, keepdims=True)
        hit = j_f == first
        sel = jnp.where(hit, 1.0, sel)
        score = jnp.where(hit, TAKEN_SCORE, score)

    qaug_ref[0:2 * NSA_HEAD_DIM, :] = qpad
    blk_bias = jnp.where((sel > 0.0) & visible, 0.0, NEG_INF)
    qaug_ref[2 * NSA_HEAD_DIM:2 * NSA_HEAD_DIM + n_slc, :] = per_head(blk_bias).astype(bf16)
    if n_slc < 128:
        qaug_ref[2 * NSA_HEAD_DIM + n_slc:, :] = jnp.zeros((128 - n_slc, HQ), bf16)

    def scores(c):
        k0 = pl.multiple_of(c * SEL_CHUNK, SEL_CHUNK)
        kaug = jnp.concatenate([ks_ref[0, pl.ds(k0, SEL_CHUNK), :], eblk_ref[pl.ds(k0, SEL_CHUNK), :]], axis=1)
        return jnp.dot(kaug, qaug_ref[...], preferred_element_type=f32)

    def pv_prev(c, slot):
        kprev = pl.multiple_of(jnp.maximum(c - 1, 0) * SEL_CHUNK, SEL_CHUNK)
        pv = jnp.dot(vsT_ref[pl.ds(grow, 64), pl.ds(kprev, SEL_CHUNK)], p_ref[slot],
                     preferred_element_type=f32)
        acc_ref[...] = pend_ref[...] * acc_ref[...] + pv

    def softmax(s, smax, slot):
        m = m_ref[...]
        m_new = jnp.maximum(m, smax)
        scale = jnp.exp(m - m_new)
        p = jnp.exp(s - m_new)
        p_ref[slot] = p.astype(bf16)
        l_ref[...] = scale * l_ref[...] + jnp.sum(p, axis=0, keepdims=True)
        m_ref[...] = m_new
        pend_ref[...] = scale

    def step(c, cur, nxt):
        pv_prev(c, nxt)
        s_next = scores(c + 1)
        s_ref[nxt] = s_next
        smax_cur = smax_ref[...]
        smax_ref[...] = jnp.max(s_next, axis=0, keepdims=True)
        softmax(s_ref[cur], smax_cur, cur)

    last = q0 // SEL_CHUNK
    odd = last % 2
    s0 = scores(0)
    s_ref[odd] = s0
    smax_ref[...] = jnp.max(s0, axis=0, keepdims=True)
    p_ref[1 - odd] = jnp.zeros((SEL_CHUNK, HQ), bf16)
    acc_ref[...] = jnp.zeros((NSA_HEAD_DIM, HQ), f32)
    pend_ref[...] = jnp.zeros((1, HQ), f32)
    m_ref[...] = jnp.full((1, HQ), NEG_INF, f32)
    l_ref[...] = jnp.zeros((1, HQ), f32)

    @pl.when(odd == 1)
    def _():
        step(0, 1, 0)

    def pair(pp, carry):
        c = odd + 2 * pp
        step(c, 0, 1)
        step(c + 1, 1, 0)
        return carry

    lax.fori_loop(0, last // 2, pair, 0)

    pv_prev(last, 1)
    kp_d = last * SEL_CHUNK + lax.broadcasted_iota(jnp.int32, (SEL_CHUNK, Q_BLOCK), 0)
    s_d = s_ref[0] + per_head(jnp.where(kp_d <= t_row, 0.0, NEG_INF))
    softmax(s_d, jnp.max(s_d, axis=0, keepdims=True), 0)
    pv_prev(last + 1, 0)
    osel = acc_ref[...] * (1.0 / l_ref[...])

    w0 = pl.multiple_of(jnp.maximum(q0 - WINDOW, 0), Q_BLOCK)
    kp = w0 + lax.broadcasted_iota(jnp.int32, (WIN_KEYS, Q_BLOCK), 0)
    bias_w = jnp.where((kp <= t_row) & (kp > (t_row - WINDOW)), 0.0, NEG_INF)
    sw = jnp.dot(kw_ref[0, pl.ds(w0, WIN_KEYS), :], qpad, preferred_element_type=f32) + per_head(bias_w)
    pw = jnp.exp(sw - jnp.max(sw, axis=0, keepdims=True))
    lw = jnp.sum(pw, axis=0, keepdims=True)
    ow = jnp.dot(vwT_ref[pl.ds(grow, 64), pl.ds(w0, WIN_KEYS)], pw.astype(bf16),
                 preferred_element_type=f32) * (1.0 / lw)

    gate = _sigmoid(gT_ref[...])
    for hp in range(NSA_HPG):
        cols = slice(hp * Q_BLOCK, (hp + 1) * Q_BLOCK)
        o = (gate[hp:hp + 1, :] * oc[:, cols] + gate[4 + hp:5 + hp, :] * osel[:, cols]
             + gate[8 + hp:9 + hp, :] * ow[:, cols])
        o_ref[hp * 64:(hp + 1) * 64, :] = o.astype(bf16)


def _nsa(qT, gT, kcmp, vcmpT, ks, vsT, kw, vwT, covT, eblk, B, S):
    n_q = S // Q_BLOCK
    n_slc = S // SLC_BLOCK
    ncmp = S // CMP_STRIDE
    tok = lambda r: pl.BlockSpec((r, Q_BLOCK), lambda b, g, i: (g, b * n_q + i))
    per_b_nat = pl.BlockSpec((1, S, 128), lambda b, g, i: (b, 0, 0))
    per_b_tr = pl.BlockSpec((128, S), lambda b, g, i: (0, b))
    return pl.pallas_call(
        functools.partial(_nsa_kernel, n_slc=n_slc, n_top=min(N_SELECT, n_slc)),
        grid=(B, NSA_KV_HEADS, n_q),
        in_specs=[tok(256), tok(16),
                  pl.BlockSpec((1, ncmp, 128), lambda b, g, i: (b, 0, 0)),
                  pl.BlockSpec((1, 128, ncmp), lambda b, g, i: (b, 0, 0)),
                  per_b_nat, per_b_tr, per_b_nat, per_b_tr,
                  _const_spec((n_slc, ncmp)), _const_spec((S, 128))],
        out_specs=tok(256),
        out_shape=jax.ShapeDtypeStruct((512, B * S), bf16),
        scratch_shapes=[pltpu.VMEM((2 * NSA_HEAD_DIM + 128, NSA_HPG * Q_BLOCK), bf16),
                        pltpu.VMEM((2, SEL_CHUNK, NSA_HPG * Q_BLOCK), f32),
                        pltpu.VMEM((2, SEL_CHUNK, NSA_HPG * Q_BLOCK), bf16),
                        pltpu.VMEM((NSA_HEAD_DIM, NSA_HPG * Q_BLOCK), f32)]
        + [pltpu.VMEM((1, NSA_HPG * Q_BLOCK), f32)] * 4,
        compiler_params=_params("parallel", "parallel", "arbitrary"),
        name="nsa_attn",
    )(qT, gT, kcmp, vcmpT, ks, vsT, kw, vwT, covT, eblk)


def _memkv_kernel(mem_ref, wk_ref, wvT_ref, k_ref, vT_ref):
    m = mem_ref[0].astype(bf16)
    k_ref[0] = jnp.dot(m, wk_ref[...], preferred_element_type=f32).astype(bf16)
    vT_ref[0] = _nt(wvT_ref[...], m).astype(bf16)


def _memkv(mem, wk, wvT):
    B, M, _ = mem.shape
    return pl.pallas_call(
        _memkv_kernel,
        grid=(B,),
        in_specs=[pl.BlockSpec((1, M, D_MODEL), lambda b: (b, 0, 0)),
                  _const_spec((D_MODEL, 512)), _const_spec((512, D_MODEL))],
        out_specs=[pl.BlockSpec((1, M, 512), lambda b: (b, 0, 0)),
                   pl.BlockSpec((1, 512, M), lambda b: (b, 0, 0))],
        out_shape=[jax.ShapeDtypeStruct((B, M, 512), bf16), jax.ShapeDtypeStruct((B, 512, M), bf16)],
        compiler_params=_params("parallel"),
        name="mem_kv",
    )(mem, wk, wvT)


def _memattn_kernel(qT_ref, k_ref, vT_ref, o_ref):
    for h in range(MEM_HEADS):
        rows = slice(h * MEM_HEAD_DIM, (h + 1) * MEM_HEAD_DIM)
        s = jnp.dot(k_ref[0, :, rows], qT_ref[rows, :], preferred_element_type=f32) * (MEM_HEAD_DIM ** -0.5)
        p = jnp.exp(s - jnp.max(s, axis=0, keepdims=True))
        l = jnp.sum(p, axis=0, keepdims=True)
        o = jnp.dot(vT_ref[0, rows, :], p.astype(bf16), preferred_element_type=f32) * (1.0 / l)
        o_ref[rows, :] = o.astype(bf16)


def _memattn(mqT, mk, mvT, B, S, tq):
    M = mk.shape[1]
    nq = S // tq
    return pl.pallas_call(
        _memattn_kernel,
        grid=(B, nq),
        in_specs=[pl.BlockSpec((512, tq), lambda b, i: (0, b * nq + i)),
                  pl.BlockSpec((1, M, 512), lambda b, i: (b, 0, 0)),
                  pl.BlockSpec((1, 512, M), lambda b, i: (b, 0, 0))],
        out_specs=pl.BlockSpec((512, tq), lambda b, i: (0, b * nq + i)),
        out_shape=jax.ShapeDtypeStruct((512, B * S), bf16),
        compiler_params=_params("parallel", "parallel"),
        name="mem_attn",
    )(mqT, mk, mvT)


def _outproj_kernel(x_ref, lng_ref, lnb_ref, ol_ref, onT_ref, omT_ref, wo_ref, g1_ref, b1_ref, h1_ref):
    h0 = _ln(x_ref[...], lng_ref[...], lnb_ref[...])
    mixed = jnp.dot(ol_ref[...], wo_ref[0:1024, :], preferred_element_type=f32)
    mixed = mixed + _tn(onT_ref[...], wo_ref[1024:1536, :])
    mixed = mixed + _tn(omT_ref[...], wo_ref[1536:2048, :])
    h1_ref[...] = _ln(ALPHA * h0 + mixed, g1_ref[...], b1_ref[...])


def _outproj(x2, lng, lnb, ol, onT, omT, wo, g1, b1, tm):
    T = x2.shape[0]
    row = lambda w: pl.BlockSpec((tm, w), lambda i: (i, 0))
    trs = pl.BlockSpec((512, tm), lambda i: (0, i))
    vec = _const_spec((1, D_MODEL))
    return pl.pallas_call(
        _outproj_kernel,
        grid=(T // tm,),
        in_specs=[row(D_MODEL), vec, vec, row(1024), trs, trs, _const_spec((D_MODEL, D_MODEL)), vec, vec],
        out_specs=row(D_MODEL),
        out_shape=jax.ShapeDtypeStruct((T, D_MODEL), f32),
        compiler_params=_params("parallel"),
        name="out_proj_ln1",
    )(x2, lng, lnb, ol, onT, omT, wo, g1, b1)


def _ffn_kernel(h_ref, halo_ref, wg_ref, wu_ref, cwg_ref, cwu_ref, cbg_ref, cbu_ref, wd_ref,
                g2_ref, b2_ref, o_ref, lhs_ref, acc_ref, *, tm, blocks_per_seq):
    i = pl.program_id(0)
    j = pl.program_id(1)

    @pl.when(j == 0)
    def _():
        first = (i % blocks_per_seq) == 0
        lhs_ref[0:8, :] = jnp.where(first, 0.0, halo_ref[...]).astype(bf16)
        lhs_ref[8:8 + tm, :] = h_ref[...].astype(bf16)
        acc_ref[...] = jnp.zeros_like(acc_ref)

    lhs = lhs_ref[...]

    def conv(w_ref, cw_ref, cb_ref):
        up = jnp.dot(lhs, w_ref[...], preferred_element_type=f32)
        y = cb_ref[...] + pltpu.roll(up, 2, 0)[8:8 + tm, :] * cw_ref[0:1, :]
        y = y + pltpu.roll(up, 1, 0)[8:8 + tm, :] * cw_ref[1:2, :]
        return y + up[8:8 + tm, :] * cw_ref[2:3, :]

    act = (_gelu(conv(wg_ref, cwg_ref, cbg_ref)) * conv(wu_ref, cwu_ref, cbu_ref)).astype(bf16)
    acc_ref[...] += jnp.dot(act, wd_ref[...], preferred_element_type=f32)

    @pl.when(j == pl.num_programs(1) - 1)
    def _():
        o_ref[...] = _ln(ALPHA * h_ref[...] + acc_ref[...], g2_ref[...], b2_ref[...])


def _ffn(h1, wup, cw, cb, wd, g2, b2, tm, tf, blocks_per_seq):
    T = h1.shape[0]
    nf = D_FF // tf
    vec = _const_spec((1, D_MODEL))
    return pl.pallas_call(
        functools.partial(_ffn_kernel, tm=tm, blocks_per_seq=blocks_per_seq),
        grid=(T // tm, nf),
        in_specs=[pl.BlockSpec((tm, D_MODEL), lambda i, j: (i, 0)),
                  pl.BlockSpec((8, D_MODEL), lambda i, j: (jnp.maximum(i * (tm // 8) - 1, 0), 0)),
                  pl.BlockSpec((D_MODEL, tf), lambda i, j: (0, j)),
                  pl.BlockSpec((D_MODEL, tf), lambda i, j: (0, nf + j)),
                  pl.BlockSpec((FFN_CONV_WIDTH, tf), lambda i, j: (0, j)),
                  pl.BlockSpec((FFN_CONV_WIDTH, tf), lambda i, j: (0, nf + j)),
                  pl.BlockSpec((1, tf), lambda i, j: (0, j)),
                  pl.BlockSpec((1, tf), lambda i, j: (0, nf + j)),
                  pl.BlockSpec((tf, D_MODEL), lambda i, j: (j, 0)),
                  vec, vec],
        out_specs=pl.BlockSpec((tm, D_MODEL), lambda i, j: (i, 0)),
        out_shape=jax.ShapeDtypeStruct((T, D_MODEL), f32),
        scratch_shapes=[pltpu.VMEM((tm + 8, D_MODEL), bf16), pltpu.VMEM((tm, D_MODEL), f32)],
        compiler_params=_params("parallel", "arbitrary"),
        name="conv_ffn_ln2",
    )(h1, h1, wup, wup, cw, cw, cb, cb, wd, g2, b2)


def _layer(x, mem, ln_in_g, ln_in_b, w_in, lru_conv_w, lru_conv_b, lru_wa, lru_ba, lru_wx, lru_bx,
           lru_lam, cmp_pe_k, cmp_w1_k, cmp_w2_k, cmp_pe_v, cmp_w1_v, cmp_w2_v, w_mem_kv, w_out,
           ln1_g, ln1_b, ffn_w_up, ffn_conv_w, ffn_conv_b, ffn_w_down, ln2_g, ln2_b,
           *, tm_proj, tb_lru, tq_mem, tm_out, tm_ffn, tf_ffn):
    B, S, _ = x.shape
    T = B * S
    G, Dh = NSA_KV_HEADS, NSA_HEAD_DIM
    row = lambda v: v.reshape(1, -1)

    c_q = 2 * LRU_WIDTH
    c_kv = c_q + NSA_HEADS * Dh
    c_gate = c_kv + 6 * G * Dh
    c_mq = c_gate + 3 * NSA_HEADS
    kv_cols = [w_in[:, c_kv + n * G * Dh: c_kv + (n + 1) * G * Dh] for n in range(6)]
    wn = jnp.concatenate([w_in[:, :c_q], kv_cols[0], kv_cols[1], kv_cols[2], kv_cols[4]], axis=1).astype(bf16)
    w_gate = w_in[:, c_gate:c_mq].reshape(D_MODEL, G, NSA_HPG, 3)
    w_gate = jnp.pad(w_gate.transpose(1, 3, 2, 0), ((0, 0), (0, 1), (0, 0), (0, 0)))
    wt = jnp.concatenate([w_in[:, c_q:c_kv].T, kv_cols[3].T, kv_cols[5].T, w_in[:, c_mq:].T,
                          w_gate.reshape(2 * 16, D_MODEL)], axis=0).astype(bf16)

    wax = jnp.concatenate([lru_wa, lru_wx], axis=-1).astype(bf16)

    def cmp_weights(pe, w1, w2):
        pe2 = jnp.broadcast_to(pe.reshape(2, CMP_STRIDE, 1, Dh), (2, CMP_STRIDE, G, Dh)).reshape(2, -1)
        eye = jnp.eye(G, dtype=f32)
        w1e = jnp.einsum('hlde,gk->hlgdke', w1.reshape(2, CMP_STRIDE, Dh, Dh), eye)
        w1e = w1e.reshape(2, CMP_STRIDE * G * Dh, G * Dh).astype(bf16)
        w2e = jnp.einsum('ef,gk->gekf', w2, eye).reshape(G * Dh, G * Dh)
        return pe2, w1e, w2e

    pek, w1k, w2k = cmp_weights(cmp_pe_k, cmp_w1_k, cmp_w2_k)
    pev, w1v, w2v = cmp_weights(cmp_pe_v, cmp_w1_v, cmp_w2_v)

    n_slc, ncmp = S // SLC_BLOCK, S // CMP_STRIDE
    ci = jnp.arange(ncmp)[None, :] * CMP_STRIDE
    sj = jnp.arange(n_slc)[:, None] * SLC_BLOCK
    covT = ((ci <= sj + SLC_BLOCK - 1) & (ci + CMP_BLOCK - 1 >= sj)).astype(bf16)
    eblk = (jnp.arange(S)[:, None] // SLC_BLOCK == jnp.arange(128)[None, :]).astype(bf16)

    wk_mem = w_mem_kv[:, :512].astype(bf16)
    wvT_mem = w_mem_kv[:, 512:].T.astype(bf16)
    wo = w_out.astype(bf16)
    wup = ffn_w_up.astype(bf16)
    wd = ffn_w_down.astype(bf16)

    x2 = x.reshape(T, D_MODEL)
    lx, ly, kc, vc, ks, kw, qT, vsT, vwT, mqT, gT = _proj(x2, row(ln_in_g), row(ln_in_b), wn, wt, tm_proj)

    o_lru = _lru(lx.reshape(B, S, LRU_WIDTH), ly.reshape(B, S, LRU_WIDTH), lru_conv_w, row(lru_conv_b),
                 wax, row(lru_ba), row(lru_bx), row(lru_lam), tb_lru)

    sub = lambda a: a.reshape(B, ncmp, CMP_STRIDE * G * Dh)
    kcmp, vcmpT = _compress(sub(kc), sub(vc), pek, w1k, w2k.astype(bf16), pev, w1v, w2v.T.astype(bf16))
    o_nsaT = _nsa(qT, gT, kcmp, vcmpT, ks.reshape(B, S, 128), vsT, kw.reshape(B, S, 128), vwT, covT, eblk, B, S)

    mk, mvT = _memkv(mem, wk_mem, wvT_mem)
    o_memT = _memattn(mqT, mk, mvT, B, S, tq_mem)

    h1 = _outproj(x2, row(ln_in_g), row(ln_in_b), o_lru.reshape(T, LRU_WIDTH), o_nsaT, o_memT, wo,
                  row(ln1_g), row(ln1_b), tm_out)
    out = _ffn(h1, wup, ffn_conv_w, row(ffn_conv_b), wd, row(ln2_g), row(ln2_b), tm_ffn, tf_ffn, S // tm_ffn)
    return out.reshape(B, S, D_MODEL)


def kernel(x, mem, ln_in_g, ln_in_b, w_in, lru_conv_w, lru_conv_b, lru_wa, lru_ba, lru_wx, lru_bx,
           lru_lam, cmp_pe_k, cmp_w1_k, cmp_w2_k, cmp_pe_v, cmp_w1_v, cmp_w2_v, w_mem_kv, w_out,
           ln1_g, ln1_b, ffn_w_up, ffn_conv_w, ffn_conv_b, ffn_w_down, ln2_g, ln2_b):
    return _layer(x, mem, ln_in_g, ln_in_b, w_in[0], lru_conv_w[0], lru_conv_b[0], lru_wa[0], lru_ba[0],
                  lru_wx[0], lru_bx[0], lru_lam[0], cmp_pe_k[0], cmp_w1_k[0], cmp_w2_k[0], cmp_pe_v[0],
                  cmp_w1_v[0], cmp_w2_v[0], w_mem_kv[0], w_out[0], ln1_g[0], ln1_b[0], ffn_w_up[0],
                  ffn_conv_w[0], ffn_conv_b[0], ffn_w_down[0], ln2_g[0], ln2_b[0],
                  tm_proj=512, tb_lru=256, tq_mem=512, tm_out=512, tm_ffn=512, tf_ffn=512)
```

```python
import functools
import math

import jax
import jax.numpy as jnp
from jax import lax
from jax.experimental import pallas as pl
from jax.experimental.pallas import tpu as pltpu

f32 = jnp.float32
bf16 = jnp.bfloat16

D_MODEL = 2048
LRU_WIDTH = 1024
LRU_BLOCKS = 8
LRU_BLOCK_DIM = 128
LRU_CONV_WIDTH = 4
LRU_C = 8.0
LRU_TINY = 1e-30
NSA_HEADS = 8
NSA_KV_HEADS = 2
NSA_HPG = NSA_HEADS // NSA_KV_HEADS
NSA_HEAD_DIM = 64
CMP_STRIDE = 16
CMP_BLOCK = 32
SLC_BLOCK = 64
N_SELECT = 16
WINDOW = 512
Q_BLOCK = 128
MEM_HEADS = 4
MEM_HEAD_DIM = 128
D_FF = 5632
FFN_CONV_WIDTH = 3
LN_EPS = 1e-5
NEG_INF = -1e30
FORCE_SCORE = 1e9
TAKEN_SCORE = -3e38
ALPHA = 2.0 ** 0.25

V7X_VMEM_LIMIT_BYTES = 56 * 1024 * 1024

_TQ0, _TVS0, _TVW0, _TMQ0, _TG0, _TROWS = 0, 512, 640, 768, 1280, 1312
_NX0, _NY0, _NKC0, _NVC0, _NKS0, _NKW0, _NCOLS = 0, 1024, 2048, 2176, 2304, 2432, 2560

SEL_CHUNK = 512
WIN_KEYS = WINDOW + Q_BLOCK
ONES_ROWS = 16


def _ln(x, g, b):
    mu = jnp.mean(x, axis=-1, keepdims=True)
    xc = x - mu
    var = jnp.mean(xc * xc, axis=-1, keepdims=True)
    return xc * lax.rsqrt(var + LN_EPS) * g + b


def _gelu(x):
    return jax.nn.gelu(x)


def _sigmoid(x):
    return 1.0 / (1.0 + jnp.exp(-x))


def _sigmoid_tanh(x):
    return 0.5 * jnp.tanh(0.5 * x) + 0.5


def _nt(a, b):
    return lax.dot_general(a, b, (((1,), (1,)), ((), ())), preferred_element_type=f32)


def _tn(a, b):
    return lax.dot_general(a, b, (((0,), (0,)), ((), ())), preferred_element_type=f32)


def _params(*sem, flags=None):
    return pltpu.CompilerParams(dimension_semantics=sem, vmem_limit_bytes=V7X_VMEM_LIMIT_BYTES, flags=flags)


def _const_spec(shape):
    nd = len(shape)
    return pl.BlockSpec(shape, lambda *_: (0,) * nd, pipeline_mode=pl.Buffered(1))


def _proj_kernel(x_ref, g_ref, b_ref, wn_ref, wt_ref,
                 lx_ref, ly_ref, kc_ref, vc_ref, ks_ref, kw_ref,
                 qT_ref, vsT_ref, vwT_ref, mqT_ref, gT_ref):
    h = _ln(x_ref[...], g_ref[...], b_ref[...]).astype(bf16)

    def nat(c0, c1):
        return jnp.dot(h, wn_ref[:, c0:c1], preferred_element_type=f32)

    lx_ref[...] = nat(_NX0, _NY0)
    ly_ref[...] = nat(_NY0, _NKC0)
    kc_ref[...] = nat(_NKC0, _NVC0)
    vc_ref[...] = nat(_NVC0, _NKS0)
    ks_ref[...] = nat(_NKS0, _NKW0).astype(bf16)
    kw_ref[...] = nat(_NKW0, _NCOLS).astype(bf16)

    def tr(r0, r1):
        return _nt(wt_ref[r0:r1, :], h)

    qT_ref[...] = (tr(_TQ0, _TVS0) * (NSA_HEAD_DIM ** -0.5)).astype(bf16)
    vsT_ref[...] = tr(_TVS0, _TVW0).astype(bf16)
    vwT_ref[...] = tr(_TVW0, _TMQ0).astype(bf16)
    mqT_ref[...] = tr(_TMQ0, _TG0).astype(bf16)
    gT_ref[...] = tr(_TG0, _TROWS)


def _proj(x2, g, b, wn, wt, tm):
    T = x2.shape[0]
    nat = lambda w: pl.BlockSpec((tm, w), lambda i: (i, 0))
    trs = lambda r: pl.BlockSpec((r, tm), lambda i: (0, i))
    return pl.pallas_call(
        _proj_kernel,
        grid=(T // tm,),
        in_specs=[pl.BlockSpec((tm, D_MODEL), lambda i: (i, 0)),
                  _const_spec((1, D_MODEL)), _const_spec((1, D_MODEL)),
                  _const_spec((D_MODEL, _NCOLS)), _const_spec((_TROWS, D_MODEL))],
        out_specs=[nat(1024), nat(1024), nat(128), nat(128), nat(128), nat(128),
                   trs(512), trs(128), trs(128), trs(512), trs(32)],
        out_shape=[jax.ShapeDtypeStruct((T, 1024), f32), jax.ShapeDtypeStruct((T, 1024), f32),
                   jax.ShapeDtypeStruct((T, 128), f32), jax.ShapeDtypeStruct((T, 128), f32),
                   jax.ShapeDtypeStruct((T, 128), bf16), jax.ShapeDtypeStruct((T, 128), bf16),
                   jax.ShapeDtypeStruct((512, T), bf16), jax.ShapeDtypeStruct((128, T), bf16),
                   jax.ShapeDtypeStruct((128, T), bf16), jax.ShapeDtypeStruct((512, T), bf16),
                   jax.ShapeDtypeStruct((32, T), f32)],
        compiler_params=_params("parallel"),
        name="proj",
    )(x2, g, b, wn, wt)


def _lru_pitch(tb):
    seg = tb // 8
    return seg + 8 if (seg // 8) % 2 == 0 else seg


def _lru_kernel(x_ref, y_ref, cw_ref, cb_ref, wax_ref, ba_ref, bx_ref, lam_ref, o_ref,
                xext_ref, a_ref, u_ref, carry_ref, *, tb):
    s = pl.program_id(1)

    @pl.when(s == 0)
    def _():
        xext_ref[0:8, :] = jnp.zeros((8, LRU_WIDTH), f32)
        carry_ref[...] = jnp.zeros((1, LRU_WIDTH), f32)

    x = x_ref[0]
    xext_ref[8:8 + tb, :] = x
    xc = cb_ref[...]
    for k in range(LRU_CONV_WIDTH):
        off = 8 - (LRU_CONV_WIDTH - 1) + k
        xc = xc + xext_ref[off:off + tb, :] * cw_ref[k:k + 1, :]
    xext_ref[0:8, :] = x[tb - 8:tb, :]

    xb = xc.astype(bf16)
    gates = [jnp.dot(xb[:, n * 128:(n + 1) * 128], wax_ref[n], preferred_element_type=f32)
             for n in range(LRU_BLOCKS)]
    r = _sigmoid_tanh(jnp.concatenate([gt[:, 0:128] for gt in gates], axis=1) + ba_ref[...])
    i = _sigmoid_tanh(jnp.concatenate([gt[:, 128:256] for gt in gates], axis=1) + bx_ref[...])
    nl = -lam_ref[...]
    softplus = jnp.maximum(nl, 0.0) + jnp.log1p(jnp.exp(-jnp.abs(nl)))
    log_a = (-LRU_C) * r * softplus
    a = jnp.exp(log_a)
    th = jnp.tanh(log_a)
    z = -2.0 * th / (1.0 - th)
    u = (z * lax.rsqrt(jnp.maximum(z, LRU_TINY))) * (i * xc)
    seg = tb // 8
    pitch = _lru_pitch(tb)
    for n in range(LRU_BLOCKS):
        for r8 in range(8):
            a_ref[n, r8 * pitch:r8 * pitch + seg, :] = a[r8 * seg:(r8 + 1) * seg, n * 128:(n + 1) * 128]
            u_ref[n, r8 * pitch:r8 * pitch + seg, :] = u[r8 * seg:(r8 + 1) * seg, n * 128:(n + 1) * 128]

    def sweep(k, carry):
        rows = pl.ds(k, 8, stride=pitch)
        out = []
        for n in range(LRU_BLOCKS):
            h, p = carry[n]
            ak = a_ref[n, rows, :]
            h = ak * h + u_ref[n, rows, :]
            p = ak * p
            u_ref[n, rows, :] = h
            a_ref[n, rows, :] = p
            out.append((h, p))
        return tuple(out)

    init = tuple((jnp.zeros((8, 128), f32), jnp.ones((8, 128), f32)) for _ in range(LRU_BLOCKS))
    ends = lax.fori_loop(0, seg, sweep, init, unroll=4)
    h_end = jnp.concatenate([e[0] for e in ends], axis=1)
    p_end = jnp.concatenate([e[1] for e in ends], axis=1)

    c = carry_ref[...]
    for r8 in range(8):
        rows = slice(r8 * seg, (r8 + 1) * seg)
        prow = slice(r8 * pitch, r8 * pitch + seg)
        h_loc = jnp.concatenate([u_ref[n, prow, :] for n in range(LRU_BLOCKS)], axis=1)
        p_cum = jnp.concatenate([a_ref[n, prow, :] for n in range(LRU_BLOCKS)], axis=1)
        o_ref[0, rows, :] = (_gelu(y_ref[0, rows, :]) * (h_loc + p_cum * c)).astype(bf16)
        c = h_end[r8:r8 + 1, :] + p_end[r8:r8 + 1, :] * c
    carry_ref[...] = c


def _lru(lx, ly, cw, cb, wax, ba, bx, lam, tb):
    B, S, _ = lx.shape
    blk = pl.BlockSpec((1, tb, LRU_WIDTH), lambda b, s: (b, s, 0))
    return pl.pallas_call(
        functools.partial(_lru_kernel, tb=tb),
        grid=(B, S // tb),
        in_specs=[blk, blk, _const_spec((LRU_CONV_WIDTH, LRU_WIDTH)), _const_spec((1, LRU_WIDTH)),
                  _const_spec((LRU_BLOCKS, 128, 256)), _const_spec((1, LRU_WIDTH)),
                  _const_spec((1, LRU_WIDTH)), _const_spec((1, LRU_WIDTH))],
        out_specs=blk,
        out_shape=jax.ShapeDtypeStruct((B, S, LRU_WIDTH), bf16),
        scratch_shapes=[pltpu.VMEM((tb + 8, LRU_WIDTH), f32),
                        pltpu.VMEM((LRU_BLOCKS, 8 * _lru_pitch(tb), 128), f32),
                        pltpu.VMEM((LRU_BLOCKS, 8 * _lru_pitch(tb), 128), f32),
                        pltpu.VMEM((1, LRU_WIDTH), f32)],
        compiler_params=_params("parallel", "arbitrary"),
        name="lru",
    )(lx, ly, cw, cb, wax, ba, bx, lam)


def _cmp_kernel(kc_ref, vc_ref, pek_ref, w1k_ref, w2k_ref, pev_ref, w1v_ref, w2vT_ref,
                kcmp_ref, vcmpT_ref):
    nsub = kc_ref.shape[1]

    def hidden(sub_ref, pe_ref, w1_ref):
        sub = sub_ref[0]
        y0 = jnp.dot((sub + pe_ref[0:1, :]).astype(bf16), w1_ref[0], preferred_element_type=f32)
        y1 = jnp.dot((sub + pe_ref[1:2, :]).astype(bf16), w1_ref[1], preferred_element_type=f32)
        return _gelu(y0 + pltpu.roll(y1, nsub - 1, 0)).astype(bf16)

    kcmp_ref[0] = jnp.dot(hidden(kc_ref, pek_ref, w1k_ref), w2k_ref[...],
                          preferred_element_type=f32).astype(bf16)
    vcmpT_ref[0] = _nt(w2vT_ref[...], hidden(vc_ref, pev_ref, w1v_ref)).astype(bf16)


def _compress(kc, vc, pek, w1k, w2k, pev, w1v, w2vT):
    B, nsub, width = kc.shape
    blk = pl.BlockSpec((1, nsub, width), lambda b: (b, 0, 0))
    return pl.pallas_call(
        _cmp_kernel,
        grid=(B,),
        in_specs=[blk, blk, _const_spec((2, width)), _const_spec((2, width, 128)), _const_spec((128, 128)),
                  _const_spec((2, width)), _const_spec((2, width, 128)), _const_spec((128, 128))],
        out_specs=[pl.BlockSpec((1, nsub, 128), lambda b: (b, 0, 0)),
                   pl.BlockSpec((1, 128, nsub), lambda b: (b, 0, 0))],
        out_shape=[jax.ShapeDtypeStruct((B, nsub, 128), bf16), jax.ShapeDtypeStruct((B, 128, nsub), bf16)],
        compiler_params=_params("parallel"),
        name="nsa_compress",
    )(kc, vc, pek, w1k, w2k, pev, w1v, w2vT)


def _nsa_kernel(qT_ref, gT_ref, kcmp_ref, vcmpT_ref, ks_ref, vsT_ref, kw_ref, vwT_ref, covT_ref, eblk_ref,
                o_ref, qaug_ref, s_ref, acc_ref, m_ref, smax_ref, *, n_slc, n_top):
    g = pl.program_id(1)
    i = pl.program_id(2)
    q0 = i * Q_BLOCK
    grow = pl.multiple_of(g * NSA_HEAD_DIM, NSA_HEAD_DIM)
    HQ = NSA_HPG * Q_BLOCK

    def per_head(m):
        return jnp.concatenate([m] * NSA_HPG, axis=1)

    qh = jnp.concatenate([qT_ref[hp * 64:(hp + 1) * 64, :] for hp in range(NSA_HPG)], axis=1)
    zq = jnp.zeros_like(qh)
    qpad = jnp.concatenate([jnp.where(g == 0, qh, zq), jnp.where(g == 1, qh, zq)], axis=0)
    t_row = q0 + lax.broadcasted_iota(jnp.int32, (1, Q_BLOCK), 1)

    ncmp = kcmp_ref.shape[1]
    n_iota = lax.broadcasted_iota(jnp.int32, (ncmp, Q_BLOCK), 0)
    bias_c = jnp.where((n_iota * CMP_STRIDE + (CMP_BLOCK - 1)) <= t_row, 0.0, NEG_INF)
    sc = jnp.dot(kcmp_ref[0], qpad, preferred_element_type=f32) + per_head(bias_c)
    pc = jnp.exp(sc - jnp.max(sc, axis=0, keepdims=True))
    lc = jnp.sum(pc, axis=0, keepdims=True)
    pc = pc * jnp.where(per_head(t_row) >= CMP_BLOCK - 1, 1.0 / lc, 0.0)
    oc = jnp.dot(vcmpT_ref[0, pl.ds(grow, 64), :], pc.astype(bf16), preferred_element_type=f32)

    psum = pc[:, 0:Q_BLOCK]
    for hp in range(1, NSA_HPG):
        psum = psum + pc[:, hp * Q_BLOCK:(hp + 1) * Q_BLOCK]
    hi = psum.astype(bf16)
    r1 = psum - hi.astype(f32)
    mid = r1.astype(bf16)
    lo = (r1 - mid.astype(f32)).astype(bf16)
    cov = covT_ref[...]
    imp = (jnp.dot(cov, hi, preferred_element_type=f32) + jnp.dot(cov, mid, preferred_element_type=f32)
           + jnp.dot(cov, lo, preferred_element_type=f32))
    j_i = lax.broadcasted_iota(jnp.int32, (n_slc, Q_BLOCK), 0)
    j_f = j_i.astype(f32)
    cur = t_row >> 6
    forced = (j_i == 0) | (j_i == cur) | (j_i == cur - 1)
    visible = (j_i * SLC_BLOCK) <= t_row
    score = jnp.where(forced, FORCE_SCORE, jnp.where(visible, imp, NEG_INF))

    w0 = pl.multiple_of(jnp.maximum(q0 - WINDOW, 0), Q_BLOCK)
    kp = w0 + lax.broadcasted_iota(jnp.int32, (WIN_KEYS, Q_BLOCK), 0)
    bias_w = jnp.where((kp <= t_row) & (kp > (t_row - WINDOW)), 0.0, NEG_INF)
    sw = jnp.dot(kw_ref[0, pl.ds(w0, WIN_KEYS), :], qpad, preferred_element_type=f32) + per_head(bias_w)
    pw = jnp.exp(sw - jnp.max(sw, axis=0, keepdims=True)).astype(bf16)
    vw1 = jnp.concatenate([vwT_ref[pl.ds(grow, 64), pl.ds(w0, WIN_KEYS)],
                           jnp.ones((ONES_ROWS, WIN_KEYS), bf16)], axis=0)
    ow = jnp.dot(vw1, pw, preferred_element_type=f32)
    ow = ow[0:NSA_HEAD_DIM, :] * (1.0 / ow[NSA_HEAD_DIM:NSA_HEAD_DIM + 1, :])

    sel = jnp.zeros((n_slc, Q_BLOCK), f32)
    for _ in range(n_top):
        mx = jnp.max(score, axis=0, keepdims=True)
        first = jnp.min(jnp.where(score == mx, j_f, float(n_slc)), axis=0, keepdims=True)
        hit = j_f == first
        sel = jnp.where(hit, 1.0, sel)
        score = jnp.where(hit, TAKEN_SCORE, score)

    qaug_ref[0:2 * NSA_HEAD_DIM, :] = qpad
    blk_bias = jnp.where((sel > 0.0) & visible, 0.0, NEG_INF)
    qaug_ref[2 * NSA_HEAD_DIM:2 * NSA_HEAD_DIM + n_slc, :] = per_head(blk_bias).astype(bf16)
    if n_slc < 128:
        qaug_ref[2 * NSA_HEAD_DIM + n_slc:, :] = jnp.zeros((128 - n_slc, HQ), bf16)

    def scores(c):
        k0 = pl.multiple_of(c * SEL_CHUNK, SEL_CHUNK)
        kaug = jnp.concatenate([ks_ref[0, pl.ds(k0, SEL_CHUNK), :], eblk_ref[pl.ds(k0, SEL_CHUNK), :]], axis=1)
        return jnp.dot(kaug, qaug_ref[...], preferred_element_type=f32)

    def attend(c, s, smax):
        k0 = pl.multiple_of(c * SEL_CHUNK, SEL_CHUNK)
        m = m_ref[...]
        m_new = jnp.maximum(m, smax)
        m_ref[...] = m_new
        p = jnp.exp(s - m_new).astype(bf16)
        v1 = jnp.concatenate([vsT_ref[pl.ds(grow, 64), pl.ds(k0, SEL_CHUNK)],
                              jnp.ones((ONES_ROWS, SEL_CHUNK), bf16)], axis=0)
        acc_ref[...] = jnp.exp(m - m_new) * acc_ref[...] + jnp.dot(v1, p, preferred_element_type=f32)

    def step(c, cur, nxt):
        smax = smax_ref[...]
        s_next = scores(c + 1)
        s_ref[nxt] = s_next
        smax_ref[...] = jnp.max(s_next, axis=0, keepdims=True)
        attend(c, s_ref[cur], smax)

    last = q0 // SEL_CHUNK
    odd = last % 2
    s0 = scores(0)
    s_ref[odd] = s0
    smax_ref[...] = jnp.max(s0, axis=0, keepdims=True)
    acc_ref[...] = jnp.zeros((NSA_HEAD_DIM + ONES_ROWS, HQ), f32)
    m_ref[...] = jnp.full((1, HQ), NEG_INF, f32)

    @pl.when(odd == 1)
    def _():
        step(0, 1, 0)

    def pair(pp, carry):
        c = odd + 2 * pp
        step(c, 0, 1)
        step(c + 1, 1, 0)
        return carry

    lax.fori_loop(0, last // 2, pair, 0)

    kp_d = last * SEL_CHUNK + lax.broadcasted_iota(jnp.int32, (SEL_CHUNK, Q_BLOCK), 0)
    s_d = s_ref[0] + per_head(jnp.where(kp_d <= t_row, 0.0, NEG_INF))
    attend(last, s_d, jnp.max(s_d, axis=0, keepdims=True))
    osel = acc_ref[0:NSA_HEAD_DIM, :] * (1.0 / acc_ref[NSA_HEAD_DIM:NSA_HEAD_DIM + 1, :])

    gate = _sigmoid(gT_ref[...])
    for hp in range(NSA_HPG):
        cols = slice(hp * Q_BLOCK, (hp + 1) * Q_BLOCK)
        o = (gate[hp:hp + 1, :] * oc[:, cols] + gate[4 + hp:5 + hp, :] * osel[:, cols]
             + gate[8 + hp:9 + hp, :] * ow[:, cols])
        o_ref[hp * 64:(hp + 1) * 64, :] = o.astype(bf16)


def _nsa(qT, gT, kcmp, vcmpT, ks, vsT, kw, vwT, covT, eblk, B, S):
    n_q = S // Q_BLOCK
    n_slc = S // SLC_BLOCK
    ncmp = S // CMP_STRIDE
    tok = lambda r: pl.BlockSpec((r, Q_BLOCK), lambda b, g, i: (g, b * n_q + i))
    per_b_nat = pl.BlockSpec((1, S, 128), lambda b, g, i: (b, 0, 0))
    per_b_tr = pl.BlockSpec((128, S), lambda b, g, i: (0, b))
    return pl.pallas_call(
        functools.partial(_nsa_kernel, n_slc=n_slc, n_top=min(N_SELECT, n_slc)),
        grid=(B, NSA_KV_HEADS, n_q),
        in_specs=[tok(256), tok(16),
                  pl.BlockSpec((1, ncmp, 128), lambda b, g, i: (b, 0, 0)),
                  pl.BlockSpec((1, 128, ncmp), lambda b, g, i: (b, 0, 0)),
                  per_b_nat, per_b_tr, per_b_nat, per_b_tr,
                  _const_spec((n_slc, ncmp)), _const_spec((S, 128))],
        out_specs=tok(256),
        out_shape=jax.ShapeDtypeStruct((512, B * S), bf16),
        scratch_shapes=[pltpu.VMEM((2 * NSA_HEAD_DIM + 128, NSA_HPG * Q_BLOCK), bf16),
                        pltpu.VMEM((2, SEL_CHUNK, NSA_HPG * Q_BLOCK), f32),
                        pltpu.VMEM((NSA_HEAD_DIM + ONES_ROWS, NSA_HPG * Q_BLOCK), f32)]
        + [pltpu.VMEM((1, NSA_HPG * Q_BLOCK), f32)] * 2,
        compiler_params=_params("parallel", "parallel", "arbitrary"),
        name="nsa_attn",
    )(qT, gT, kcmp, vcmpT, ks, vsT, kw, vwT, covT, eblk)


def _memkv_kernel(mem_ref, wk_ref, wvT_ref, k_ref, vT_ref):
    m = mem_ref[0].astype(bf16)
    k_ref[0] = jnp.dot(m, wk_ref[...], preferred_element_type=f32).astype(bf16)
    vT_ref[0] = _nt(wvT_ref[...], m).astype(bf16)


def _memkv(mem, wk, wvT):
    B, M, _ = mem.shape
    return pl.pallas_call(
        _memkv_kernel,
        grid=(B,),
        in_specs=[pl.BlockSpec((1, M, D_MODEL), lambda b: (b, 0, 0)),
                  _const_spec((D_MODEL, 512)), _const_spec((512, D_MODEL))],
        out_specs=[pl.BlockSpec((1, M, 512), lambda b: (b, 0, 0)),
                   pl.BlockSpec((1, 512, M), lambda b: (b, 0, 0))],
        out_shape=[jax.ShapeDtypeStruct((B, M, 512), bf16), jax.ShapeDtypeStruct((B, 512, M), bf16)],
        compiler_params=_params("parallel"),
        name="mem_kv",
    )(mem, wk, wvT)


def _memattn_kernel(qT_ref, k_ref, vT_ref, o_ref):
    for h in range(MEM_HEADS):
        rows = slice(h * MEM_HEAD_DIM, (h + 1) * MEM_HEAD_DIM)
        s = jnp.dot(k_ref[0, :, rows], qT_ref[rows, :], preferred_element_type=f32) * (MEM_HEAD_DIM ** -0.5)
        p = jnp.exp(s - jnp.max(s, axis=0, keepdims=True))
        l = jnp.sum(p, axis=0, keepdims=True)
        o = jnp.dot(vT_ref[0, rows, :], p.astype(bf16), preferred_element_type=f32) * (1.0 / l)
        o_ref[rows, :] = o.astype(bf16)


def _memattn(mqT, mk, mvT, B, S, tq):
    M = mk.shape[1]
    nq = S // tq
    return pl.pallas_call(
        _memattn_kernel,
        grid=(B, nq),
        in_specs=[pl.BlockSpec((512, tq), lambda b, i: (0, b * nq + i)),
                  pl.BlockSpec((1, M, 512), lambda b, i: (b, 0, 0)),
                  pl.BlockSpec((1, 512, M), lambda b, i: (b, 0, 0))],
        out_specs=pl.BlockSpec((512, tq), lambda b, i: (0, b * nq + i)),
        out_shape=jax.ShapeDtypeStruct((512, B * S), bf16),
        compiler_params=_params("parallel", "parallel"),
        name="mem_attn",
    )(mqT, mk, mvT)


def _outproj_kernel(x_ref, lng_ref, lnb_ref, ol_ref, onT_ref, omT_ref, wo_ref, g1_ref, b1_ref, h1_ref):
    h0 = _ln(x_ref[...], lng_ref[...], lnb_ref[...])
    mixed = jnp.dot(ol_ref[...], wo_ref[0:1024, :], preferred_element_type=f32)
    mixed = mixed + _tn(onT_ref[...], wo_ref[1024:1536, :])
    mixed = mixed + _tn(omT_ref[...], wo_ref[1536:2048, :])
    h1_ref[...] = _ln(ALPHA * h0 + mixed, g1_ref[...], b1_ref[...])


def _outproj(x2, lng, lnb, ol, onT, omT, wo, g1, b1, tm):
    T = x2.shape[0]
    row = lambda w: pl.BlockSpec((tm, w), lambda i: (i, 0))
    trs = pl.BlockSpec((512, tm), lambda i: (0, i))
    vec = _const_spec((1, D_MODEL))
    return pl.pallas_call(
        _outproj_kernel,
        grid=(T // tm,),
        in_specs=[row(D_MODEL), vec, vec, row(1024), trs, trs, _const_spec((D_MODEL, D_MODEL)), vec, vec],
        out_specs=row(D_MODEL),
        out_shape=jax.ShapeDtypeStruct((T, D_MODEL), f32),
        compiler_params=_params("parallel"),
        name="out_proj_ln1",
    )(x2, lng, lnb, ol, onT, omT, wo, g1, b1)


def _ffn_kernel(h_ref, halo_ref, wg_ref, wu_ref, cwg_ref, cwu_ref, cbg_ref, cbu_ref, wd_ref,
                g2_ref, b2_ref, o_ref, lhs_ref, *, tm, blocks_per_seq):
    i = pl.program_id(0)
    j = pl.program_id(1)

    @pl.when(j == 0)
    def _():
        first = (i % blocks_per_seq) == 0
        lhs_ref[0:8, :] = jnp.where(first, 0.0, halo_ref[...]).astype(bf16)
        lhs_ref[8:8 + tm, :] = h_ref[...].astype(bf16)
        o_ref[...] = jnp.zeros_like(o_ref)

    lhs = lhs_ref[...]

    def conv(w_ref, cw_ref, cb_ref):
        up = jnp.dot(lhs, w_ref[...], preferred_element_type=f32)
        y = cb_ref[...] + pltpu.roll(up, 2, 0)[8:8 + tm, :] * cw_ref[0:1, :]
        y = y + pltpu.roll(up, 1, 0)[8:8 + tm, :] * cw_ref[1:2, :]
        return y + up[8:8 + tm, :] * cw_ref[2:3, :]

    act = (_gelu(conv(wg_ref, cwg_ref, cbg_ref)) * conv(wu_ref, cwu_ref, cbu_ref)).astype(bf16)
    o_ref[...] += jnp.dot(act, wd_ref[...], preferred_element_type=f32)

    @pl.when(j == pl.num_programs(1) - 1)
    def _():
        o_ref[...] = _ln(ALPHA * h_ref[...] + o_ref[...], g2_ref[...], b2_ref[...])


def _ffn(h1, wup, cw, cb, wd, g2, b2, tm, tf, blocks_per_seq):
    T = h1.shape[0]
    nf = D_FF // tf
    vec = _const_spec((1, D_MODEL))
    return pl.pallas_call(
        functools.partial(_ffn_kernel, tm=tm, blocks_per_seq=blocks_per_seq),
        grid=(T // tm, nf),
        in_specs=[pl.BlockSpec((tm, D_MODEL), lambda i, j: (i, 0), pipeline_mode=pl.Buffered(1)),
                  pl.BlockSpec((8, D_MODEL), lambda i, j: (jnp.maximum(i * (tm // 8) - 1, 0), 0)),
                  pl.BlockSpec((D_MODEL, tf), lambda i, j: (0, j)),
                  pl.BlockSpec((D_MODEL, tf), lambda i, j: (0, nf + j)),
                  pl.BlockSpec((FFN_CONV_WIDTH, tf), lambda i, j: (0, j)),
                  pl.BlockSpec((FFN_CONV_WIDTH, tf), lambda i, j: (0, nf + j)),
                  pl.BlockSpec((1, tf), lambda i, j: (0, j)),
                  pl.BlockSpec((1, tf), lambda i, j: (0, nf + j)),
                  pl.BlockSpec((tf, D_MODEL), lambda i, j: (j, 0)),
                  vec, vec],
        out_specs=pl.BlockSpec((tm, D_MODEL), lambda i, j: (i, 0)),
        out_shape=jax.ShapeDtypeStruct((T, D_MODEL), f32),
        scratch_shapes=[pltpu.VMEM((tm + 8, D_MODEL), bf16)],
        compiler_params=_params("parallel", "arbitrary"),
        name="conv_ffn_ln2",
    )(h1, h1, wup, wup, cw, cw, cb, cb, wd, g2, b2)


def _layer(x, mem, ln_in_g, ln_in_b, w_in, lru_conv_w, lru_conv_b, lru_wa, lru_ba, lru_wx, lru_bx,
           lru_lam, cmp_pe_k, cmp_w1_k, cmp_w2_k, cmp_pe_v, cmp_w1_v, cmp_w2_v, w_mem_kv, w_out,
           ln1_g, ln1_b, ffn_w_up, ffn_conv_w, ffn_conv_b, ffn_w_down, ln2_g, ln2_b,
           *, tm_proj, tb_lru, tq_mem, tm_out, tm_ffn, tf_ffn):
    B, S, _ = x.shape
    T = B * S
    G, Dh = NSA_KV_HEADS, NSA_HEAD_DIM
    row = lambda v: v.reshape(1, -1)

    c_q = 2 * LRU_WIDTH
    c_kv = c_q + NSA_HEADS * Dh
    c_gate = c_kv + 6 * G * Dh
    c_mq = c_gate + 3 * NSA_HEADS
    kv_cols = [w_in[:, c_kv + n * G * Dh: c_kv + (n + 1) * G * Dh] for n in range(6)]
    wn = jnp.concatenate([w_in[:, :c_q], kv_cols[0], kv_cols[1], kv_cols[2], kv_cols[4]], axis=1).astype(bf16)
    w_gate = w_in[:, c_gate:c_mq].reshape(D_MODEL, G, NSA_HPG, 3)
    w_gate = jnp.pad(w_gate.transpose(1, 3, 2, 0), ((0, 0), (0, 1), (0, 0), (0, 0)))
    wt = jnp.concatenate([w_in[:, c_q:c_kv].T, kv_cols[3].T, kv_cols[5].T, w_in[:, c_mq:].T,
                          w_gate.reshape(2 * 16, D_MODEL)], axis=0).astype(bf16)

    wax = jnp.concatenate([lru_wa, lru_wx], axis=-1).astype(bf16)

    def cmp_weights(pe, w1, w2):
        pe2 = jnp.broadcast_to(pe.reshape(2, CMP_STRIDE, 1, Dh), (2, CMP_STRIDE, G, Dh)).reshape(2, -1)
        eye = jnp.eye(G, dtype=f32)
        w1e = jnp.einsum('hlde,gk->hlgdke', w1.reshape(2, CMP_STRIDE, Dh, Dh), eye)
        w1e = w1e.reshape(2, CMP_STRIDE * G * Dh, G * Dh).astype(bf16)
        w2e = jnp.einsum('ef,gk->gekf', w2, eye).reshape(G * Dh, G * Dh)
        return pe2, w1e, w2e

    pek, w1k, w2k = cmp_weights(cmp_pe_k, cmp_w1_k, cmp_w2_k)
    pev, w1v, w2v = cmp_weights(cmp_pe_v, cmp_w1_v, cmp_w2_v)

    n_slc, ncmp = S // SLC_BLOCK, S // CMP_STRIDE
    ci = jnp.arange(ncmp)[None, :] * CMP_STRIDE
    sj = jnp.arange(n_slc)[:, None] * SLC_BLOCK
    covT = ((ci <= sj + SLC_BLOCK - 1) & (ci + CMP_BLOCK - 1 >= sj)).astype(bf16)
    eblk = (jnp.arange(S)[:, None] // SLC_BLOCK == jnp.arange(128)[None, :]).astype(bf16)

    wk_mem = w_mem_kv[:, :512].astype(bf16)
    wvT_mem = w_mem_kv[:, 512:].T.astype(bf16)
    wo = w_out.astype(bf16)
    wup = ffn_w_up.astype(bf16)
    wd = ffn_w_down.astype(bf16)

    x2 = x.reshape(T, D_MODEL)
    lx, ly, kc, vc, ks, kw, qT, vsT, vwT, mqT, gT = _proj(x2, row(ln_in_g), row(ln_in_b), wn, wt, tm_proj)

    o_lru = _lru(lx.reshape(B, S, LRU_WIDTH), ly.reshape(B, S, LRU_WIDTH), lru_conv_w, row(lru_conv_b),
                 wax, row(lru_ba), row(lru_bx), row(lru_lam), tb_lru)

    sub = lambda a: a.reshape(B, ncmp, CMP_STRIDE * G * Dh)
    kcmp, vcmpT = _compress(sub(kc), sub(vc), pek, w1k, w2k.astype(bf16), pev, w1v, w2v.T.astype(bf16))
    o_nsaT = _nsa(qT, gT, kcmp, vcmpT, ks.reshape(B, S, 128), vsT, kw.reshape(B, S, 128), vwT, covT, eblk, B, S)

    mk, mvT = _memkv(mem, wk_mem, wvT_mem)
    o_memT = _memattn(mqT, mk, mvT, B, S, tq_mem)

    h1 = _outproj(x2, row(ln_in_g), row(ln_in_b), o_lru.reshape(T, LRU_WIDTH), o_nsaT, o_memT, wo,
                  row(ln1_g), row(ln1_b), tm_out)
    out = _ffn(h1, wup, ffn_conv_w, row(ffn_conv_b), wd, row(ln2_g), row(ln2_b), tm_ffn, tf_ffn, S // tm_ffn)
    return out.reshape(B, S, D_MODEL)


def kernel(x, mem, ln_in_g, ln_in_b, w_in, lru_conv_w, lru_conv_b, lru_wa, lru_ba, lru_wx, lru_bx,
           lru_lam, cmp_pe_k, cmp_w1_k, cmp_w2_k, cmp_pe_v, cmp_w1_v, cmp_w2_v, w_mem_kv, w_out,
           ln1_g, ln1_b, ffn_w_up, ffn_conv_w, ffn_conv_b, ffn_w_down, ln2_g, ln2_b):
    return _layer(x, mem, ln_in_g, ln_in_b, w_in[0], lru_conv_w[0], lru_conv_b[0], lru_wa[0], lru_ba[0],
                  lru_wx[0], lru_bx[0], lru_lam[0], cmp_pe_k[0], cmp_w1_k[0], cmp_w2_k[0], cmp_pe_v[0],
                  cmp_w1_v[0], cmp_w2_v[0], w_mem_kv[0], w_out[0], ln1_g[0], ln1_b[0], ffn_w_up[0],
                  ffn_conv_w[0], ffn_conv_b[0], ffn_w_down[0], ln2_g[0], ln2_b[0],
                  tm_proj=512, tb_lru=256, tq_mem=512, tm_out=512, tm_ffn=1024, tf_ffn=512)
```

```python
import functools
import math

import jax
import jax.numpy as jnp
from jax import lax
from jax.experimental import pallas as pl
from jax.experimental.pallas import tpu as pltpu

f32 = jnp.float32
bf16 = jnp.bfloat16

D_MODEL = 2048
LRU_WIDTH = 1024
LRU_BLOCKS = 8
LRU_BLOCK_DIM = 128
LRU_CONV_WIDTH = 4
LRU_C = 8.0
LRU_TINY = 1e-30
NSA_HEADS = 8
NSA_KV_HEADS = 2
NSA_HPG = NSA_HEADS // NSA_KV_HEADS
NSA_HEAD_DIM = 64
NSA_Q_SCALE = NSA_HEAD_DIM ** -0.5 * math.log2(math.e)
CMP_STRIDE = 16
CMP_BLOCK = 32
SLC_BLOCK = 64
N_SELECT = 16
WINDOW = 512
Q_BLOCK = 128
MEM_HEADS = 4
MEM_HEAD_DIM = 128
D_FF = 5632
FFN_CONV_WIDTH = 3
LN_EPS = 1e-5
NEG_INF = -1e30
FORCE_SCORE = 1e9
TAKEN_SCORE = -3e38
ALPHA = 2.0 ** 0.25

V7X_VMEM_LIMIT_BYTES = 56 * 1024 * 1024

_TQ0, _TVS0, _TVW0, _TMQ0, _TG0, _TROWS = 0, 512, 640, 768, 1280, 1312
_NX0, _NY0, _NKC0, _NVC0, _NKS0, _NKW0, _NCOLS = 0, 1024, 2048, 2176, 2304, 2432, 2560

SEL_CHUNK = 512
WIN_KEYS = WINDOW + Q_BLOCK
ONES_ROWS = 16


def _ln(x, g, b):
    mu = jnp.mean(x, axis=-1, keepdims=True)
    xc = x - mu
    var = jnp.mean(xc * xc, axis=-1, keepdims=True)
    return xc * lax.rsqrt(var + LN_EPS) * g + b


def _gelu(x):
    return jax.nn.gelu(x)


def _sigmoid(x):
    return 1.0 / (1.0 + jnp.exp(-x))


def _sigmoid_tanh(x):
    return 0.5 * jnp.tanh(0.5 * x) + 0.5


def _nt(a, b):
    return lax.dot_general(a, b, (((1,), (1,)), ((), ())), preferred_element_type=f32)


def _tn(a, b):
    return lax.dot_general(a, b, (((0,), (0,)), ((), ())), preferred_element_type=f32)


def _params(*sem, flags=None):
    return pltpu.CompilerParams(dimension_semantics=sem, vmem_limit_bytes=V7X_VMEM_LIMIT_BYTES, flags=flags)


def _const_spec(shape):
    nd = len(shape)
    return pl.BlockSpec(shape, lambda *_: (0,) * nd, pipeline_mode=pl.Buffered(1))


def _proj_kernel(x_ref, g_ref, b_ref, wn_ref, wt_ref,
                 lx_ref, ly_ref, kc_ref, vc_ref, ks_ref, kw_ref,
                 qT_ref, vsT_ref, vwT_ref, mqT_ref, gT_ref):
    h = _ln(x_ref[...], g_ref[...], b_ref[...]).astype(bf16)

    def nat(c0, c1):
        return jnp.dot(h, wn_ref[:, c0:c1], preferred_element_type=f32)

    lx_ref[...] = nat(_NX0, _NY0)
    ly_ref[...] = nat(_NY0, _NKC0)
    kc_ref[...] = nat(_NKC0, _NVC0)
    vc_ref[...] = nat(_NVC0, _NKS0)
    ks_ref[...] = nat(_NKS0, _NKW0).astype(bf16)
    kw_ref[...] = nat(_NKW0, _NCOLS).astype(bf16)

    def tr(r0, r1):
        return _nt(wt_ref[r0:r1, :], h)

    qT_ref[...] = (tr(_TQ0, _TVS0) * NSA_Q_SCALE).astype(bf16)
    vsT_ref[...] = tr(_TVS0, _TVW0).astype(bf16)
    vwT_ref[...] = tr(_TVW0, _TMQ0).astype(bf16)
    mqT_ref[...] = tr(_TMQ0, _TG0).astype(bf16)
    gT_ref[...] = tr(_TG0, _TROWS)


def _proj(x2, g, b, wn, wt, tm):
    T = x2.shape[0]
    nat = lambda w: pl.BlockSpec((tm, w), lambda i: (i, 0))
    trs = lambda r: pl.BlockSpec((r, tm), lambda i: (0, i))
    return pl.pallas_call(
        _proj_kernel,
        grid=(T // tm,),
        in_specs=[pl.BlockSpec((tm, D_MODEL), lambda i: (i, 0)),
                  _const_spec((1, D_MODEL)), _const_spec((1, D_MODEL)),
                  _const_spec((D_MODEL, _NCOLS)), _const_spec((_TROWS, D_MODEL))],
        out_specs=[nat(1024), nat(1024), nat(128), nat(128), nat(128), nat(128),
                   trs(512), trs(128), trs(128), trs(512), trs(32)],
        out_shape=[jax.ShapeDtypeStruct((T, 1024), f32), jax.ShapeDtypeStruct((T, 1024), f32),
                   jax.ShapeDtypeStruct((T, 128), f32), jax.ShapeDtypeStruct((T, 128), f32),
                   jax.ShapeDtypeStruct((T, 128), bf16), jax.ShapeDtypeStruct((T, 128), bf16),
                   jax.ShapeDtypeStruct((512, T), bf16), jax.ShapeDtypeStruct((128, T), bf16),
                   jax.ShapeDtypeStruct((128, T), bf16), jax.ShapeDtypeStruct((512, T), bf16),
                   jax.ShapeDtypeStruct((32, T), f32)],
        compiler_params=_params("parallel"),
        name="proj",
    )(x2, g, b, wn, wt)


def _lru_pitch(tb):
    seg = tb // 8
    return seg + 8 if (seg // 8) % 2 == 0 else seg


def _lru_kernel(x_ref, y_ref, cw_ref, cb_ref, wax_ref, ba_ref, bx_ref, lam_ref, o_ref,
                xext_ref, a_ref, u_ref, carry_ref, *, tb):
    s = pl.program_id(1)

    @pl.when(s == 0)
    def _():
        xext_ref[0:8, :] = jnp.zeros((8, LRU_WIDTH), f32)
        carry_ref[...] = jnp.zeros((1, LRU_WIDTH), f32)

    x = x_ref[0]
    xext_ref[8:8 + tb, :] = x
    xc = cb_ref[...]
    for k in range(LRU_CONV_WIDTH):
        off = 8 - (LRU_CONV_WIDTH - 1) + k
        xc = xc + xext_ref[off:off + tb, :] * cw_ref[k:k + 1, :]
    xext_ref[0:8, :] = x[tb - 8:tb, :]

    xb = xc.astype(bf16)
    gates = [jnp.dot(xb[:, n * 128:(n + 1) * 128], wax_ref[n], preferred_element_type=f32)
             for n in range(LRU_BLOCKS)]
    r = _sigmoid_tanh(jnp.concatenate([gt[:, 0:128] for gt in gates], axis=1) + ba_ref[...])
    i = _sigmoid_tanh(jnp.concatenate([gt[:, 128:256] for gt in gates], axis=1) + bx_ref[...])
    nl = -lam_ref[...]
    softplus = jnp.maximum(nl, 0.0) + jnp.log1p(jnp.exp(-jnp.abs(nl)))
    log_a = (-LRU_C) * r * softplus
    a = jnp.exp(log_a)
    th = jnp.tanh(log_a)
    z = -2.0 * th / (1.0 - th)
    u = (z * lax.rsqrt(jnp.maximum(z, LRU_TINY))) * (i * xc)
    seg = tb // 8
    pitch = _lru_pitch(tb)
    for n in range(LRU_BLOCKS):
        for r8 in range(8):
            a_ref[n, r8 * pitch:r8 * pitch + seg, :] = a[r8 * seg:(r8 + 1) * seg, n * 128:(n + 1) * 128]
            u_ref[n, r8 * pitch:r8 * pitch + seg, :] = u[r8 * seg:(r8 + 1) * seg, n * 128:(n + 1) * 128]

    def sweep(k, carry):
        rows = pl.ds(k, 8, stride=pitch)
        out = []
        for n in range(LRU_BLOCKS):
            h, p = carry[n]
            ak = a_ref[n, rows, :]
            h = ak * h + u_ref[n, rows, :]
            p = ak * p
            u_ref[n, rows, :] = h
            a_ref[n, rows, :] = p
            out.append((h, p))
        return tuple(out)

    init = tuple((jnp.zeros((8, 128), f32), jnp.ones((8, 128), f32)) for _ in range(LRU_BLOCKS))
    ends = lax.fori_loop(0, seg, sweep, init, unroll=4)
    h_end = jnp.concatenate([e[0] for e in ends], axis=1)
    p_end = jnp.concatenate([e[1] for e in ends], axis=1)

    c = carry_ref[...]
    for r8 in range(8):
        rows = slice(r8 * seg, (r8 + 1) * seg)
        prow = slice(r8 * pitch, r8 * pitch + seg)
        h_loc = jnp.concatenate([u_ref[n, prow, :] for n in range(LRU_BLOCKS)], axis=1)
        p_cum = jnp.concatenate([a_ref[n, prow, :] for n in range(LRU_BLOCKS)], axis=1)
        o_ref[0, rows, :] = (_gelu(y_ref[0, rows, :]) * (h_loc + p_cum * c)).astype(bf16)
        c = h_end[r8:r8 + 1, :] + p_end[r8:r8 + 1, :] * c
    carry_ref[...] = c


def _lru(lx, ly, cw, cb, wax, ba, bx, lam, tb):
    B, S, _ = lx.shape
    blk = pl.BlockSpec((1, tb, LRU_WIDTH), lambda b, s: (b, s, 0))
    return pl.pallas_call(
        functools.partial(_lru_kernel, tb=tb),
        grid=(B, S // tb),
        in_specs=[blk, blk, _const_spec((LRU_CONV_WIDTH, LRU_WIDTH)), _const_spec((1, LRU_WIDTH)),
                  _const_spec((LRU_BLOCKS, 128, 256)), _const_spec((1, LRU_WIDTH)),
                  _const_spec((1, LRU_WIDTH)), _const_spec((1, LRU_WIDTH))],
        out_specs=blk,
        out_shape=jax.ShapeDtypeStruct((B, S, LRU_WIDTH), bf16),
        scratch_shapes=[pltpu.VMEM((tb + 8, LRU_WIDTH), f32),
                        pltpu.VMEM((LRU_BLOCKS, 8 * _lru_pitch(tb), 128), f32),
                        pltpu.VMEM((LRU_BLOCKS, 8 * _lru_pitch(tb), 128), f32),
                        pltpu.VMEM((1, LRU_WIDTH), f32)],
        compiler_params=_params("parallel", "arbitrary"),
        name="lru",
    )(lx, ly, cw, cb, wax, ba, bx, lam)


def _cmp_kernel(kc_ref, vc_ref, pek_ref, w1k_ref, w2k_ref, pev_ref, w1v_ref, w2vT_ref,
                kcmp_ref, vcmpT_ref):
    nsub = kc_ref.shape[1]

    def hidden(sub_ref, pe_ref, w1_ref):
        sub = sub_ref[0]
        y0 = jnp.dot((sub + pe_ref[0:1, :]).astype(bf16), w1_ref[0], preferred_element_type=f32)
        y1 = jnp.dot((sub + pe_ref[1:2, :]).astype(bf16), w1_ref[1], preferred_element_type=f32)
        return _gelu(y0 + pltpu.roll(y1, nsub - 1, 0)).astype(bf16)

    kcmp_ref[0] = jnp.dot(hidden(kc_ref, pek_ref, w1k_ref), w2k_ref[...],
                          preferred_element_type=f32).astype(bf16)
    vcmpT_ref[0] = _nt(w2vT_ref[...], hidden(vc_ref, pev_ref, w1v_ref)).astype(bf16)


def _compress(kc, vc, pek, w1k, w2k, pev, w1v, w2vT):
    B, nsub, width = kc.shape
    blk = pl.BlockSpec((1, nsub, width), lambda b: (b, 0, 0))
    return pl.pallas_call(
        _cmp_kernel,
        grid=(B,),
        in_specs=[blk, blk, _const_spec((2, width)), _const_spec((2, width, 128)), _const_spec((128, 128)),
                  _const_spec((2, width)), _const_spec((2, width, 128)), _const_spec((128, 128))],
        out_specs=[pl.BlockSpec((1, nsub, 128), lambda b: (b, 0, 0)),
                   pl.BlockSpec((1, 128, nsub), lambda b: (b, 0, 0))],
        out_shape=[jax.ShapeDtypeStruct((B, nsub, 128), bf16), jax.ShapeDtypeStruct((B, 128, nsub), bf16)],
        compiler_params=_params("parallel"),
        name="nsa_compress",
    )(kc, vc, pek, w1k, w2k, pev, w1v, w2vT)


def _nsa_kernel(qT_ref, gT_ref, kcmp_ref, vcmpT_ref, ks_ref, vsT_ref, kw_ref, vwT_ref, covT_ref, eblk_ref,
                o_ref, qaug_ref, s_ref, acc_ref, m_ref, smax_ref, *, n_slc, n_top):
    g = pl.program_id(1)
    i = pl.program_id(2)
    q0 = i * Q_BLOCK
    grow = pl.multiple_of(g * NSA_HEAD_DIM, NSA_HEAD_DIM)
    HQ = NSA_HPG * Q_BLOCK

    def per_head(m):
        return jnp.concatenate([m] * NSA_HPG, axis=1)

    qh = jnp.concatenate([qT_ref[hp * 64:(hp + 1) * 64, :] for hp in range(NSA_HPG)], axis=1)
    zq = jnp.zeros_like(qh)
    qpad = jnp.concatenate([jnp.where(g == 0, qh, zq), jnp.where(g == 1, qh, zq)], axis=0)
    t_row = q0 + lax.broadcasted_iota(jnp.int32, (1, Q_BLOCK), 1)

    ncmp = kcmp_ref.shape[1]

    def compressed(rows):
        n_iota = lax.broadcasted_iota(jnp.int32, (rows, Q_BLOCK), 0)
        bias_c = jnp.where((n_iota * CMP_STRIDE + (CMP_BLOCK - 1)) <= t_row, 0.0, NEG_INF)
        sc = jnp.dot(kcmp_ref[0, 0:rows, :], qpad, preferred_element_type=f32) + per_head(bias_c)
        pc = jnp.exp2(sc - jnp.max(sc, axis=0, keepdims=True))
        lc = jnp.sum(pc, axis=0, keepdims=True)
        pc = pc * jnp.where(per_head(t_row) >= CMP_BLOCK - 1, 1.0 / lc, 0.0)
        o = jnp.dot(vcmpT_ref[0, pl.ds(grow, 64), 0:rows], pc.astype(bf16), preferred_element_type=f32)
        psum = pc[:, 0:Q_BLOCK]
        for hp in range(1, NSA_HPG):
            psum = psum + pc[:, hp * Q_BLOCK:(hp + 1) * Q_BLOCK]
        hi = psum.astype(bf16)
        r1 = psum - hi.astype(f32)
        mid = r1.astype(bf16)
        lo = (r1 - mid.astype(f32)).astype(bf16)
        cov = covT_ref[:, 0:rows]
        return o, (jnp.dot(cov, hi, preferred_element_type=f32) + jnp.dot(cov, mid, preferred_element_type=f32)
                   + jnp.dot(cov, lo, preferred_element_type=f32))

    row_options = [r for r in (ncmp // 4, ncmp // 2, 3 * ncmp // 4, ncmp) if r % 128 == 0]
    need = (Q_BLOCK // CMP_STRIDE) * (i + 1)
    which = sum((need > r).astype(jnp.int32) for r in row_options[:-1])
    oc, imp = lax.switch(which, [functools.partial(compressed, r) for r in row_options])

    j_i = lax.broadcasted_iota(jnp.int32, (n_slc, Q_BLOCK), 0)
    j_f = j_i.astype(f32)
    cur = t_row >> 6
    forced = (j_i == 0) | (j_i == cur) | (j_i == cur - 1)
    visible = (j_i * SLC_BLOCK) <= t_row
    score = jnp.where(forced, TAKEN_SCORE, jnp.where(visible, imp, NEG_INF))

    w0 = pl.multiple_of(jnp.maximum(q0 - WINDOW, 0), Q_BLOCK)
    kp = w0 + lax.broadcasted_iota(jnp.int32, (WIN_KEYS, Q_BLOCK), 0)
    bias_w = jnp.where((kp <= t_row) & (kp > (t_row - WINDOW)), 0.0, NEG_INF)
    sw = jnp.dot(kw_ref[0, pl.ds(w0, WIN_KEYS), :], qpad, preferred_element_type=f32) + per_head(bias_w)
    pw = jnp.exp2(sw - jnp.max(sw, axis=0, keepdims=True)).astype(bf16)
    vw1 = jnp.concatenate([vwT_ref[pl.ds(grow, 64), pl.ds(w0, WIN_KEYS)],
                           jnp.ones((ONES_ROWS, WIN_KEYS), bf16)], axis=0)
    ow = jnp.dot(vw1, pw, preferred_element_type=f32)
    ow = ow[0:NSA_HEAD_DIM, :] * (1.0 / ow[NSA_HEAD_DIM:NSA_HEAD_DIM + 1, :])

    sel = jnp.where(forced, 1.0, 0.0)
    for _ in range(n_top - 3):
        mx = jnp.max(score, axis=0, keepdims=True)
        first = jnp.min(jnp.where(score == mx, j_f, float(n_slc)), axis=0, keepdims=True)
        hit = j_f == first
        sel = jnp.where(hit, 1.0, sel)
        score = jnp.where(hit, TAKEN_SCORE, score)

    qaug_ref[0:2 * NSA_HEAD_DIM, :] = qpad
    blk_bias = jnp.where((sel > 0.0) & visible, 0.0, NEG_INF)
    qaug_ref[2 * NSA_HEAD_DIM:2 * NSA_HEAD_DIM + n_slc, :] = per_head(blk_bias).astype(bf16)
    if n_slc < 128:
        qaug_ref[2 * NSA_HEAD_DIM + n_slc:, :] = jnp.zeros((128 - n_slc, HQ), bf16)

    def scores(c):
        k0 = pl.multiple_of(c * SEL_CHUNK, SEL_CHUNK)
        kaug = jnp.concatenate([ks_ref[0, pl.ds(k0, SEL_CHUNK), :], eblk_ref[pl.ds(k0, SEL_CHUNK), :]], axis=1)
        return jnp.dot(kaug, qaug_ref[...], preferred_element_type=f32)

    def attend(c, s, smax):
        k0 = pl.multiple_of(c * SEL_CHUNK, SEL_CHUNK)
        m = m_ref[...]
        m_new = jnp.maximum(m, smax)
        m_ref[...] = m_new
        p = jnp.exp2(s - m_new).astype(bf16)
        v1 = jnp.concatenate([vsT_ref[pl.ds(grow, 64), pl.ds(k0, SEL_CHUNK)],
                              jnp.ones((ONES_ROWS, SEL_CHUNK), bf16)], axis=0)
        acc_ref[...] = jnp.exp2(m - m_new) * acc_ref[...] + jnp.dot(v1, p, preferred_element_type=f32)

    def step(c, cur, nxt):
        smax = smax_ref[...]
        s_next = scores(c + 1)
        s_ref[nxt] = s_next
        smax_ref[...] = jnp.max(s_next, axis=0, keepdims=True)
        attend(c, s_ref[cur], smax)

    last = q0 // SEL_CHUNK
    odd = last % 2
    s0 = scores(0)
    s_ref[odd] = s0
    smax_ref[...] = jnp.max(s0, axis=0, keepdims=True)
    acc_ref[...] = jnp.zeros((NSA_HEAD_DIM + ONES_ROWS, HQ), f32)
    m_ref[...] = jnp.full((1, HQ), NEG_INF, f32)

    @pl.when(odd == 1)
    def _():
        step(0, 1, 0)

    has_pair = (last // 2) % 2

    @pl.when(has_pair == 1)
    def _():
        step(odd, 0, 1)
        step(odd + 1, 1, 0)

    def quad(qq, carry):
        c = odd + 2 * has_pair + 4 * qq
        step(c, 0, 1)
        step(c + 1, 1, 0)
        step(c + 2, 0, 1)
        step(c + 3, 1, 0)
        return carry

    lax.fori_loop(0, last // 4, quad, 0)

    kp_d = last * SEL_CHUNK + lax.broadcasted_iota(jnp.int32, (SEL_CHUNK, Q_BLOCK), 0)
    s_d = s_ref[0] + per_head(jnp.where(kp_d <= t_row, 0.0, NEG_INF))
    attend(last, s_d, jnp.max(s_d, axis=0, keepdims=True))
    osel = acc_ref[0:NSA_HEAD_DIM, :] * (1.0 / acc_ref[NSA_HEAD_DIM:NSA_HEAD_DIM + 1, :])

    gate = _sigmoid(gT_ref[...])
    for hp in range(NSA_HPG):
        cols = slice(hp * Q_BLOCK, (hp + 1) * Q_BLOCK)
        o = (gate[hp:hp + 1, :] * oc[:, cols] + gate[4 + hp:5 + hp, :] * osel[:, cols]
             + gate[8 + hp:9 + hp, :] * ow[:, cols])
        o_ref[hp * 64:(hp + 1) * 64, :] = o.astype(bf16)


def _nsa(qT, gT, kcmp, vcmpT, ks, vsT, kw, vwT, covT, eblk, B, S):
    n_q = S // Q_BLOCK
    n_slc = S // SLC_BLOCK
    ncmp = S // CMP_STRIDE
    tok = lambda r: pl.BlockSpec((r, Q_BLOCK), lambda b, g, i: (g, b * n_q + i))
    per_b_nat = pl.BlockSpec((1, S, 128), lambda b, g, i: (b, 0, 0))
    per_b_tr = pl.BlockSpec((128, S), lambda b, g, i: (0, b))
    return pl.pallas_call(
        functools.partial(_nsa_kernel, n_slc=n_slc, n_top=min(N_SELECT, n_slc)),
        grid=(B, NSA_KV_HEADS, n_q),
        in_specs=[tok(256), tok(16),
                  pl.BlockSpec((1, ncmp, 128), lambda b, g, i: (b, 0, 0)),
                  pl.BlockSpec((1, 128, ncmp), lambda b, g, i: (b, 0, 0)),
                  per_b_nat, per_b_tr, per_b_nat, per_b_tr,
                  _const_spec((n_slc, ncmp)), _const_spec((S, 128))],
        out_specs=tok(256),
        out_shape=jax.ShapeDtypeStruct((512, B * S), bf16),
        scratch_shapes=[pltpu.VMEM((2 * NSA_HEAD_DIM + 128, NSA_HPG * Q_BLOCK), bf16),
                        pltpu.VMEM((2, SEL_CHUNK, NSA_HPG * Q_BLOCK), f32),
                        pltpu.VMEM((NSA_HEAD_DIM + ONES_ROWS, NSA_HPG * Q_BLOCK), f32)]
        + [pltpu.VMEM((1, NSA_HPG * Q_BLOCK), f32)] * 2,
        compiler_params=_params("parallel", "parallel", "arbitrary"),
        name="nsa_attn",
    )(qT, gT, kcmp, vcmpT, ks, vsT, kw, vwT, covT, eblk)


def _memkv_kernel(mem_ref, wk_ref, wvT_ref, k_ref, vT_ref):
    m = mem_ref[0].astype(bf16)
    k_ref[0] = jnp.dot(m, wk_ref[...], preferred_element_type=f32).astype(bf16)
    vT_ref[0] = _nt(wvT_ref[...], m).astype(bf16)


def _memkv(mem, wk, wvT):
    B, M, _ = mem.shape
    return pl.pallas_call(
        _memkv_kernel,
        grid=(B,),
        in_specs=[pl.BlockSpec((1, M, D_MODEL), lambda b: (b, 0, 0)),
                  _const_spec((D_MODEL, 512)), _const_spec((512, D_MODEL))],
        out_specs=[pl.BlockSpec((1, M, 512), lambda b: (b, 0, 0)),
                   pl.BlockSpec((1, 512, M), lambda b: (b, 0, 0))],
        out_shape=[jax.ShapeDtypeStruct((B, M, 512), bf16), jax.ShapeDtypeStruct((B, 512, M), bf16)],
        compiler_params=_params("parallel"),
        name="mem_kv",
    )(mem, wk, wvT)


def _memattn_kernel(qT_ref, k_ref, vT_ref, o_ref):
    for h in range(MEM_HEADS):
        rows = slice(h * MEM_HEAD_DIM, (h + 1) * MEM_HEAD_DIM)
        s = jnp.dot(k_ref[0, :, rows], qT_ref[rows, :], preferred_element_type=f32) * (MEM_HEAD_DIM ** -0.5)
        p = jnp.exp(s - jnp.max(s, axis=0, keepdims=True))
        l = jnp.sum(p, axis=0, keepdims=True)
        o = jnp.dot(vT_ref[0, rows, :], p.astype(bf16), preferred_element_type=f32) * (1.0 / l)
        o_ref[rows, :] = o.astype(bf16)


def _memattn(mqT, mk, mvT, B, S, tq):
    M = mk.shape[1]
    nq = S // tq
    return pl.pallas_call(
        _memattn_kernel,
        grid=(B, nq),
        in_specs=[pl.BlockSpec((512, tq), lambda b, i: (0, b * nq + i)),
                  pl.BlockSpec((1, M, 512), lambda b, i: (b, 0, 0)),
                  pl.BlockSpec((1, 512, M), lambda b, i: (b, 0, 0))],
        out_specs=pl.BlockSpec((512, tq), lambda b, i: (0, b * nq + i)),
        out_shape=jax.ShapeDtypeStruct((512, B * S), bf16),
        compiler_params=_params("parallel", "parallel"),
        name="mem_attn",
    )(mqT, mk, mvT)


def _outproj_kernel(x_ref, lng_ref, lnb_ref, ol_ref, onT_ref, omT_ref, wo_ref, g1_ref, b1_ref, h1_ref):
    h0 = _ln(x_ref[...], lng_ref[...], lnb_ref[...])
    mixed = jnp.dot(ol_ref[...], wo_ref[0:1024, :], preferred_element_type=f32)
    mixed = mixed + _tn(onT_ref[...], wo_ref[1024:1536, :])
    mixed = mixed + _tn(omT_ref[...], wo_ref[1536:2048, :])
    h1_ref[...] = _ln(ALPHA * h0 + mixed, g1_ref[...], b1_ref[...])


def _outproj(x2, lng, lnb, ol, onT, omT, wo, g1, b1, tm):
    T = x2.shape[0]
    row = lambda w: pl.BlockSpec((tm, w), lambda i: (i, 0))
    trs = pl.BlockSpec((512, tm), lambda i: (0, i))
    vec = _const_spec((1, D_MODEL))
    return pl.pallas_call(
        _outproj_kernel,
        grid=(T // tm,),
        in_specs=[row(D_MODEL), vec, vec, row(1024), trs, trs, _const_spec((D_MODEL, D_MODEL)), vec, vec],
        out_specs=row(D_MODEL),
        out_shape=jax.ShapeDtypeStruct((T, D_MODEL), f32),
        compiler_params=_params("parallel"),
        name="out_proj_ln1",
    )(x2, lng, lnb, ol, onT, omT, wo, g1, b1)


def _ffn_kernel(h_ref, halo_ref, wg_ref, wu_ref, cwg_ref, cwu_ref, cbg_ref, cbu_ref, wd_ref,
                g2_ref, b2_ref, o_ref, lhs_ref, *, tm, blocks_per_seq):
    i = pl.program_id(0)
    j = pl.program_id(1)

    @pl.when(j == 0)
    def _():
        first = (i % blocks_per_seq) == 0
        lhs_ref[0:8, :] = jnp.where(first, 0.0, halo_ref[...]).astype(bf16)
        lhs_ref[8:8 + tm, :] = h_ref[...].astype(bf16)
        o_ref[...] = jnp.zeros_like(o_ref)

    lhs = lhs_ref[...]

    def conv(w_ref, cw_ref, cb_ref):
        up = jnp.dot(lhs, w_ref[...], preferred_element_type=f32)
        y = cb_ref[...] + pltpu.roll(up, 2, 0)[8:8 + tm, :] * cw_ref[0:1, :]
        y = y + pltpu.roll(up, 1, 0)[8:8 + tm, :] * cw_ref[1:2, :]
        return y + up[8:8 + tm, :] * cw_ref[2:3, :]

    act = (_gelu(conv(wg_ref, cwg_ref, cbg_ref)) * conv(wu_ref, cwu_ref, cbu_ref)).astype(bf16)
    o_ref[...] += jnp.dot(act, wd_ref[...], preferred_element_type=f32)

    @pl.when(j == pl.num_programs(1) - 1)
    def _():
        o_ref[...] = _ln(ALPHA * h_ref[...] + o_ref[...], g2_ref[...], b2_ref[...])


def _ffn(h1, wup, cw, cb, wd, g2, b2, tm, tf, blocks_per_seq):
    T = h1.shape[0]
    nf = D_FF // tf
    vec = _const_spec((1, D_MODEL))
    return pl.pallas_call(
        functools.partial(_ffn_kernel, tm=tm, blocks_per_seq=blocks_per_seq),
        grid=(T // tm, nf),
        in_specs=[pl.BlockSpec((tm, D_MODEL), lambda i, j: (i, 0), pipeline_mode=pl.Buffered(1)),
                  pl.BlockSpec((8, D_MODEL), lambda i, j: (jnp.maximum(i * (tm // 8) - 1, 0), 0)),
                  pl.BlockSpec((None, D_MODEL, tf), lambda i, j: (j, 0, 0)),
                  pl.BlockSpec((None, D_MODEL, tf), lambda i, j: (nf + j, 0, 0)),
                  pl.BlockSpec((FFN_CONV_WIDTH, tf), lambda i, j: (0, j)),
                  pl.BlockSpec((FFN_CONV_WIDTH, tf), lambda i, j: (0, nf + j)),
                  pl.BlockSpec((1, tf), lambda i, j: (0, j)),
                  pl.BlockSpec((1, tf), lambda i, j: (0, nf + j)),
                  pl.BlockSpec((tf, D_MODEL), lambda i, j: (j, 0)),
                  vec, vec],
        out_specs=pl.BlockSpec((tm, D_MODEL), lambda i, j: (i, 0)),
        out_shape=jax.ShapeDtypeStruct((T, D_MODEL), f32),
        scratch_shapes=[pltpu.VMEM((tm + 8, D_MODEL), bf16)],
        compiler_params=_params("parallel", "arbitrary"),
        name="conv_ffn_ln2",
    )(h1, h1, wup, wup, cw, cw, cb, cb, wd, g2, b2)


def _layer(x, mem, ln_in_g, ln_in_b, w_in, lru_conv_w, lru_conv_b, lru_wa, lru_ba, lru_wx, lru_bx,
           lru_lam, cmp_pe_k, cmp_w1_k, cmp_w2_k, cmp_pe_v, cmp_w1_v, cmp_w2_v, w_mem_kv, w_out,
           ln1_g, ln1_b, ffn_w_up, ffn_conv_w, ffn_conv_b, ffn_w_down, ln2_g, ln2_b,
           *, tm_proj, tb_lru, tq_mem, tm_out, tm_ffn, tf_ffn):
    B, S, _ = x.shape
    T = B * S
    G, Dh = NSA_KV_HEADS, NSA_HEAD_DIM
    row = lambda v: v.reshape(1, -1)

    c_q = 2 * LRU_WIDTH
    c_kv = c_q + NSA_HEADS * Dh
    c_gate = c_kv + 6 * G * Dh
    c_mq = c_gate + 3 * NSA_HEADS
    kv_cols = [w_in[:, c_kv + n * G * Dh: c_kv + (n + 1) * G * Dh] for n in range(6)]
    wn = jnp.concatenate([w_in[:, :c_q], kv_cols[0], kv_cols[1], kv_cols[2], kv_cols[4]], axis=1).astype(bf16)
    w_gate = w_in[:, c_gate:c_mq].reshape(D_MODEL, G, NSA_HPG, 3)
    w_gate = jnp.pad(w_gate.transpose(1, 3, 2, 0), ((0, 0), (0, 1), (0, 0), (0, 0)))
    wt = jnp.concatenate([w_in[:, c_q:c_kv].T, kv_cols[3].T, kv_cols[5].T, w_in[:, c_mq:].T,
                          w_gate.reshape(2 * 16, D_MODEL)], axis=0).astype(bf16)

    wax = jnp.concatenate([lru_wa, lru_wx], axis=-1).astype(bf16)

    def cmp_weights(pe, w1, w2):
        pe2 = jnp.broadcast_to(pe.reshape(2, CMP_STRIDE, 1, Dh), (2, CMP_STRIDE, G, Dh)).reshape(2, -1)
        eye = jnp.eye(G, dtype=f32)
        w1e = jnp.einsum('hlde,gk->hlgdke', w1.reshape(2, CMP_STRIDE, Dh, Dh), eye)
        w1e = w1e.reshape(2, CMP_STRIDE * G * Dh, G * Dh).astype(bf16)
        w2e = jnp.einsum('ef,gk->gekf', w2, eye).reshape(G * Dh, G * Dh)
        return pe2, w1e, w2e

    pek, w1k, w2k = cmp_weights(cmp_pe_k, cmp_w1_k, cmp_w2_k)
    pev, w1v, w2v = cmp_weights(cmp_pe_v, cmp_w1_v, cmp_w2_v)

    n_slc, ncmp = S // SLC_BLOCK, S // CMP_STRIDE
    ci = jnp.arange(ncmp)[None, :] * CMP_STRIDE
    sj = jnp.arange(n_slc)[:, None] * SLC_BLOCK
    covT = ((ci <= sj + SLC_BLOCK - 1) & (ci + CMP_BLOCK - 1 >= sj)).astype(bf16)
    eblk = (jnp.arange(S)[:, None] // SLC_BLOCK == jnp.arange(128)[None, :]).astype(bf16)

    wk_mem = w_mem_kv[:, :512].astype(bf16)
    wvT_mem = w_mem_kv[:, 512:].T.astype(bf16)
    wo = w_out.astype(bf16)
    wup = ffn_w_up.reshape(D_MODEL, 2 * D_FF // tf_ffn, tf_ffn).transpose(1, 0, 2).astype(bf16)
    wd = ffn_w_down.astype(bf16)

    x2 = x.reshape(T, D_MODEL)
    lx, ly, kc, vc, ks, kw, qT, vsT, vwT, mqT, gT = _proj(x2, row(ln_in_g), row(ln_in_b), wn, wt, tm_proj)

    o_lru = _lru(lx.reshape(B, S, LRU_WIDTH), ly.reshape(B, S, LRU_WIDTH), lru_conv_w, row(lru_conv_b),
                 wax, row(lru_ba), row(lru_bx), row(lru_lam), tb_lru)

    sub = lambda a: a.reshape(B, ncmp, CMP_STRIDE * G * Dh)
    kcmp, vcmpT = _compress(sub(kc), sub(vc), pek, w1k, w2k.astype(bf16), pev, w1v, w2v.T.astype(bf16))
    o_nsaT = _nsa(qT, gT, kcmp, vcmpT, ks.reshape(B, S, 128), vsT, kw.reshape(B, S, 128), vwT, covT, eblk, B, S)

    mk, mvT = _memkv(mem, wk_mem, wvT_mem)
    o_memT = _memattn(mqT, mk, mvT, B, S, tq_mem)

    h1 = _outproj(x2, row(ln_in_g), row(ln_in_b), o_lru.reshape(T, LRU_WIDTH), o_nsaT, o_memT, wo,
                  row(ln1_g), row(ln1_b), tm_out)
    out = _ffn(h1, wup, ffn_conv_w, row(ffn_conv_b), wd, row(ln2_g), row(ln2_b), tm_ffn, tf_ffn, S // tm_ffn)
    return out.reshape(B, S, D_MODEL)


def kernel(x, mem, ln_in_g, ln_in_b, w_in, lru_conv_w, lru_conv_b, lru_wa, lru_ba, lru_wx, lru_bx,
           lru_lam, cmp_pe_k, cmp_w1_k, cmp_w2_k, cmp_pe_v, cmp_w1_v, cmp_w2_v, w_mem_kv, w_out,
           ln1_g, ln1_b, ffn_w_up, ffn_conv_w, ffn_conv_b, ffn_w_down, ln2_g, ln2_b):
    return _layer(x, mem, ln_in_g, ln_in_b, w_in[0], lru_conv_w[0], lru_conv_b[0], lru_wa[0], lru_ba[0],
                  lru_wx[0], lru_bx[0], lru_lam[0], cmp_pe_k[0], cmp_w1_k[0], cmp_w2_k[0], cmp_pe_v[0],
                  cmp_w1_v[0], cmp_w2_v[0], w_mem_kv[0], w_out[0], ln1_g[0], ln1_b[0], ffn_w_up[0],
                  ffn_conv_w[0], ffn_conv_b[0], ffn_w_down[0], ln2_g[0], ln2_b[0],
                  tm_proj=512, tb_lru=256, tq_mem=512, tm_out=512, tm_ffn=1024, tf_ffn=512)
```

```python
import functools
import math

import jax
import jax.numpy as jnp
from jax import lax
from jax.experimental import pallas as pl
from jax.experimental.pallas import tpu as pltpu

f32 = jnp.float32
bf16 = jnp.bfloat16

D_MODEL = 2048
LRU_WIDTH = 1024
LRU_BLOCKS = 8
LRU_BLOCK_DIM = 128
LRU_CONV_WIDTH = 4
LRU_C = 8.0
LRU_TINY = 1e-30
NSA_HEADS = 8
NSA_KV_HEADS = 2
NSA_HPG = NSA_HEADS // NSA_KV_HEADS
NSA_HEAD_DIM = 64
NSA_Q_SCALE = NSA_HEAD_DIM ** -0.5 * math.log2(math.e)
CMP_STRIDE = 16
CMP_BLOCK = 32
SLC_BLOCK = 64
N_SELECT = 16
WINDOW = 512
Q_BLOCK = 128
MEM_HEADS = 4
MEM_HEAD_DIM = 128
D_FF = 5632
FFN_CONV_WIDTH = 3
LN_EPS = 1e-5
NEG_INF = -1e30
FORCE_SCORE = 1e9
TAKEN_SCORE = -3e38
ALPHA = 2.0 ** 0.25

V7X_VMEM_LIMIT_BYTES = 56 * 1024 * 1024

_TQ0, _TVS0, _TVW0, _TMQ0, _TG0, _TROWS = 0, 512, 640, 768, 1280, 1312
_NX0, _NY0, _NKC0, _NVC0, _NKS0, _NKW0, _NCOLS = 0, 1024, 2048, 2176, 2304, 2432, 2560

SEL_CHUNK = 512
WIN_KEYS = WINDOW + Q_BLOCK
OUT_SUB_ROWS = 512
ONES_ROWS = 16


def _ln(x, g, b):
    mu = jnp.mean(x, axis=-1, keepdims=True)
    xc = x - mu
    var = jnp.mean(xc * xc, axis=-1, keepdims=True)
    return xc * lax.rsqrt(var + LN_EPS) * g + b


def _gelu(x):
    return jax.nn.gelu(x)


def _sigmoid(x):
    return 1.0 / (1.0 + jnp.exp(-x))


def _sigmoid_tanh(x):
    return 0.5 * jnp.tanh(0.5 * x) + 0.5


def _nt(a, b):
    return lax.dot_general(a, b, (((1,), (1,)), ((), ())), preferred_element_type=f32)


def _tn(a, b):
    return lax.dot_general(a, b, (((0,), (0,)), ((), ())), preferred_element_type=f32)


def _params(*sem, flags=None):
    return pltpu.CompilerParams(dimension_semantics=sem, vmem_limit_bytes=V7X_VMEM_LIMIT_BYTES, flags=flags)


def _const_spec(shape):
    nd = len(shape)
    return pl.BlockSpec(shape, lambda *_: (0,) * nd, pipeline_mode=pl.Buffered(1))


def _proj_kernel(x_ref, g_ref, b_ref, wn_ref, wt_ref,
                 lx_ref, ly_ref, kc_ref, vc_ref, ks_ref, kw_ref,
                 qT_ref, vsT_ref, vwT_ref, mqT_ref, gT_ref, h0_ref):
    h0 = _ln(x_ref[...], g_ref[...], b_ref[...])
    h0_ref[...] = h0
    h = h0.astype(bf16)

    def nat(c0, c1):
        return jnp.dot(h, wn_ref[:, c0:c1], preferred_element_type=f32)

    lx_ref[...] = nat(_NX0, _NY0)
    ly_ref[...] = nat(_NY0, _NKC0)
    kvc = nat(_NKC0, _NKS0)
    kc_ref[...] = kvc[:, 0:128]
    vc_ref[...] = kvc[:, 128:256]
    ksw = nat(_NKS0, _NCOLS)
    ks_ref[...] = ksw[:, 0:128].astype(bf16)
    kw_ref[...] = ksw[:, 128:256].astype(bf16)

    def tr(r0, r1):
        return _nt(wt_ref[r0:r1, :], h)

    qT_ref[...] = (tr(_TQ0, _TVS0) * NSA_Q_SCALE).astype(bf16)
    vsT_ref[...] = tr(_TVS0, _TVW0).astype(bf16)
    vwT_ref[...] = tr(_TVW0, _TMQ0).astype(bf16)
    mqT_ref[...] = tr(_TMQ0, _TG0).astype(bf16)
    gT_ref[...] = tr(_TG0, _TROWS)


def _proj(x2, g, b, wn, wt, tm):
    T = x2.shape[0]
    nat = lambda w: pl.BlockSpec((tm, w), lambda i: (i, 0))
    trs = lambda r: pl.BlockSpec((r, tm), lambda i: (0, i))
    return pl.pallas_call(
        _proj_kernel,
        grid=(T // tm,),
        in_specs=[pl.BlockSpec((tm, D_MODEL), lambda i: (i, 0)),
                  _const_spec((1, D_MODEL)), _const_spec((1, D_MODEL)),
                  _const_spec((D_MODEL, _NCOLS)), _const_spec((_TROWS, D_MODEL))],
        out_specs=[nat(1024), nat(1024), nat(128), nat(128), nat(128), nat(128),
                   trs(512), trs(128), trs(128), trs(512), trs(32), nat(D_MODEL)],
        out_shape=[jax.ShapeDtypeStruct((T, 1024), f32), jax.ShapeDtypeStruct((T, 1024), f32),
                   jax.ShapeDtypeStruct((T, 128), f32), jax.ShapeDtypeStruct((T, 128), f32),
                   jax.ShapeDtypeStruct((T, 128), bf16), jax.ShapeDtypeStruct((T, 128), bf16),
                   jax.ShapeDtypeStruct((512, T), bf16), jax.ShapeDtypeStruct((128, T), bf16),
                   jax.ShapeDtypeStruct((128, T), bf16), jax.ShapeDtypeStruct((512, T), bf16),
                   jax.ShapeDtypeStruct((32, T), f32), jax.ShapeDtypeStruct((T, D_MODEL), f32)],
        compiler_params=_params("parallel"),
        name="proj",
    )(x2, g, b, wn, wt)


def _lru_pitch(tb):
    seg = tb // 8
    return seg + 8 if (seg // 8) % 2 == 0 else seg


def _lru_kernel(x_ref, y_ref, cw_ref, cb_ref, wax_ref, ba_ref, bx_ref, lam_ref, o_ref,
                xext_ref, a_ref, u_ref, carry_ref, *, tb):
    s = pl.program_id(1)

    @pl.when(s == 0)
    def _():
        xext_ref[0:8, :] = jnp.zeros((8, LRU_WIDTH), f32)
        carry_ref[...] = jnp.zeros((1, LRU_WIDTH), f32)

    x = x_ref[0]
    xext_ref[8:8 + tb, :] = x
    xc = cb_ref[...]
    for k in range(LRU_CONV_WIDTH):
        off = 8 - (LRU_CONV_WIDTH - 1) + k
        xc = xc + xext_ref[off:off + tb, :] * cw_ref[k:k + 1, :]
    xext_ref[0:8, :] = x[tb - 8:tb, :]

    xb = xc.astype(bf16)
    gates = [jnp.dot(xb[:, n * 128:(n + 1) * 128], wax_ref[n], preferred_element_type=f32)
             for n in range(LRU_BLOCKS)]
    r = _sigmoid_tanh(jnp.concatenate([gt[:, 0:128] for gt in gates], axis=1) + ba_ref[...])
    i = _sigmoid_tanh(jnp.concatenate([gt[:, 128:256] for gt in gates], axis=1) + bx_ref[...])
    nl = -lam_ref[...]
    softplus = jnp.maximum(nl, 0.0) + jnp.log1p(jnp.exp(-jnp.abs(nl)))
    log_a = (-LRU_C) * r * softplus
    a = jnp.exp(log_a)
    th = jnp.tanh(log_a)
    z = -2.0 * th / (1.0 - th)
    u = (z * lax.rsqrt(jnp.maximum(z, LRU_TINY))) * (i * xc)
    seg = tb // 8
    pitch = _lru_pitch(tb)
    for n in range(LRU_BLOCKS):
        for r8 in range(8):
            a_ref[n, r8 * pitch:r8 * pitch + seg, :] = a[r8 * seg:(r8 + 1) * seg, n * 128:(n + 1) * 128]
            u_ref[n, r8 * pitch:r8 * pitch + seg, :] = u[r8 * seg:(r8 + 1) * seg, n * 128:(n + 1) * 128]

    def sweep(k, carry):
        rows = pl.ds(k, 8, stride=pitch)
        out = []
        for n in range(LRU_BLOCKS):
            h, p = carry[n]
            ak = a_ref[n, rows, :]
            h = ak * h + u_ref[n, rows, :]
            p = ak * p
            u_ref[n, rows, :] = h
            a_ref[n, rows, :] = p
            out.append((h, p))
        return tuple(out)

    init = tuple((jnp.zeros((8, 128), f32), jnp.ones((8, 128), f32)) for _ in range(LRU_BLOCKS))
    ends = lax.fori_loop(0, seg, sweep, init, unroll=4)
    h_end = jnp.concatenate([e[0] for e in ends], axis=1)
    p_end = jnp.concatenate([e[1] for e in ends], axis=1)

    c = carry_ref[...]
    for r8 in range(8):
        rows = slice(r8 * seg, (r8 + 1) * seg)
        prow = slice(r8 * pitch, r8 * pitch + seg)
        h_loc = jnp.concatenate([u_ref[n, prow, :] for n in range(LRU_BLOCKS)], axis=1)
        p_cum = jnp.concatenate([a_ref[n, prow, :] for n in range(LRU_BLOCKS)], axis=1)
        o_ref[0, rows, :] = (_gelu(y_ref[0, rows, :]) * (h_loc + p_cum * c)).astype(bf16)
        c = h_end[r8:r8 + 1, :] + p_end[r8:r8 + 1, :] * c
    carry_ref[...] = c


def _lru(lx, ly, cw, cb, wax, ba, bx, lam, tb):
    B, S, _ = lx.shape
    blk = pl.BlockSpec((1, tb, LRU_WIDTH), lambda b, s: (b, s, 0))
    return pl.pallas_call(
        functools.partial(_lru_kernel, tb=tb),
        grid=(B, S // tb),
        in_specs=[blk, blk, _const_spec((LRU_CONV_WIDTH, LRU_WIDTH)), _const_spec((1, LRU_WIDTH)),
                  _const_spec((LRU_BLOCKS, 128, 256)), _const_spec((1, LRU_WIDTH)),
                  _const_spec((1, LRU_WIDTH)), _const_spec((1, LRU_WIDTH))],
        out_specs=blk,
        out_shape=jax.ShapeDtypeStruct((B, S, LRU_WIDTH), bf16),
        scratch_shapes=[pltpu.VMEM((tb + 8, LRU_WIDTH), f32),
                        pltpu.VMEM((LRU_BLOCKS, 8 * _lru_pitch(tb), 128), f32),
                        pltpu.VMEM((LRU_BLOCKS, 8 * _lru_pitch(tb), 128), f32),
                        pltpu.VMEM((1, LRU_WIDTH), f32)],
        compiler_params=_params("parallel", "arbitrary"),
        name="lru",
    )(lx, ly, cw, cb, wax, ba, bx, lam)


def _cmp_kernel(kc_ref, vc_ref, pek_ref, w1k_ref, w2k_ref, pev_ref, w1v_ref, w2vT_ref,
                kcmp_ref, vcmpT_ref):
    nsub = kc_ref.shape[1]

    def hidden(sub_ref, pe_ref, w1_ref):
        sub = sub_ref[0]
        y0 = jnp.dot((sub + pe_ref[0:1, :]).astype(bf16), w1_ref[0], preferred_element_type=f32)
        y1 = jnp.dot((sub + pe_ref[1:2, :]).astype(bf16), w1_ref[1], preferred_element_type=f32)
        return _gelu(y0 + pltpu.roll(y1, nsub - 1, 0)).astype(bf16)

    kcmp_ref[0] = jnp.dot(hidden(kc_ref, pek_ref, w1k_ref), w2k_ref[...],
                          preferred_element_type=f32).astype(bf16)
    vcmpT_ref[0] = _nt(w2vT_ref[...], hidden(vc_ref, pev_ref, w1v_ref)).astype(bf16)


def _compress(kc, vc, pek, w1k, w2k, pev, w1v, w2vT):
    B, nsub, width = kc.shape
    blk = pl.BlockSpec((1, nsub, width), lambda b: (b, 0, 0))
    return pl.pallas_call(
        _cmp_kernel,
        grid=(B,),
        in_specs=[blk, blk, _const_spec((2, width)), _const_spec((2, width, 128)), _const_spec((128, 128)),
                  _const_spec((2, width)), _const_spec((2, width, 128)), _const_spec((128, 128))],
        out_specs=[pl.BlockSpec((1, nsub, 128), lambda b: (b, 0, 0)),
                   pl.BlockSpec((1, 128, nsub), lambda b: (b, 0, 0))],
        out_shape=[jax.ShapeDtypeStruct((B, nsub, 128), bf16), jax.ShapeDtypeStruct((B, 128, nsub), bf16)],
        compiler_params=_params("parallel"),
        name="nsa_compress",
    )(kc, vc, pek, w1k, w2k, pev, w1v, w2vT)


def _nsa_kernel(qT_ref, gT_ref, kcmp_ref, vcmpT_ref, ks_ref, vsT_ref, kw_ref, vwT_ref, covT_ref, eblk_ref,
                o_ref, qaug_ref, s_ref, acc_ref, m_ref, smax_ref, *, n_slc, n_top):
    g = pl.program_id(1)
    i = pl.program_id(2)
    q0 = i * Q_BLOCK
    grow = pl.multiple_of(g * NSA_HEAD_DIM, NSA_HEAD_DIM)
    HQ = NSA_HPG * Q_BLOCK

    def per_head(m):
        return jnp.concatenate([m] * NSA_HPG, axis=1)

    qh = jnp.concatenate([qT_ref[hp * 64:(hp + 1) * 64, :] for hp in range(NSA_HPG)], axis=1)
    zq = jnp.zeros_like(qh)
    qpad = jnp.concatenate([jnp.where(g == 0, qh, zq), jnp.where(g == 1, qh, zq)], axis=0)
    t_row = q0 + lax.broadcasted_iota(jnp.int32, (1, Q_BLOCK), 1)

    ncmp = kcmp_ref.shape[1]

    def compressed(rows):
        n_iota = lax.broadcasted_iota(jnp.int32, (rows, Q_BLOCK), 0)
        bias_c = jnp.where((n_iota * CMP_STRIDE + (CMP_BLOCK - 1)) <= t_row, 0.0, NEG_INF)
        sc = jnp.dot(kcmp_ref[0, 0:rows, :], qpad, preferred_element_type=f32) + per_head(bias_c)
        pc = jnp.exp2(sc - jnp.max(sc, axis=0, keepdims=True))
        lc = jnp.sum(pc, axis=0, keepdims=True)
        pc = pc * jnp.where(per_head(t_row) >= CMP_BLOCK - 1, 1.0 / lc, 0.0)
        o = jnp.dot(vcmpT_ref[0, pl.ds(grow, 64), 0:rows], pc.astype(bf16), preferred_element_type=f32)
        psum = pc[:, 0:Q_BLOCK]
        for hp in range(1, NSA_HPG):
            psum = psum + pc[:, hp * Q_BLOCK:(hp + 1) * Q_BLOCK]
        hi = psum.astype(bf16)
        r1 = psum - hi.astype(f32)
        mid = r1.astype(bf16)
        lo = (r1 - mid.astype(f32)).astype(bf16)
        cov = covT_ref[:, 0:rows]
        return o, (jnp.dot(cov, hi, preferred_element_type=f32) + jnp.dot(cov, mid, preferred_element_type=f32)
                   + jnp.dot(cov, lo, preferred_element_type=f32))

    row_options = [r for r in (ncmp // 4, ncmp // 2, 3 * ncmp // 4, ncmp) if r % 128 == 0]
    need = (Q_BLOCK // CMP_STRIDE) * (i + 1)
    which = sum((need > r).astype(jnp.int32) for r in row_options[:-1])
    oc, imp = lax.switch(which, [functools.partial(compressed, r) for r in row_options])

    j_i = lax.broadcasted_iota(jnp.int32, (n_slc, Q_BLOCK), 0)
    j_f = j_i.astype(f32)
    cur = t_row >> 6
    forced = (j_i == 0) | (j_i == cur) | (j_i == cur - 1)
    visible = (j_i * SLC_BLOCK) <= t_row
    score = jnp.where(forced, TAKEN_SCORE, jnp.where(visible, imp, NEG_INF))

    w0 = pl.multiple_of(jnp.maximum(q0 - WINDOW, 0), Q_BLOCK)
    kp = w0 + lax.broadcasted_iota(jnp.int32, (WIN_KEYS, Q_BLOCK), 0)
    bias_w = jnp.where((kp <= t_row) & (kp > (t_row - WINDOW)), 0.0, NEG_INF)
    sw = jnp.dot(kw_ref[0, pl.ds(w0, WIN_KEYS), :], qpad, preferred_element_type=f32) + per_head(bias_w)
    pw = jnp.exp2(sw - jnp.max(sw, axis=0, keepdims=True)).astype(bf16)
    vw1 = jnp.concatenate([vwT_ref[pl.ds(grow, 64), pl.ds(w0, WIN_KEYS)],
                           jnp.ones((ONES_ROWS, WIN_KEYS), bf16)], axis=0)
    ow = jnp.dot(vw1, pw, preferred_element_type=f32)
    ow = ow[0:NSA_HEAD_DIM, :] * (1.0 / ow[NSA_HEAD_DIM:NSA_HEAD_DIM + 1, :])

    sel = jnp.where(forced, 1.0, 0.0)
    for _ in range(n_top - 3):
        mx = jnp.max(score, axis=0, keepdims=True)
        first = jnp.min(jnp.where(score == mx, j_f, float(n_slc)), axis=0, keepdims=True)
        hit = j_f == first
        sel = jnp.where(hit, 1.0, sel)
        score = jnp.where(hit, TAKEN_SCORE, score)

    qaug_ref[0:2 * NSA_HEAD_DIM, :] = qpad
    blk_bias = jnp.where((sel > 0.0) & visible, 0.0, NEG_INF)
    qaug_ref[2 * NSA_HEAD_DIM:2 * NSA_HEAD_DIM + n_slc, :] = per_head(blk_bias).astype(bf16)
    if n_slc < 128:
        qaug_ref[2 * NSA_HEAD_DIM + n_slc:, :] = jnp.zeros((128 - n_slc, HQ), bf16)

    def scores(c):
        k0 = pl.multiple_of(c * SEL_CHUNK, SEL_CHUNK)
        kaug = jnp.concatenate([ks_ref[0, pl.ds(k0, SEL_CHUNK), :], eblk_ref[pl.ds(k0, SEL_CHUNK), :]], axis=1)
        return jnp.dot(kaug, qaug_ref[...], preferred_element_type=f32)

    def attend(c, s, smax):
        k0 = pl.multiple_of(c * SEL_CHUNK, SEL_CHUNK)
        m = m_ref[...]
        m_new = jnp.maximum(m, smax)
        m_ref[...] = m_new
        p = jnp.exp2(s - m_new).astype(bf16)
        v1 = jnp.concatenate([vsT_ref[pl.ds(grow, 64), pl.ds(k0, SEL_CHUNK)],
                              jnp.ones((ONES_ROWS, SEL_CHUNK), bf16)], axis=0)
        acc_ref[...] = jnp.exp2(m - m_new) * acc_ref[...] + jnp.dot(v1, p, preferred_element_type=f32)

    def step(c, cur, nxt):
        smax = smax_ref[...]
        s_next = scores(c + 1)
        s_ref[nxt] = s_next
        smax_ref[...] = jnp.max(s_next, axis=0, keepdims=True)
        attend(c, s_ref[cur], smax)

    last = q0 // SEL_CHUNK
    odd = last % 2
    s0 = scores(0)
    s_ref[odd] = s0
    smax_ref[...] = jnp.max(s0, axis=0, keepdims=True)
    acc_ref[...] = jnp.zeros((NSA_HEAD_DIM + ONES_ROWS, HQ), f32)
    m_ref[...] = jnp.full((1, HQ), NEG_INF, f32)

    @pl.when(odd == 1)
    def _():
        step(0, 1, 0)

    has_pair = (last // 2) % 2

    @pl.when(has_pair == 1)
    def _():
        step(odd, 0, 1)
        step(odd + 1, 1, 0)

    def quad(qq, carry):
        c = odd + 2 * has_pair + 4 * qq
        step(c, 0, 1)
        step(c + 1, 1, 0)
        step(c + 2, 0, 1)
        step(c + 3, 1, 0)
        return carry

    lax.fori_loop(0, last // 4, quad, 0)

    kp_d = last * SEL_CHUNK + lax.broadcasted_iota(jnp.int32, (SEL_CHUNK, Q_BLOCK), 0)
    s_d = s_ref[0] + per_head(jnp.where(kp_d <= t_row, 0.0, NEG_INF))
    attend(last, s_d, jnp.max(s_d, axis=0, keepdims=True))
    osel = acc_ref[0:NSA_HEAD_DIM, :] * (1.0 / acc_ref[NSA_HEAD_DIM:NSA_HEAD_DIM + 1, :])

    gate = _sigmoid(gT_ref[...])
    for hp in range(NSA_HPG):
        cols = slice(hp * Q_BLOCK, (hp + 1) * Q_BLOCK)
        o = (gate[hp:hp + 1, :] * oc[:, cols] + gate[4 + hp:5 + hp, :] * osel[:, cols]
             + gate[8 + hp:9 + hp, :] * ow[:, cols])
        o_ref[hp * 64:(hp + 1) * 64, :] = o.astype(bf16)


def _nsa(qT, gT, kcmp, vcmpT, ks, vsT, kw, vwT, covT, eblk, B, S):
    n_q = S // Q_BLOCK
    n_slc = S // SLC_BLOCK
    ncmp = S // CMP_STRIDE
    tok = lambda r: pl.BlockSpec((r, Q_BLOCK), lambda b, g, i: (g, b * n_q + i))
    per_b_nat = pl.BlockSpec((1, S, 128), lambda b, g, i: (b, 0, 0))
    per_b_tr = pl.BlockSpec((128, S), lambda b, g, i: (0, b))
    return pl.pallas_call(
        functools.partial(_nsa_kernel, n_slc=n_slc, n_top=min(N_SELECT, n_slc)),
        grid=(B, NSA_KV_HEADS, n_q),
        in_specs=[tok(256), tok(16),
                  pl.BlockSpec((1, ncmp, 128), lambda b, g, i: (b, 0, 0)),
                  pl.BlockSpec((1, 128, ncmp), lambda b, g, i: (b, 0, 0)),
                  per_b_nat, per_b_tr, per_b_nat, per_b_tr,
                  _const_spec((n_slc, ncmp)), _const_spec((S, 128))],
        out_specs=tok(256),
        out_shape=jax.ShapeDtypeStruct((512, B * S), bf16),
        scratch_shapes=[pltpu.VMEM((2 * NSA_HEAD_DIM + 128, NSA_HPG * Q_BLOCK), bf16),
                        pltpu.VMEM((2, SEL_CHUNK, NSA_HPG * Q_BLOCK), f32),
                        pltpu.VMEM((NSA_HEAD_DIM + ONES_ROWS, NSA_HPG * Q_BLOCK), f32)]
        + [pltpu.VMEM((1, NSA_HPG * Q_BLOCK), f32)] * 2,
        compiler_params=_params("parallel", "parallel", "arbitrary"),
        name="nsa_attn",
    )(qT, gT, kcmp, vcmpT, ks, vsT, kw, vwT, covT, eblk)


def _memkv_kernel(mem_ref, wk_ref, wvT_ref, k_ref, vT_ref):
    m = mem_ref[0].astype(bf16)
    k_ref[0] = jnp.dot(m, wk_ref[...], preferred_element_type=f32).astype(bf16)
    vT_ref[0] = _nt(wvT_ref[...], m).astype(bf16)


def _memkv(mem, wk, wvT):
    B, M, _ = mem.shape
    return pl.pallas_call(
        _memkv_kernel,
        grid=(B,),
        in_specs=[pl.BlockSpec((1, M, D_MODEL), lambda b: (b, 0, 0)),
                  _const_spec((D_MODEL, 512)), _const_spec((512, D_MODEL))],
        out_specs=[pl.BlockSpec((1, M, 512), lambda b: (b, 0, 0)),
                   pl.BlockSpec((1, 512, M), lambda b: (b, 0, 0))],
        out_shape=[jax.ShapeDtypeStruct((B, M, 512), bf16), jax.ShapeDtypeStruct((B, 512, M), bf16)],
        compiler_params=_params("parallel"),
        name="mem_kv",
    )(mem, wk, wvT)


def _memattn_kernel(qT_ref, k_ref, vT_ref, o_ref):
    for h in range(MEM_HEADS):
        rows = slice(h * MEM_HEAD_DIM, (h + 1) * MEM_HEAD_DIM)
        s = jnp.dot(k_ref[0, :, rows], qT_ref[rows, :], preferred_element_type=f32) * (MEM_HEAD_DIM ** -0.5)
        p = jnp.exp(s - jnp.max(s, axis=0, keepdims=True))
        l = jnp.sum(p, axis=0, keepdims=True)
        o = jnp.dot(vT_ref[0, rows, :], p.astype(bf16), preferred_element_type=f32) * (1.0 / l)
        o_ref[rows, :] = o.astype(bf16)


def _memattn(mqT, mk, mvT, B, S, tq):
    M = mk.shape[1]
    nq = S // tq
    return pl.pallas_call(
        _memattn_kernel,
        grid=(B, nq),
        in_specs=[pl.BlockSpec((512, tq), lambda b, i: (0, b * nq + i)),
                  pl.BlockSpec((1, M, 512), lambda b, i: (b, 0, 0)),
                  pl.BlockSpec((1, 512, M), lambda b, i: (b, 0, 0))],
        out_specs=pl.BlockSpec((512, tq), lambda b, i: (0, b * nq + i)),
        out_shape=jax.ShapeDtypeStruct((512, B * S), bf16),
        compiler_params=_params("parallel", "parallel"),
        name="mem_attn",
    )(mqT, mk, mvT)


def _outproj_kernel(h0_ref, ol_ref, onT_ref, omT_ref, wo_ref, g1_ref, b1_ref, h1_ref, *, tm):
    for r0 in range(0, tm, OUT_SUB_ROWS):
        rows = slice(r0, r0 + OUT_SUB_ROWS)
        mixed = jnp.dot(ol_ref[rows, :], wo_ref[0:1024, :], preferred_element_type=f32)
        mixed = mixed + _tn(onT_ref[:, rows], wo_ref[1024:1536, :])
        mixed = mixed + _tn(omT_ref[:, rows], wo_ref[1536:2048, :])
        h1_ref[rows, :] = _ln(ALPHA * h0_ref[rows, :] + mixed, g1_ref[...], b1_ref[...])


def _outproj(h0, ol, onT, omT, wo, g1, b1, tm):
    T = h0.shape[0]
    row = lambda w: pl.BlockSpec((tm, w), lambda i: (i, 0))
    trs = pl.BlockSpec((512, tm), lambda i: (0, i))
    vec = _const_spec((1, D_MODEL))
    return pl.pallas_call(
        functools.partial(_outproj_kernel, tm=tm),
        grid=(T // tm,),
        in_specs=[row(D_MODEL), row(1024), trs, trs, _const_spec((D_MODEL, D_MODEL)), vec, vec],
        out_specs=row(D_MODEL),
        out_shape=jax.ShapeDtypeStruct((T, D_MODEL), f32),
        compiler_params=_params("parallel"),
        name="out_proj_ln1",
    )(h0, ol, onT, omT, wo, g1, b1)


def _ffn_kernel(h_ref, halo_ref, wg_ref, wu_ref, cwg_ref, cwu_ref, cbg_ref, cbu_ref, wd_ref,
                g2_ref, b2_ref, o_ref, lhs_ref, *, tm, blocks_per_seq):
    i = pl.program_id(0)
    j = pl.program_id(1)

    @pl.when(j == 0)
    def _():
        first = (i % blocks_per_seq) == 0
        lhs_ref[0:8, :] = jnp.where(first, 0.0, halo_ref[...]).astype(bf16)
        h = h_ref[...]
        lhs_ref[8:8 + tm, :] = h.astype(bf16)
        o_ref[...] = ALPHA * h

    lhs = lhs_ref[...]

    def conv(w_ref, cw_ref, cb_ref):
        up = jnp.dot(lhs, w_ref[...], preferred_element_type=f32)
        y = cb_ref[...] + pltpu.roll(up, 2, 0)[8:8 + tm, :] * cw_ref[0:1, :]
        y = y + pltpu.roll(up, 1, 0)[8:8 + tm, :] * cw_ref[1:2, :]
        return y + up[8:8 + tm, :] * cw_ref[2:3, :]

    act = (_gelu(conv(wg_ref, cwg_ref, cbg_ref)) * conv(wu_ref, cwu_ref, cbu_ref)).astype(bf16)
    o_ref[...] += jnp.dot(act, wd_ref[...], preferred_element_type=f32)

    @pl.when(j == pl.num_programs(1) - 1)
    def _():
        o_ref[...] = _ln(o_ref[...], g2_ref[...], b2_ref[...])


def _ffn(h1, wup, cw, cb, wd, g2, b2, tm, tf, blocks_per_seq):
    T = h1.shape[0]
    nf = D_FF // tf
    vec = _const_spec((1, D_MODEL))
    return pl.pallas_call(
        functools.partial(_ffn_kernel, tm=tm, blocks_per_seq=blocks_per_seq),
        grid=(T // tm, nf),
        in_specs=[pl.BlockSpec((tm, D_MODEL), lambda i, j: (i, 0), pipeline_mode=pl.Buffered(1)),
                  pl.BlockSpec((8, D_MODEL), lambda i, j: (jnp.maximum(i * (tm // 8) - 1, 0), 0)),
                  pl.BlockSpec((D_MODEL, tf), lambda i, j: (0, j)),
                  pl.BlockSpec((D_MODEL, tf), lambda i, j: (0, nf + j)),
                  pl.BlockSpec((FFN_CONV_WIDTH, tf), lambda i, j: (0, j)),
                  pl.BlockSpec((FFN_CONV_WIDTH, tf), lambda i, j: (0, nf + j)),
                  pl.BlockSpec((1, tf), lambda i, j: (0, j)),
                  pl.BlockSpec((1, tf), lambda i, j: (0, nf + j)),
                  pl.BlockSpec((tf, D_MODEL), lambda i, j: (j, 0)),
                  vec, vec],
        out_specs=pl.BlockSpec((tm, D_MODEL), lambda i, j: (i, 0)),
        out_shape=jax.ShapeDtypeStruct((T, D_MODEL), f32),
        scratch_shapes=[pltpu.VMEM((tm + 8, D_MODEL), bf16)],
        compiler_params=_params("parallel", "arbitrary"),
        name="conv_ffn_ln2",
    )(h1, h1, wup, wup, cw, cw, cb, cb, wd, g2, b2)


def _layer(x, mem, ln_in_g, ln_in_b, w_in, lru_conv_w, lru_conv_b, lru_wa, lru_ba, lru_wx, lru_bx,
           lru_lam, cmp_pe_k, cmp_w1_k, cmp_w2_k, cmp_pe_v, cmp_w1_v, cmp_w2_v, w_mem_kv, w_out,
           ln1_g, ln1_b, ffn_w_up, ffn_conv_w, ffn_conv_b, ffn_w_down, ln2_g, ln2_b,
           *, tm_proj, tb_lru, tq_mem, tm_out, tm_ffn, tf_ffn):
    B, S, _ = x.shape
    T = B * S
    G, Dh = NSA_KV_HEADS, NSA_HEAD_DIM
    row = lambda v: v.reshape(1, -1)

    c_q = 2 * LRU_WIDTH
    c_kv = c_q + NSA_HEADS * Dh
    c_gate = c_kv + 6 * G * Dh
    c_mq = c_gate + 3 * NSA_HEADS
    kv_cols = [w_in[:, c_kv + n * G * Dh: c_kv + (n + 1) * G * Dh] for n in range(6)]
    wn = jnp.concatenate([w_in[:, :c_q], kv_cols[0], kv_cols[1], kv_cols[2], kv_cols[4]], axis=1).astype(bf16)
    w_gate = w_in[:, c_gate:c_mq].reshape(D_MODEL, G, NSA_HPG, 3)
    w_gate = jnp.pad(w_gate.transpose(1, 3, 2, 0), ((0, 0), (0, 1), (0, 0), (0, 0)))
    wt = jnp.concatenate([w_in[:, c_q:c_kv].T, kv_cols[3].T, kv_cols[5].T, w_in[:, c_mq:].T,
                          w_gate.reshape(2 * 16, D_MODEL)], axis=0).astype(bf16)

    wax = jnp.concatenate([lru_wa, lru_wx], axis=-1).astype(bf16)

    def cmp_weights(pe, w1, w2):
        pe2 = jnp.broadcast_to(pe.reshape(2, CMP_STRIDE, 1, Dh), (2, CMP_STRIDE, G, Dh)).reshape(2, -1)
        eye = jnp.eye(G, dtype=f32)
        w1e = jnp.einsum('hlde,gk->hlgdke', w1.reshape(2, CMP_STRIDE, Dh, Dh), eye)
        w1e = w1e.reshape(2, CMP_STRIDE * G * Dh, G * Dh).astype(bf16)
        w2e = jnp.einsum('ef,gk->gekf', w2, eye).reshape(G * Dh, G * Dh)
        return pe2, w1e, w2e

    pek, w1k, w2k = cmp_weights(cmp_pe_k, cmp_w1_k, cmp_w2_k)
    pev, w1v, w2v = cmp_weights(cmp_pe_v, cmp_w1_v, cmp_w2_v)

    n_slc, ncmp = S // SLC_BLOCK, S // CMP_STRIDE
    ci = jnp.arange(ncmp)[None, :] * CMP_STRIDE
    sj = jnp.arange(n_slc)[:, None] * SLC_BLOCK
    covT = ((ci <= sj + SLC_BLOCK - 1) & (ci + CMP_BLOCK - 1 >= sj)).astype(bf16)
    eblk = (jnp.arange(S)[:, None] // SLC_BLOCK == jnp.arange(128)[None, :]).astype(bf16)

    wk_mem = w_mem_kv[:, :512].astype(bf16)
    wvT_mem = w_mem_kv[:, 512:].T.astype(bf16)
    wo = w_out.astype(bf16)
    wup = ffn_w_up.astype(bf16)
    wd = ffn_w_down.astype(bf16)

    x2 = x.reshape(T, D_MODEL)
    lx, ly, kc, vc, ks, kw, qT, vsT, vwT, mqT, gT, h0 = _proj(x2, row(ln_in_g), row(ln_in_b), wn, wt, tm_proj)

    o_lru = _lru(lx.reshape(B, S, LRU_WIDTH), ly.reshape(B, S, LRU_WIDTH), lru_conv_w, row(lru_conv_b),
                 wax, row(lru_ba), row(lru_bx), row(lru_lam), tb_lru)

    sub = lambda a: a.reshape(B, ncmp, CMP_STRIDE * G * Dh)
    kcmp, vcmpT = _compress(sub(kc), sub(vc), pek, w1k, w2k.astype(bf16), pev, w1v, w2v.T.astype(bf16))
    o_nsaT = _nsa(qT, gT, kcmp, vcmpT, ks.reshape(B, S, 128), vsT, kw.reshape(B, S, 128), vwT, covT, eblk, B, S)

    mk, mvT = _memkv(mem, wk_mem, wvT_mem)
    o_memT = _memattn(mqT, mk, mvT, B, S, tq_mem)

    h1 = _outproj(h0, o_lru.reshape(T, LRU_WIDTH), o_nsaT, o_memT, wo, row(ln1_g), row(ln1_b), tm_out)
    out = _ffn(h1, wup, ffn_conv_w, row(ffn_conv_b), wd, row(ln2_g), row(ln2_b), tm_ffn, tf_ffn, S // tm_ffn)
    return out.reshape(B, S, D_MODEL)


def kernel(x, mem, ln_in_g, ln_in_b, w_in, lru_conv_w, lru_conv_b, lru_wa, lru_ba, lru_wx, lru_bx,
           lru_lam, cmp_pe_k, cmp_w1_k, cmp_w2_k, cmp_pe_v, cmp_w1_v, cmp_w2_v, w_mem_kv, w_out,
           ln1_g, ln1_b, ffn_w_up, ffn_conv_w, ffn_conv_b, ffn_w_down, ln2_g, ln2_b):
    return _layer(x, mem, ln_in_g, ln_in_b, w_in[0], lru_conv_w[0], lru_conv_b[0], lru_wa[0], lru_ba[0],
                  lru_wx[0], lru_bx[0], lru_lam[0], cmp_pe_k[0], cmp_w1_k[0], cmp_w2_k[0], cmp_pe_v[0],
                  cmp_w1_v[0], cmp_w2_v[0], w_mem_kv[0], w_out[0], ln1_g[0], ln1_b[0], ffn_w_up[0],
                  ffn_conv_w[0], ffn_conv_b[0], ffn_w_down[0], ln2_g[0], ln2_b[0],
                  tm_proj=512, tb_lru=256, tq_mem=512, tm_out=1024, tm_ffn=1024, tf_ffn=512)
```

```python
import functools
import math

import jax
import jax.numpy as jnp
from jax import lax
from jax.experimental import pallas as pl
from jax.experimental.pallas import tpu as pltpu

f32 = jnp.float32
bf16 = jnp.bfloat16

D_MODEL = 2048
LRU_WIDTH = 1024
LRU_BLOCKS = 8
LRU_BLOCK_DIM = 128
LRU_CONV_WIDTH = 4
LRU_C = 8.0
LRU_TINY = 1e-30
NSA_HEADS = 8
NSA_KV_HEADS = 2
NSA_HPG = NSA_HEADS // NSA_KV_HEADS
NSA_HEAD_DIM = 64
NSA_Q_SCALE = NSA_HEAD_DIM ** -0.5 * math.log2(math.e)
CMP_STRIDE = 16
CMP_BLOCK = 32
SLC_BLOCK = 64
N_SELECT = 16
WINDOW = 512
Q_BLOCK = 128
MEM_HEADS = 4
MEM_HEAD_DIM = 128
D_FF = 5632
FFN_CONV_WIDTH = 3
LN_EPS = 1e-5
NEG_INF = -1e30
FORCE_SCORE = 1e9
TAKEN_SCORE = -3e38
ALPHA = 2.0 ** 0.25

V7X_VMEM_LIMIT_BYTES = 56 * 1024 * 1024

_TQ0, _TVS0, _TVW0, _TMQ0, _TG0, _TROWS = 0, 512, 640, 768, 1280, 1312
_NX0, _NY0, _NKC0, _NVC0, _NKS0, _NKW0, _NCOLS = 0, 1024, 2048, 2176, 2304, 2432, 2560

SEL_CHUNK = 512
WIN_KEYS = WINDOW + Q_BLOCK
GATE_ROWS = 16
OUT_SUB_ROWS = 512
ONES_ROWS = 16


def _ln(x, g, b):
    mu = jnp.mean(x, axis=-1, keepdims=True)
    xc = x - mu
    var = jnp.mean(xc * xc, axis=-1, keepdims=True)
    return xc * lax.rsqrt(var + LN_EPS) * g + b


def _gelu(x):
    return jax.nn.gelu(x)


def _sigmoid(x):
    return 1.0 / (1.0 + jnp.exp(-x))


def _sigmoid_tanh(x):
    return 0.5 * jnp.tanh(0.5 * x) + 0.5


def _nt(a, b):
    return lax.dot_general(a, b, (((1,), (1,)), ((), ())), preferred_element_type=f32)


def _tn(a, b):
    return lax.dot_general(a, b, (((0,), (0,)), ((), ())), preferred_element_type=f32)


def _params(*sem, flags=None):
    return pltpu.CompilerParams(dimension_semantics=sem, vmem_limit_bytes=V7X_VMEM_LIMIT_BYTES, flags=flags)


def _const_spec(shape):
    nd = len(shape)
    return pl.BlockSpec(shape, lambda *_: (0,) * nd, pipeline_mode=pl.Buffered(1))


def _proj_kernel(x_ref, g_ref, b_ref, wn_ref, wt_ref,
                 lx_ref, ly_ref, kc_ref, vc_ref, ks_ref, kw_ref,
                 qT_ref, vsT_ref, vwT_ref, mqT_ref, gT_ref, h0_ref):
    h0 = _ln(x_ref[...], g_ref[...], b_ref[...])
    h0_ref[...] = h0
    h = h0.astype(bf16)

    def nat(c0, c1):
        return jnp.dot(h, wn_ref[:, c0:c1], preferred_element_type=f32)

    lx_ref[...] = nat(_NX0, _NY0)
    ly_ref[...] = nat(_NY0, _NKC0)
    kvc = nat(_NKC0, _NKS0)
    kc_ref[...] = kvc[:, 0:128]
    vc_ref[...] = kvc[:, 128:256]
    ksw = nat(_NKS0, _NCOLS)
    ks_ref[...] = ksw[:, 0:128].astype(bf16)
    kw_ref[...] = ksw[:, 128:256].astype(bf16)

    def tr(r0, r1):
        return _nt(wt_ref[r0:r1, :], h)

    qT_ref[...] = (tr(_TQ0, _TVS0) * NSA_Q_SCALE).astype(bf16)
    vsT_ref[...] = tr(_TVS0, _TVW0).astype(bf16)
    vwT_ref[...] = tr(_TVW0, _TMQ0).astype(bf16)
    mqT_ref[...] = tr(_TMQ0, _TG0).astype(bf16)
    gT_ref[...] = tr(_TG0, _TROWS)


def _proj(x2, g, b, wn, wt, tm):
    T = x2.shape[0]
    nat = lambda w: pl.BlockSpec((tm, w), lambda i: (i, 0))
    trs = lambda r: pl.BlockSpec((r, tm), lambda i: (0, i))
    return pl.pallas_call(
        _proj_kernel,
        grid=(T // tm,),
        in_specs=[pl.BlockSpec((tm, D_MODEL), lambda i: (i, 0)),
                  _const_spec((1, D_MODEL)), _const_spec((1, D_MODEL)),
                  _const_spec((D_MODEL, _NCOLS)), _const_spec((_TROWS, D_MODEL))],
        out_specs=[nat(1024), nat(1024), nat(128), nat(128), nat(128), nat(128),
                   trs(512), trs(128), trs(128), trs(512), trs(32), nat(D_MODEL)],
        out_shape=[jax.ShapeDtypeStruct((T, 1024), f32), jax.ShapeDtypeStruct((T, 1024), f32),
                   jax.ShapeDtypeStruct((T, 128), f32), jax.ShapeDtypeStruct((T, 128), f32),
                   jax.ShapeDtypeStruct((T, 128), bf16), jax.ShapeDtypeStruct((T, 128), bf16),
                   jax.ShapeDtypeStruct((512, T), bf16), jax.ShapeDtypeStruct((128, T), bf16),
                   jax.ShapeDtypeStruct((128, T), bf16), jax.ShapeDtypeStruct((512, T), bf16),
                   jax.ShapeDtypeStruct((32, T), f32), jax.ShapeDtypeStruct((T, D_MODEL), f32)],
        compiler_params=_params("parallel"),
        name="proj",
    )(x2, g, b, wn, wt)


def _lru_pitch(tb):
    seg = tb // 8
    return seg + 8 if (seg // 8) % 2 == 0 else seg


def _lru_kernel(x_ref, y_ref, cw_ref, cb_ref, wax_ref, ba_ref, bx_ref, lam_ref, o_ref,
                xext_ref, a_ref, u_ref, carry_ref, *, tb):
    s = pl.program_id(1)

    @pl.when(s == 0)
    def _():
        xext_ref[0:8, :] = jnp.zeros((8, LRU_WIDTH), f32)
        carry_ref[...] = jnp.zeros((1, LRU_WIDTH), f32)

    x = x_ref[0]
    xext_ref[8:8 + tb, :] = x
    xe = xext_ref[...]
    xc = cb_ref[...]
    for k in range(LRU_CONV_WIDTH):
        back = LRU_CONV_WIDTH - 1 - k
        shifted = pltpu.roll(xe, back, 0)[8:8 + tb, :] if back else x
        xc = xc + shifted * cw_ref[k:k + 1, :]
    xext_ref[0:8, :] = x[tb - 8:tb, :]

    xb = xc.astype(bf16)
    gates = [jnp.dot(xb[:, n * 128:(n + 1) * 128], wax_ref[n], preferred_element_type=f32)
             for n in range(LRU_BLOCKS)]
    r = _sigmoid_tanh(jnp.concatenate([gt[:, 0:128] for gt in gates], axis=1) + ba_ref[...])
    i = _sigmoid_tanh(jnp.concatenate([gt[:, 128:256] for gt in gates], axis=1) + bx_ref[...])
    nl = -lam_ref[...]
    softplus = jnp.maximum(nl, 0.0) + jnp.log1p(jnp.exp(-jnp.abs(nl)))
    log_a = (-LRU_C) * r * softplus
    a = jnp.exp(log_a)
    th = jnp.tanh(log_a)
    z = -2.0 * th / (1.0 - th)
    u = (z * lax.rsqrt(jnp.maximum(z, LRU_TINY))) * (i * xc)
    seg = tb // 8
    pitch = _lru_pitch(tb)
    for n in range(LRU_BLOCKS):
        for r8 in range(8):
            a_ref[n, r8 * pitch:r8 * pitch + seg, :] = a[r8 * seg:(r8 + 1) * seg, n * 128:(n + 1) * 128]
            u_ref[n, r8 * pitch:r8 * pitch + seg, :] = u[r8 * seg:(r8 + 1) * seg, n * 128:(n + 1) * 128]

    def sweep(k, carry):
        rows = pl.ds(k, 8, stride=pitch)
        out = []
        for n in range(LRU_BLOCKS):
            h, p = carry[n]
            ak = a_ref[n, rows, :]
            h = ak * h + u_ref[n, rows, :]
            p = ak * p
            u_ref[n, rows, :] = h
            a_ref[n, rows, :] = p
            out.append((h, p))
        return tuple(out)

    init = tuple((jnp.zeros((8, 128), f32), jnp.ones((8, 128), f32)) for _ in range(LRU_BLOCKS))
    ends = lax.fori_loop(0, seg, sweep, init, unroll=4)
    h_end = jnp.concatenate([e[0] for e in ends], axis=1)
    p_end = jnp.concatenate([e[1] for e in ends], axis=1)

    c = carry_ref[...]
    for r8 in range(8):
        rows = slice(r8 * seg, (r8 + 1) * seg)
        prow = slice(r8 * pitch, r8 * pitch + seg)
        h_loc = jnp.concatenate([u_ref[n, prow, :] for n in range(LRU_BLOCKS)], axis=1)
        p_cum = jnp.concatenate([a_ref[n, prow, :] for n in range(LRU_BLOCKS)], axis=1)
        o_ref[0, rows, :] = (_gelu(y_ref[0, rows, :]) * (h_loc + p_cum * c)).astype(bf16)
        c = h_end[r8:r8 + 1, :] + p_end[r8:r8 + 1, :] * c
    carry_ref[...] = c


def _lru(lx, ly, cw, cb, wax, ba, bx, lam, tb):
    B, S, _ = lx.shape
    blk = pl.BlockSpec((1, tb, LRU_WIDTH), lambda b, s: (b, s, 0))
    return pl.pallas_call(
        functools.partial(_lru_kernel, tb=tb),
        grid=(B, S // tb),
        in_specs=[blk, blk, _const_spec((LRU_CONV_WIDTH, LRU_WIDTH)), _const_spec((1, LRU_WIDTH)),
                  _const_spec((LRU_BLOCKS, 128, 256)), _const_spec((1, LRU_WIDTH)),
                  _const_spec((1, LRU_WIDTH)), _const_spec((1, LRU_WIDTH))],
        out_specs=blk,
        out_shape=jax.ShapeDtypeStruct((B, S, LRU_WIDTH), bf16),
        scratch_shapes=[pltpu.VMEM((tb + 8, LRU_WIDTH), f32),
                        pltpu.VMEM((LRU_BLOCKS, 8 * _lru_pitch(tb), 128), f32),
                        pltpu.VMEM((LRU_BLOCKS, 8 * _lru_pitch(tb), 128), f32),
                        pltpu.VMEM((1, LRU_WIDTH), f32)],
        compiler_params=_params("parallel", "arbitrary"),
        name="lru",
    )(lx, ly, cw, cb, wax, ba, bx, lam)


def _cmp_kernel(kc_ref, vc_ref, pek_ref, w1k_ref, w2k_ref, pev_ref, w1v_ref, w2vT_ref,
                kcmp_ref, vcmpT_ref):
    nsub = kc_ref.shape[1]

    def hidden(sub_ref, pe_ref, w1_ref):
        sub = sub_ref[0]
        y0 = jnp.dot((sub + pe_ref[0:1, :]).astype(bf16), w1_ref[0], preferred_element_type=f32)
        y1 = jnp.dot((sub + pe_ref[1:2, :]).astype(bf16), w1_ref[1], preferred_element_type=f32)
        return _gelu(y0 + pltpu.roll(y1, nsub - 1, 0)).astype(bf16)

    kcmp_ref[0] = jnp.dot(hidden(kc_ref, pek_ref, w1k_ref), w2k_ref[...],
                          preferred_element_type=f32).astype(bf16)
    vcmpT_ref[0] = _nt(w2vT_ref[...], hidden(vc_ref, pev_ref, w1v_ref)).astype(bf16)


def _compress(kc, vc, pek, w1k, w2k, pev, w1v, w2vT):
    B, nsub, width = kc.shape
    blk = pl.BlockSpec((1, nsub, width), lambda b: (b, 0, 0))
    return pl.pallas_call(
        _cmp_kernel,
        grid=(B,),
        in_specs=[blk, blk, _const_spec((2, width)), _const_spec((2, width, 128)), _const_spec((128, 128)),
                  _const_spec((2, width)), _const_spec((2, width, 128)), _const_spec((128, 128))],
        out_specs=[pl.BlockSpec((1, nsub, 128), lambda b: (b, 0, 0)),
                   pl.BlockSpec((1, 128, nsub), lambda b: (b, 0, 0))],
        out_shape=[jax.ShapeDtypeStruct((B, nsub, 128), bf16), jax.ShapeDtypeStruct((B, 128, nsub), bf16)],
        compiler_params=_params("parallel"),
        name="nsa_compress",
    )(kc, vc, pek, w1k, w2k, pev, w1v, w2vT)


def _nsa_kernel(qT_ref, gT_ref, kcmp_ref, vcmpT_ref, ks_ref, vsT_ref, kw_ref, vwT_ref, covT_ref, eblk_ref,
                o_ref, qaug_ref, s_ref, acc_ref, m_ref, smax_ref, *, n_slc, n_top):
    i = pl.program_id(1)
    q0 = i * Q_BLOCK
    HQ = NSA_HPG * Q_BLOCK
    GROUPS = range(NSA_KV_HEADS)
    DH = NSA_HEAD_DIM

    def per_head(m):
        return jnp.concatenate([m] * NSA_HPG, axis=1)

    def vrows(g):
        return slice(g * DH, (g + 1) * DH)

    def padded_q(g):
        base = g * NSA_HPG * DH
        qh = jnp.concatenate([qT_ref[base + hp * DH:base + (hp + 1) * DH, :] for hp in range(NSA_HPG)], axis=1)
        zq = jnp.zeros_like(qh)
        return jnp.concatenate([qh, zq] if g == 0 else [zq, qh], axis=0)

    qpad = [padded_q(g) for g in GROUPS]
    t_row = q0 + lax.broadcasted_iota(jnp.int32, (1, Q_BLOCK), 1)

    ncmp = kcmp_ref.shape[1]

    def compressed(rows):
        n_iota = lax.broadcasted_iota(jnp.int32, (rows, Q_BLOCK), 0)
        bias_c = per_head(jnp.where((n_iota * CMP_STRIDE + (CMP_BLOCK - 1)) <= t_row, 0.0, NEG_INF))
        has_key = per_head(t_row) >= CMP_BLOCK - 1
        kc = kcmp_ref[0, 0:rows, :]
        cov = covT_ref[:, 0:rows]
        out = []
        for g in GROUPS:
            sc = jnp.dot(kc, qpad[g], preferred_element_type=f32) + bias_c
            pc = jnp.exp2(sc - jnp.max(sc, axis=0, keepdims=True))
            lc = jnp.sum(pc, axis=0, keepdims=True)
            pc = pc * jnp.where(has_key, 1.0 / lc, 0.0)
            o = jnp.dot(vcmpT_ref[0, vrows(g), 0:rows], pc.astype(bf16), preferred_element_type=f32)
            psum = pc[:, 0:Q_BLOCK]
            for hp in range(1, NSA_HPG):
                psum = psum + pc[:, hp * Q_BLOCK:(hp + 1) * Q_BLOCK]
            hi = psum.astype(bf16)
            r1 = psum - hi.astype(f32)
            mid = r1.astype(bf16)
            lo = (r1 - mid.astype(f32)).astype(bf16)
            out += [o, (jnp.dot(cov, hi, preferred_element_type=f32) + jnp.dot(cov, mid, preferred_element_type=f32)
                        + jnp.dot(cov, lo, preferred_element_type=f32))]
        return tuple(out)

    row_options = [r for r in (ncmp // 4, ncmp // 2, 3 * ncmp // 4, ncmp) if r % 128 == 0]
    need = (Q_BLOCK // CMP_STRIDE) * (i + 1)
    which = sum((need > r).astype(jnp.int32) for r in row_options[:-1])
    cmp_out = lax.switch(which, [functools.partial(compressed, r) for r in row_options])
    oc = [cmp_out[2 * g] for g in GROUPS]
    imp = [cmp_out[2 * g + 1] for g in GROUPS]

    j_i = lax.broadcasted_iota(jnp.int32, (n_slc, Q_BLOCK), 0)
    j_f = j_i.astype(f32)
    cur = t_row >> 6
    forced = (j_i == 0) | (j_i == cur) | (j_i == cur - 1)
    visible = (j_i * SLC_BLOCK) <= t_row
    score = [jnp.where(forced, TAKEN_SCORE, jnp.where(visible, imp[g], NEG_INF)) for g in GROUPS]

    w0 = pl.multiple_of(jnp.maximum(q0 - WINDOW, 0), Q_BLOCK)
    kp = w0 + lax.broadcasted_iota(jnp.int32, (WIN_KEYS, Q_BLOCK), 0)
    bias_w = per_head(jnp.where((kp <= t_row) & (kp > (t_row - WINDOW)), 0.0, NEG_INF))
    kwin = kw_ref[pl.ds(w0, WIN_KEYS), :]
    ow = []
    for g in GROUPS:
        sw = jnp.dot(kwin, qpad[g], preferred_element_type=f32) + bias_w
        pw = jnp.exp2(sw - jnp.max(sw, axis=0, keepdims=True)).astype(bf16)
        vw1 = jnp.concatenate([vwT_ref[vrows(g), pl.ds(w0, WIN_KEYS)], jnp.ones((ONES_ROWS, WIN_KEYS), bf16)],
                              axis=0)
        o = jnp.dot(vw1, pw, preferred_element_type=f32)
        ow.append(o[0:DH, :] * (1.0 / o[DH:DH + 1, :]))

    sel = [jnp.where(forced, 1.0, 0.0) for g in GROUPS]
    for _ in range(n_top - 3):
        for g in GROUPS:
            mx = jnp.max(score[g], axis=0, keepdims=True)
            first = jnp.min(jnp.where(score[g] == mx, j_f, float(n_slc)), axis=0, keepdims=True)
            hit = j_f == first
            sel[g] = jnp.where(hit, 1.0, sel[g])
            score[g] = jnp.where(hit, TAKEN_SCORE, score[g])

    for g in GROUPS:
        qaug_ref[g, 0:2 * DH, :] = qpad[g]
        blk_bias = jnp.where((sel[g] > 0.0) & visible, 0.0, NEG_INF)
        qaug_ref[g, 2 * DH:2 * DH + n_slc, :] = per_head(blk_bias).astype(bf16)
        if n_slc < 128:
            qaug_ref[g, 2 * DH + n_slc:, :] = jnp.zeros((128 - n_slc, HQ), bf16)

    def scores(c):
        k0 = pl.multiple_of(c * SEL_CHUNK, SEL_CHUNK)
        kaug = jnp.concatenate([ks_ref[pl.ds(k0, SEL_CHUNK), :], eblk_ref[pl.ds(k0, SEL_CHUNK), :]], axis=1)
        return [jnp.dot(kaug, qaug_ref[g], preferred_element_type=f32) for g in GROUPS]

    def attend(g, c, s, smax):
        k0 = pl.multiple_of(c * SEL_CHUNK, SEL_CHUNK)
        m = m_ref[g]
        m_new = jnp.maximum(m, smax)
        m_ref[g] = m_new
        p = jnp.exp2(s - m_new).astype(bf16)
        v1 = jnp.concatenate([vsT_ref[vrows(g), pl.ds(k0, SEL_CHUNK)], jnp.ones((ONES_ROWS, SEL_CHUNK), bf16)],
                             axis=0)
        acc_ref[g] = jnp.exp2(m - m_new) * acc_ref[g] + jnp.dot(v1, p, preferred_element_type=f32)

    def step(c, cur, nxt):
        smax = [smax_ref[g] for g in GROUPS]
        s_next = scores(c + 1)
        for g in GROUPS:
            s_ref[g, nxt] = s_next[g]
            smax_ref[g] = jnp.max(s_next[g], axis=0, keepdims=True)
        for g in GROUPS:
            attend(g, c, s_ref[g, cur], smax[g])

    last = q0 // SEL_CHUNK
    odd = last % 2
    s0 = scores(0)
    for g in GROUPS:
        s_ref[g, odd] = s0[g]
        smax_ref[g] = jnp.max(s0[g], axis=0, keepdims=True)
        acc_ref[g] = jnp.zeros((DH + ONES_ROWS, HQ), f32)
        m_ref[g] = jnp.full((1, HQ), NEG_INF, f32)

    @pl.when(odd == 1)
    def _():
        step(0, 1, 0)

    has_pair = (last // 2) % 2

    @pl.when(has_pair == 1)
    def _():
        step(odd, 0, 1)
        step(odd + 1, 1, 0)

    def quad(qq, carry):
        c = odd + 2 * has_pair + 4 * qq
        step(c, 0, 1)
        step(c + 1, 1, 0)
        step(c + 2, 0, 1)
        step(c + 3, 1, 0)
        return carry

    lax.fori_loop(0, last // 4, quad, 0)

    kp_d = last * SEL_CHUNK + lax.broadcasted_iota(jnp.int32, (SEL_CHUNK, Q_BLOCK), 0)
    bias_d = per_head(jnp.where(kp_d <= t_row, 0.0, NEG_INF))
    for g in GROUPS:
        s_d = s_ref[g, 0] + bias_d
        attend(g, last, s_d, jnp.max(s_d, axis=0, keepdims=True))

    gate = _sigmoid(gT_ref[...])
    for g in GROUPS:
        osel = acc_ref[g, 0:DH, :] * (1.0 / acc_ref[g, DH:DH + 1, :])
        for hp in range(NSA_HPG):
            cols = slice(hp * Q_BLOCK, (hp + 1) * Q_BLOCK)
            gr = g * GATE_ROWS + hp
            o = (gate[gr:gr + 1, :] * oc[g][:, cols] + gate[gr + 4:gr + 5, :] * osel[:, cols]
                 + gate[gr + 8:gr + 9, :] * ow[g][:, cols])
            o_ref[(g * NSA_HPG + hp) * DH:(g * NSA_HPG + hp + 1) * DH, :] = o.astype(bf16)


def _nsa(qT, gT, kcmp, vcmpT, ks, vsT, kw, vwT, covT, eblk, B, S):
    n_q = S // Q_BLOCK
    n_slc = S // SLC_BLOCK
    ncmp = S // CMP_STRIDE
    G = NSA_KV_HEADS
    HQ = NSA_HPG * Q_BLOCK
    tok = lambda r: pl.BlockSpec((r, Q_BLOCK), lambda b, i: (0, b * n_q + i))
    per_b_nat = pl.BlockSpec((S, 128), lambda b, i: (b, 0))
    per_b_tr = pl.BlockSpec((128, S), lambda b, i: (0, b))
    return pl.pallas_call(
        functools.partial(_nsa_kernel, n_slc=n_slc, n_top=min(N_SELECT, n_slc)),
        grid=(B, n_q),
        in_specs=[tok(NSA_HEADS * NSA_HEAD_DIM), tok(G * GATE_ROWS),
                  pl.BlockSpec((1, ncmp, 128), lambda b, i: (b, 0, 0)),
                  pl.BlockSpec((1, 128, ncmp), lambda b, i: (b, 0, 0)),
                  per_b_nat, per_b_tr, per_b_nat, per_b_tr,
                  _const_spec((n_slc, ncmp)), _const_spec((S, 128))],
        out_specs=tok(NSA_HEADS * NSA_HEAD_DIM),
        out_shape=jax.ShapeDtypeStruct((NSA_HEADS * NSA_HEAD_DIM, B * S), bf16),
        scratch_shapes=[pltpu.VMEM((G, 2 * NSA_HEAD_DIM + 128, HQ), bf16),
                        pltpu.VMEM((G, 2, SEL_CHUNK, HQ), f32),
                        pltpu.VMEM((G, NSA_HEAD_DIM + ONES_ROWS, HQ), f32),
                        pltpu.VMEM((G, 1, HQ), f32), pltpu.VMEM((G, 1, HQ), f32)],
        compiler_params=_params("parallel", "arbitrary"),
        name="nsa_attn",
    )(qT, gT, kcmp, vcmpT, ks, vsT, kw, vwT, covT, eblk)


def _memkv_kernel(mem_ref, wk_ref, wvT_ref, k_ref, vT_ref):
    m = mem_ref[0].astype(bf16)
    k_ref[0] = jnp.dot(m, wk_ref[...], preferred_element_type=f32).astype(bf16)
    vT_ref[0] = _nt(wvT_ref[...], m).astype(bf16)


def _memkv(mem, wk, wvT):
    B, M, _ = mem.shape
    return pl.pallas_call(
        _memkv_kernel,
        grid=(B,),
        in_specs=[pl.BlockSpec((1, M, D_MODEL), lambda b: (b, 0, 0)),
                  _const_spec((D_MODEL, 512)), _const_spec((512, D_MODEL))],
        out_specs=[pl.BlockSpec((1, M, 512), lambda b: (b, 0, 0)),
                   pl.BlockSpec((1, 512, M), lambda b: (b, 0, 0))],
        out_shape=[jax.ShapeDtypeStruct((B, M, 512), bf16), jax.ShapeDtypeStruct((B, 512, M), bf16)],
        compiler_params=_params("parallel"),
        name="mem_kv",
    )(mem, wk, wvT)


def _memattn_kernel(qT_ref, k_ref, vT_ref, o_ref):
    for h in range(MEM_HEADS):
        rows = slice(h * MEM_HEAD_DIM, (h + 1) * MEM_HEAD_DIM)
        s = jnp.dot(k_ref[0, :, rows], qT_ref[rows, :], preferred_element_type=f32) * (MEM_HEAD_DIM ** -0.5)
        p = jnp.exp(s - jnp.max(s, axis=0, keepdims=True))
        l = jnp.sum(p, axis=0, keepdims=True)
        o = jnp.dot(vT_ref[0, rows, :], p.astype(bf16), preferred_element_type=f32) * (1.0 / l)
        o_ref[rows, :] = o.astype(bf16)


def _memattn(mqT, mk, mvT, B, S, tq):
    M = mk.shape[1]
    nq = S // tq
    return pl.pallas_call(
        _memattn_kernel,
        grid=(B, nq),
        in_specs=[pl.BlockSpec((512, tq), lambda b, i: (0, b * nq + i)),
                  pl.BlockSpec((1, M, 512), lambda b, i: (b, 0, 0)),
                  pl.BlockSpec((1, 512, M), lambda b, i: (b, 0, 0))],
        out_specs=pl.BlockSpec((512, tq), lambda b, i: (0, b * nq + i)),
        out_shape=jax.ShapeDtypeStruct((512, B * S), bf16),
        compiler_params=_params("parallel", "parallel"),
        name="mem_attn",
    )(mqT, mk, mvT)


def _outproj_kernel(h0_ref, ol_ref, onT_ref, omT_ref, wo_ref, g1_ref, b1_ref, h1_ref, *, tm):
    for r0 in range(0, tm, OUT_SUB_ROWS):
        rows = slice(r0, r0 + OUT_SUB_ROWS)
        mixed = jnp.dot(ol_ref[rows, :], wo_ref[0:1024, :], preferred_element_type=f32)
        mixed = mixed + _tn(onT_ref[:, rows], wo_ref[1024:1536, :])
        mixed = mixed + _tn(omT_ref[:, rows], wo_ref[1536:2048, :])
        h1_ref[rows, :] = _ln(ALPHA * h0_ref[rows, :] + mixed, g1_ref[...], b1_ref[...])


def _outproj(h0, ol, onT, omT, wo, g1, b1, tm):
    T = h0.shape[0]
    row = lambda w: pl.BlockSpec((tm, w), lambda i: (i, 0))
    trs = pl.BlockSpec((512, tm), lambda i: (0, i))
    vec = _const_spec((1, D_MODEL))
    return pl.pallas_call(
        functools.partial(_outproj_kernel, tm=tm),
        grid=(T // tm,),
        in_specs=[row(D_MODEL), row(1024), trs, trs, _const_spec((D_MODEL, D_MODEL)), vec, vec],
        out_specs=row(D_MODEL),
        out_shape=jax.ShapeDtypeStruct((T, D_MODEL), f32),
        compiler_params=_params("parallel"),
        name="out_proj_ln1",
    )(h0, ol, onT, omT, wo, g1, b1)


def _ffn_kernel(h_ref, halo_ref, wg_ref, wu_ref, cwg_ref, cwu_ref, cbg_ref, cbu_ref, wd_ref,
                g2_ref, b2_ref, o_ref, lhs_ref, *, tm, blocks_per_seq):
    i = pl.program_id(0)
    j = pl.program_id(1)

    @pl.when(j == 0)
    def _():
        first = (i % blocks_per_seq) == 0
        lhs_ref[0:8, :] = jnp.where(first, 0.0, halo_ref[...]).astype(bf16)
        h = h_ref[...]
        lhs_ref[8:8 + tm, :] = h.astype(bf16)
        o_ref[...] = ALPHA * h

    lhs = lhs_ref[...]

    def conv(w_ref, cw_ref, cb_ref):
        up = jnp.dot(lhs, w_ref[...], preferred_element_type=f32)
        y = cb_ref[...] + pltpu.roll(up, 2, 0)[8:8 + tm, :] * cw_ref[0:1, :]
        y = y + pltpu.roll(up, 1, 0)[8:8 + tm, :] * cw_ref[1:2, :]
        return y + up[8:8 + tm, :] * cw_ref[2:3, :]

    act = (_gelu(conv(wg_ref, cwg_ref, cbg_ref)) * conv(wu_ref, cwu_ref, cbu_ref)).astype(bf16)
    o_ref[...] += jnp.dot(act, wd_ref[...], preferred_element_type=f32)

    @pl.when(j == pl.num_programs(1) - 1)
    def _():
        o_ref[...] = _ln(o_ref[...], g2_ref[...], b2_ref[...])


def _ffn(h1, wup, cw, cb, wd, g2, b2, tm, tf, blocks_per_seq):
    T = h1.shape[0]
    nf = D_FF // tf
    vec = _const_spec((1, D_MODEL))
    return pl.pallas_call(
        functools.partial(_ffn_kernel, tm=tm, blocks_per_seq=blocks_per_seq),
        grid=(T // tm, nf),
        in_specs=[pl.BlockSpec((tm, D_MODEL), lambda i, j: (i, 0), pipeline_mode=pl.Buffered(1)),
                  pl.BlockSpec((8, D_MODEL), lambda i, j: (jnp.maximum(i * (tm // 8) - 1, 0), 0)),
                  pl.BlockSpec((D_MODEL, tf), lambda i, j: (0, j)),
                  pl.BlockSpec((D_MODEL, tf), lambda i, j: (0, nf + j)),
                  pl.BlockSpec((FFN_CONV_WIDTH, tf), lambda i, j: (0, j)),
                  pl.BlockSpec((FFN_CONV_WIDTH, tf), lambda i, j: (0, nf + j)),
                  pl.BlockSpec((1, tf), lambda i, j: (0, j)),
                  pl.BlockSpec((1, tf), lambda i, j: (0, nf + j)),
                  pl.BlockSpec((tf, D_MODEL), lambda i, j: (j, 0)),
                  vec, vec],
        out_specs=pl.BlockSpec((tm, D_MODEL), lambda i, j: (i, 0)),
        out_shape=jax.ShapeDtypeStruct((T, D_MODEL), f32),
        scratch_shapes=[pltpu.VMEM((tm + 8, D_MODEL), bf16)],
        compiler_params=_params("parallel", "arbitrary"),
        name="conv_ffn_ln2",
    )(h1, h1, wup, wup, cw, cw, cb, cb, wd, g2, b2)


def _layer(x, mem, ln_in_g, ln_in_b, w_in, lru_conv_w, lru_conv_b, lru_wa, lru_ba, lru_wx, lru_bx,
           lru_lam, cmp_pe_k, cmp_w1_k, cmp_w2_k, cmp_pe_v, cmp_w1_v, cmp_w2_v, w_mem_kv, w_out,
           ln1_g, ln1_b, ffn_w_up, ffn_conv_w, ffn_conv_b, ffn_w_down, ln2_g, ln2_b,
           *, tm_proj, tb_lru, tq_mem, tm_out, tm_ffn, tf_ffn):
    B, S, _ = x.shape
    T = B * S
    G, Dh = NSA_KV_HEADS, NSA_HEAD_DIM
    row = lambda v: v.reshape(1, -1)

    c_q = 2 * LRU_WIDTH
    c_kv = c_q + NSA_HEADS * Dh
    c_gate = c_kv + 6 * G * Dh
    c_mq = c_gate + 3 * NSA_HEADS
    kv_cols = [w_in[:, c_kv + n * G * Dh: c_kv + (n + 1) * G * Dh] for n in range(6)]
    wn = jnp.concatenate([w_in[:, :c_q], kv_cols[0], kv_cols[1], kv_cols[2], kv_cols[4]], axis=1).astype(bf16)
    w_gate = w_in[:, c_gate:c_mq].reshape(D_MODEL, G, NSA_HPG, 3)
    w_gate = jnp.pad(w_gate.transpose(1, 3, 2, 0), ((0, 0), (0, 1), (0, 0), (0, 0)))
    wt = jnp.concatenate([w_in[:, c_q:c_kv].T, kv_cols[3].T, kv_cols[5].T, w_in[:, c_mq:].T,
                          w_gate.reshape(2 * 16, D_MODEL)], axis=0).astype(bf16)

    wax = jnp.concatenate([lru_wa, lru_wx], axis=-1).astype(bf16)

    def cmp_weights(pe, w1, w2):
        pe2 = jnp.broadcast_to(pe.reshape(2, CMP_STRIDE, 1, Dh), (2, CMP_STRIDE, G, Dh)).reshape(2, -1)
        eye = jnp.eye(G, dtype=f32)
        w1e = jnp.einsum('hlde,gk->hlgdke', w1.reshape(2, CMP_STRIDE, Dh, Dh), eye)
        w1e = w1e.reshape(2, CMP_STRIDE * G * Dh, G * Dh).astype(bf16)
        w2e = jnp.einsum('ef,gk->gekf', w2, eye).reshape(G * Dh, G * Dh)
        return pe2, w1e, w2e

    pek, w1k, w2k = cmp_weights(cmp_pe_k, cmp_w1_k, cmp_w2_k)
    pev, w1v, w2v = cmp_weights(cmp_pe_v, cmp_w1_v, cmp_w2_v)

    n_slc, ncmp = S // SLC_BLOCK, S // CMP_STRIDE
    ci = jnp.arange(ncmp)[None, :] * CMP_STRIDE
    sj = jnp.arange(n_slc)[:, None] * SLC_BLOCK
    covT = ((ci <= sj + SLC_BLOCK - 1) & (ci + CMP_BLOCK - 1 >= sj)).astype(bf16)
    eblk = (jnp.arange(S)[:, None] // SLC_BLOCK == jnp.arange(128)[None, :]).astype(bf16)

    wk_mem = w_mem_kv[:, :512].astype(bf16)
    wvT_mem = w_mem_kv[:, 512:].T.astype(bf16)
    wo = w_out.astype(bf16)
    wup = ffn_w_up.astype(bf16)
    wd = ffn_w_down.astype(bf16)

    x2 = x.reshape(T, D_MODEL)
    lx, ly, kc, vc, ks, kw, qT, vsT, vwT, mqT, gT, h0 = _proj(x2, row(ln_in_g), row(ln_in_b), wn, wt, tm_proj)

    o_lru = _lru(lx.reshape(B, S, LRU_WIDTH), ly.reshape(B, S, LRU_WIDTH), lru_conv_w, row(lru_conv_b),
                 wax, row(lru_ba), row(lru_bx), row(lru_lam), tb_lru)

    sub = lambda a: a.reshape(B, ncmp, CMP_STRIDE * G * Dh)
    kcmp, vcmpT = _compress(sub(kc), sub(vc), pek, w1k, w2k.astype(bf16), pev, w1v, w2v.T.astype(bf16))
    o_nsaT = _nsa(qT, gT, kcmp, vcmpT, ks, vsT, kw, vwT, covT, eblk, B, S)

    mk, mvT = _memkv(mem, wk_mem, wvT_mem)
    o_memT = _memattn(mqT, mk, mvT, B, S, tq_mem)

    h1 = _outproj(h0, o_lru.reshape(T, LRU_WIDTH), o_nsaT, o_memT, wo, row(ln1_g), row(ln1_b), tm_out)
    out = _ffn(h1, wup, ffn_conv_w, row(ffn_conv_b), wd, row(ln2_g), row(ln2_b), tm_ffn, tf_ffn, S // tm_ffn)
    return out.reshape(B, S, D_MODEL)


def kernel(x, mem, ln_in_g, ln_in_b, w_in, lru_conv_w, lru_conv_b, lru_wa, lru_ba, lru_wx, lru_bx,
           lru_lam, cmp_pe_k, cmp_w1_k, cmp_w2_k, cmp_pe_v, cmp_w1_v, cmp_w2_v, w_mem_kv, w_out,
           ln1_g, ln1_b, ffn_w_up, ffn_conv_w, ffn_conv_b, ffn_w_down, ln2_g, ln2_b):
    return _layer(x, mem, ln_in_g, ln_in_b, w_in[0], lru_conv_w[0], lru_conv_b[0], lru_wa[0], lru_ba[0],
                  lru_wx[0], lru_bx[0], lru_lam[0], cmp_pe_k[0], cmp_w1_k[0], cmp_w2_k[0], cmp_pe_v[0],
                  cmp_w1_v[0], cmp_w2_v[0], w_mem_kv[0], w_out[0], ln1_g[0], ln1_b[0], ffn_w_up[0],
                  ffn_conv_w[0], ffn_conv_b[0], ffn_w_down[0], ln2_g[0], ln2_b[0],
                  tm_proj=512, tb_lru=256, tq_mem=512, tm_out=1024, tm_ffn=1024, tf_ffn=512)
```

```python
import functools
import math

import jax
import jax.numpy as jnp
from jax import lax
from jax.experimental import pallas as pl
from jax.experimental.pallas import tpu as pltpu

f32 = jnp.float32
bf16 = jnp.bfloat16

D_MODEL = 2048
LRU_WIDTH = 1024
LRU_BLOCKS = 8
LRU_BLOCK_DIM = 128
LRU_CONV_WIDTH = 4
LRU_C = 8.0
LRU_TINY = 1e-30
NSA_HEADS = 8
NSA_KV_HEADS = 2
NSA_HPG = NSA_HEADS // NSA_KV_HEADS
NSA_HEAD_DIM = 64
NSA_Q_SCALE = NSA_HEAD_DIM ** -0.5 * math.log2(math.e)
CMP_STRIDE = 16
CMP_BLOCK = 32
SLC_BLOCK = 64
N_SELECT = 16
WINDOW = 512
Q_BLOCK = 128
MEM_HEADS = 4
MEM_HEAD_DIM = 128
D_FF = 5632
FFN_CONV_WIDTH = 3
LN_EPS = 1e-5
NEG_INF = -1e30
FORCE_SCORE = 1e9
TAKEN_SCORE = -3e38
ALPHA = 2.0 ** 0.25

V7X_VMEM_LIMIT_BYTES = 56 * 1024 * 1024
V7X_VMEM_LIMIT_FFN_BYTES = 61 * 1024 * 1024

_NX0, _NY0, _NQ0, _NKC0, _NKS0, _NCOLS = 0, 1024, 2048, 2560, 2816, 3328

SEL_CHUNK = 512
WIN_KEYS = WINDOW + Q_BLOCK
GATE_ROWS = 16
OUT_SUB_ROWS = 512
ONES_ROWS = 16


def _ln(x, g, b):
    mu = jnp.mean(x, axis=-1, keepdims=True)
    xc = x - mu
    var = jnp.mean(xc * xc, axis=-1, keepdims=True)
    return xc * lax.rsqrt(var + LN_EPS) * g + b


def _gelu(x):
    return jax.nn.gelu(x)


def _sigmoid(x):
    return 1.0 / (1.0 + jnp.exp(-x))


def _sigmoid_tanh(x):
    return 0.5 * jnp.tanh(0.5 * x) + 0.5


def _nt(a, b):
    return lax.dot_general(a, b, (((1,), (1,)), ((), ())), preferred_element_type=f32)


def _tn(a, b):
    return lax.dot_general(a, b, (((0,), (0,)), ((), ())), preferred_element_type=f32)


def _params(*sem, vmem_limit=V7X_VMEM_LIMIT_BYTES):
    return pltpu.CompilerParams(dimension_semantics=sem, vmem_limit_bytes=vmem_limit)


def _const_spec(shape):
    nd = len(shape)
    return pl.BlockSpec(shape, lambda *_: (0,) * nd, pipeline_mode=pl.Buffered(1))


def _proj_kernel(x_ref, g_ref, b_ref, wn_ref, wmq_ref, wgT_ref,
                 lx_ref, ly_ref, kc_ref, vc_ref, ks_ref, kw_ref,
                 qT_ref, vsT_ref, vwT_ref, mqT_ref, gT_ref, h0_ref):
    h0 = _ln(x_ref[...], g_ref[...], b_ref[...])
    h0_ref[...] = h0
    h = h0.astype(bf16)

    def nat(c0, c1):
        return jnp.dot(h, wn_ref[:, c0:c1], preferred_element_type=f32)

    lx_ref[...] = nat(_NX0, _NY0)
    ly_ref[...] = nat(_NY0, _NQ0)
    qT_ref[...] = (nat(_NQ0, _NKC0) * NSA_Q_SCALE).T.astype(bf16)
    kvc = nat(_NKC0, _NKS0)
    kc_ref[...] = kvc[:, 0:128]
    vc_ref[...] = kvc[:, 128:256]
    kvsw = nat(_NKS0, _NCOLS)
    ks_ref[...] = kvsw[:, 0:128].astype(bf16)
    vsT_ref[...] = kvsw[:, 128:256].T.astype(bf16)
    kw_ref[...] = kvsw[:, 256:384].astype(bf16)
    vwT_ref[...] = kvsw[:, 384:512].T.astype(bf16)
    mqT_ref[...] = jnp.dot(h, wmq_ref[...], preferred_element_type=f32).T.astype(bf16)
    gT_ref[...] = _nt(wgT_ref[...], h)


def _proj(x2, g, b, wn, wmq, wgT, tm):
    T = x2.shape[0]
    nat = lambda w: pl.BlockSpec((tm, w), lambda i: (i, 0))
    trs = lambda r: pl.BlockSpec((r, tm), lambda i: (0, i))
    return pl.pallas_call(
        _proj_kernel,
        grid=(T // tm,),
        in_specs=[pl.BlockSpec((tm, D_MODEL), lambda i: (i, 0)),
                  _const_spec((1, D_MODEL)), _const_spec((1, D_MODEL)),
                  _const_spec((D_MODEL, _NCOLS)), _const_spec((D_MODEL, MEM_HEADS * MEM_HEAD_DIM)),
                  _const_spec((NSA_KV_HEADS * GATE_ROWS, D_MODEL))],
        out_specs=[nat(1024), nat(1024), nat(128), nat(128), nat(128), nat(128),
                   trs(512), trs(128), trs(128), trs(512), trs(32), nat(D_MODEL)],
        out_shape=[jax.ShapeDtypeStruct((T, 1024), f32), jax.ShapeDtypeStruct((T, 1024), f32),
                   jax.ShapeDtypeStruct((T, 128), f32), jax.ShapeDtypeStruct((T, 128), f32),
                   jax.ShapeDtypeStruct((T, 128), bf16), jax.ShapeDtypeStruct((T, 128), bf16),
                   jax.ShapeDtypeStruct((512, T), bf16), jax.ShapeDtypeStruct((128, T), bf16),
                   jax.ShapeDtypeStruct((128, T), bf16), jax.ShapeDtypeStruct((512, T), bf16),
                   jax.ShapeDtypeStruct((32, T), f32), jax.ShapeDtypeStruct((T, D_MODEL), f32)],
        compiler_params=_params("parallel"),
        name="proj",
    )(x2, g, b, wn, wmq, wgT)


def _lru_pitch(tb):
    seg = tb // 8
    return seg + 8 if (seg // 8) % 2 == 0 else seg


def _lru_kernel(x_ref, y_ref, cw_ref, cb_ref, wax_ref, ba_ref, bx_ref, lam_ref, o_ref,
                xext_ref, a_ref, u_ref, carry_ref, *, tb):
    s = pl.program_id(1)

    @pl.when(s == 0)
    def _():
        xext_ref[0:8, :] = jnp.zeros((8, LRU_WIDTH), f32)
        carry_ref[...] = jnp.zeros((1, LRU_WIDTH), f32)

    x = x_ref[0]
    xext_ref[8:8 + tb, :] = x
    xe = xext_ref[...]
    xc = cb_ref[...]
    for k in range(LRU_CONV_WIDTH):
        back = LRU_CONV_WIDTH - 1 - k
        shifted = pltpu.roll(xe, back, 0)[8:8 + tb, :] if back else x
        xc = xc + shifted * cw_ref[k:k + 1, :]
    xext_ref[0:8, :] = x[tb - 8:tb, :]

    xb = xc.astype(bf16)
    gates = [jnp.dot(xb[:, n * 128:(n + 1) * 128], wax_ref[n], preferred_element_type=f32)
             for n in range(LRU_BLOCKS)]
    r = _sigmoid_tanh(jnp.concatenate([gt[:, 0:128] for gt in gates], axis=1) + ba_ref[...])
    i = _sigmoid_tanh(jnp.concatenate([gt[:, 128:256] for gt in gates], axis=1) + bx_ref[...])
    nl = -lam_ref[...]
    softplus = jnp.maximum(nl, 0.0) + jnp.log1p(jnp.exp(-jnp.abs(nl)))
    log_a = (-LRU_C) * r * softplus
    a = jnp.exp(log_a)
    th = jnp.tanh(log_a)
    z = -2.0 * th / (1.0 - th)
    u = (z * lax.rsqrt(jnp.maximum(z, LRU_TINY))) * (i * xc)
    seg = tb // 8
    pitch = _lru_pitch(tb)
    for n in range(LRU_BLOCKS):
        for r8 in range(8):
            a_ref[n, r8 * pitch:r8 * pitch + seg, :] = a[r8 * seg:(r8 + 1) * seg, n * 128:(n + 1) * 128]
            u_ref[n, r8 * pitch:r8 * pitch + seg, :] = u[r8 * seg:(r8 + 1) * seg, n * 128:(n + 1) * 128]

    def sweep(k, carry):
        rows = pl.ds(k, 8, stride=pitch)
        out = []
        for n in range(LRU_BLOCKS):
            h, p = carry[n]
            ak = a_ref[n, rows, :]
            h = ak * h + u_ref[n, rows, :]
            p = ak * p
            u_ref[n, rows, :] = h
            a_ref[n, rows, :] = p
            out.append((h, p))
        return tuple(out)

    init = tuple((jnp.zeros((8, 128), f32), jnp.ones((8, 128), f32)) for _ in range(LRU_BLOCKS))
    ends = lax.fori_loop(0, seg, sweep, init, unroll=4)
    h_end = jnp.concatenate([e[0] for e in ends], axis=1)
    p_end = jnp.concatenate([e[1] for e in ends], axis=1)

    c = carry_ref[...]
    for r8 in range(8):
        rows = slice(r8 * seg, (r8 + 1) * seg)
        prow = slice(r8 * pitch, r8 * pitch + seg)
        h_loc = jnp.concatenate([u_ref[n, prow, :] for n in range(LRU_BLOCKS)], axis=1)
        p_cum = jnp.concatenate([a_ref[n, prow, :] for n in range(LRU_BLOCKS)], axis=1)
        o_ref[0, rows, :] = (_gelu(y_ref[0, rows, :]) * (h_loc + p_cum * c)).astype(bf16)
        c = h_end[r8:r8 + 1, :] + p_end[r8:r8 + 1, :] * c
    carry_ref[...] = c


def _lru(lx, ly, cw, cb, wax, ba, bx, lam, tb):
    B, S, _ = lx.shape
    blk = pl.BlockSpec((1, tb, LRU_WIDTH), lambda b, s: (b, s, 0))
    return pl.pallas_call(
        functools.partial(_lru_kernel, tb=tb),
        grid=(B, S // tb),
        in_specs=[blk, blk, _const_spec((LRU_CONV_WIDTH, LRU_WIDTH)), _const_spec((1, LRU_WIDTH)),
                  _const_spec((LRU_BLOCKS, 128, 256)), _const_spec((1, LRU_WIDTH)),
                  _const_spec((1, LRU_WIDTH)), _const_spec((1, LRU_WIDTH))],
        out_specs=blk,
        out_shape=jax.ShapeDtypeStruct((B, S, LRU_WIDTH), bf16),
        scratch_shapes=[pltpu.VMEM((tb + 8, LRU_WIDTH), f32),
                        pltpu.VMEM((LRU_BLOCKS, 8 * _lru_pitch(tb), 128), f32),
                        pltpu.VMEM((LRU_BLOCKS, 8 * _lru_pitch(tb), 128), f32),
                        pltpu.VMEM((1, LRU_WIDTH), f32)],
        compiler_params=_params("parallel", "arbitrary"),
        name="lru",
    )(lx, ly, cw, cb, wax, ba, bx, lam)


def _cmp_kernel(kc_ref, vc_ref, pek_ref, w1k_ref, w2k_ref, pev_ref, w1v_ref, w2vT_ref,
                kcmp_ref, vcmpT_ref):
    nsub = kc_ref.shape[1]

    def hidden(sub_ref, pe_ref, w1_ref):
        sub = sub_ref[0]
        y0 = jnp.dot((sub + pe_ref[0:1, :]).astype(bf16), w1_ref[0], preferred_element_type=f32)
        y1 = jnp.dot((sub + pe_ref[1:2, :]).astype(bf16), w1_ref[1], preferred_element_type=f32)
        return _gelu(y0 + pltpu.roll(y1, nsub - 1, 0)).astype(bf16)

    kcmp_ref[0] = jnp.dot(hidden(kc_ref, pek_ref, w1k_ref), w2k_ref[...],
                          preferred_element_type=f32).astype(bf16)
    vcmpT_ref[0] = _nt(w2vT_ref[...], hidden(vc_ref, pev_ref, w1v_ref)).astype(bf16)


def _compress(kc, vc, pek, w1k, w2k, pev, w1v, w2vT):
    B, nsub, width = kc.shape
    blk = pl.BlockSpec((1, nsub, width), lambda b: (b, 0, 0))
    return pl.pallas_call(
        _cmp_kernel,
        grid=(B,),
        in_specs=[blk, blk, _const_spec((2, width)), _const_spec((2, width, 128)), _const_spec((128, 128)),
                  _const_spec((2, width)), _const_spec((2, width, 128)), _const_spec((128, 128))],
        out_specs=[pl.BlockSpec((1, nsub, 128), lambda b: (b, 0, 0)),
                   pl.BlockSpec((1, 128, nsub), lambda b: (b, 0, 0))],
        out_shape=[jax.ShapeDtypeStruct((B, nsub, 128), bf16), jax.ShapeDtypeStruct((B, 128, nsub), bf16)],
        compiler_params=_params("parallel"),
        name="nsa_compress",
    )(kc, vc, pek, w1k, w2k, pev, w1v, w2vT)


def _nsa_kernel(qT_ref, gT_ref, kcmp_ref, vcmpT_ref, ks_ref, vsT_ref, kw_ref, vwT_ref, covT_ref, eblk_ref,
                o_ref, qaug_ref, s_ref, acc_ref, m_ref, smax_ref, *, n_slc, n_top):
    i = pl.program_id(1)
    q0 = i * Q_BLOCK
    HQ = NSA_HPG * Q_BLOCK
    GROUPS = range(NSA_KV_HEADS)
    DH = NSA_HEAD_DIM

    def per_head(m):
        return jnp.concatenate([m] * NSA_HPG, axis=1)

    def vrows(g):
        return slice(g * DH, (g + 1) * DH)

    def padded_q(g):
        base = g * NSA_HPG * DH
        qh = jnp.concatenate([qT_ref[base + hp * DH:base + (hp + 1) * DH, :] for hp in range(NSA_HPG)], axis=1)
        zq = jnp.zeros_like(qh)
        return jnp.concatenate([qh, zq] if g == 0 else [zq, qh], axis=0)

    qpad = [padded_q(g) for g in GROUPS]
    t_row = q0 + lax.broadcasted_iota(jnp.int32, (1, Q_BLOCK), 1)

    ncmp = kcmp_ref.shape[1]

    def compressed(rows):
        n_iota = lax.broadcasted_iota(jnp.int32, (rows, Q_BLOCK), 0)
        bias_c = per_head(jnp.where((n_iota * CMP_STRIDE + (CMP_BLOCK - 1)) <= t_row, 0.0, NEG_INF))
        has_key = per_head(t_row) >= CMP_BLOCK - 1
        kc = kcmp_ref[0, 0:rows, :]
        cov = covT_ref[:, 0:rows]
        out = []
        for g in GROUPS:
            sc = jnp.dot(kc, qpad[g], preferred_element_type=f32) + bias_c
            pc = jnp.exp2(sc - jnp.max(sc, axis=0, keepdims=True))
            lc = jnp.sum(pc, axis=0, keepdims=True)
            pc = pc * jnp.where(has_key, 1.0 / lc, 0.0)
            o = jnp.dot(vcmpT_ref[0, vrows(g), 0:rows], pc.astype(bf16), preferred_element_type=f32)
            psum = pc[:, 0:Q_BLOCK]
            for hp in range(1, NSA_HPG):
                psum = psum + pc[:, hp * Q_BLOCK:(hp + 1) * Q_BLOCK]
            hi = psum.astype(bf16)
            r1 = psum - hi.astype(f32)
            mid = r1.astype(bf16)
            lo = (r1 - mid.astype(f32)).astype(bf16)
            out += [o, (jnp.dot(cov, hi, preferred_element_type=f32) + jnp.dot(cov, mid, preferred_element_type=f32)
                        + jnp.dot(cov, lo, preferred_element_type=f32))]
        return tuple(out)

    row_options = [r for r in (ncmp // 4, ncmp // 2, 3 * ncmp // 4, ncmp) if r % 128 == 0]
    need = (Q_BLOCK // CMP_STRIDE) * (i + 1)
    which = sum((need > r).astype(jnp.int32) for r in row_options[:-1])
    cmp_out = lax.switch(which, [functools.partial(compressed, r) for r in row_options])
    oc = [cmp_out[2 * g] for g in GROUPS]
    imp = [cmp_out[2 * g + 1] for g in GROUPS]

    j_i = lax.broadcasted_iota(jnp.int32, (n_slc, Q_BLOCK), 0)
    j_f = j_i.astype(f32)
    cur = t_row >> 6
    forced = (j_i == 0) | (j_i == cur) | (j_i == cur - 1)
    visible = (j_i * SLC_BLOCK) <= t_row
    score = [jnp.where(forced, TAKEN_SCORE, jnp.where(visible, imp[g], NEG_INF)) for g in GROUPS]

    w0 = pl.multiple_of(jnp.maximum(q0 - WINDOW, 0), Q_BLOCK)
    kp = w0 + lax.broadcasted_iota(jnp.int32, (WIN_KEYS, Q_BLOCK), 0)
    bias_w = per_head(jnp.where((kp <= t_row) & (kp > (t_row - WINDOW)), 0.0, NEG_INF))
    kwin = kw_ref[pl.ds(w0, WIN_KEYS), :]
    ow = []
    for g in GROUPS:
        sw = jnp.dot(kwin, qpad[g], preferred_element_type=f32) + bias_w
        pw = jnp.exp2(sw - jnp.max(sw, axis=0, keepdims=True)).astype(bf16)
        vw1 = jnp.concatenate([vwT_ref[vrows(g), pl.ds(w0, WIN_KEYS)], jnp.ones((ONES_ROWS, WIN_KEYS), bf16)],
                              axis=0)
        o = jnp.dot(vw1, pw, preferred_element_type=f32)
        ow.append(o[0:DH, :] * (1.0 / o[DH:DH + 1, :]))

    sel = [jnp.where(forced, 1.0, 0.0) for g in GROUPS]
    for _ in range(n_top - 3):
        for g in GROUPS:
            mx = jnp.max(score[g], axis=0, keepdims=True)
            first = jnp.min(jnp.where(score[g] == mx, j_f, float(n_slc)), axis=0, keepdims=True)
            hit = j_f == first
            sel[g] = jnp.where(hit, 1.0, sel[g])
            score[g] = jnp.where(hit, TAKEN_SCORE, score[g])

    for g in GROUPS:
        qaug_ref[g, 0:2 * DH, :] = qpad[g]
        blk_bias = jnp.where((sel[g] > 0.0) & visible, 0.0, NEG_INF)
        qaug_ref[g, 2 * DH:2 * DH + n_slc, :] = per_head(blk_bias).astype(bf16)
        if n_slc < 128:
            qaug_ref[g, 2 * DH + n_slc:, :] = jnp.zeros((128 - n_slc, HQ), bf16)

    def scores(c):
        k0 = pl.multiple_of(c * SEL_CHUNK, SEL_CHUNK)
        kaug = jnp.concatenate([ks_ref[pl.ds(k0, SEL_CHUNK), :], eblk_ref[pl.ds(k0, SEL_CHUNK), :]], axis=1)
        return [jnp.dot(kaug, qaug_ref[g], preferred_element_type=f32) for g in GROUPS]

    def attend(g, c, s, smax):
        k0 = pl.multiple_of(c * SEL_CHUNK, SEL_CHUNK)
        m = m_ref[g]
        m_new = jnp.maximum(m, smax)
        m_ref[g] = m_new
        p = jnp.exp2(s - m_new).astype(bf16)
        v1 = jnp.concatenate([vsT_ref[vrows(g), pl.ds(k0, SEL_CHUNK)], jnp.ones((ONES_ROWS, SEL_CHUNK), bf16)],
                             axis=0)
        acc_ref[g] = jnp.exp2(m - m_new) * acc_ref[g] + jnp.dot(v1, p, preferred_element_type=f32)

    def step(c, cur, nxt):
        smax = [smax_ref[g] for g in GROUPS]
        s_next = scores(c + 1)
        for g in GROUPS:
            s_ref[g, nxt] = s_next[g]
            smax_ref[g] = jnp.max(s_next[g], axis=0, keepdims=True)
        for g in GROUPS:
            attend(g, c, s_ref[g, cur], smax[g])

    last = q0 // SEL_CHUNK
    odd = last % 2
    s0 = scores(0)
    for g in GROUPS:
        s_ref[g, odd] = s0[g]
        smax_ref[g] = jnp.max(s0[g], axis=0, keepdims=True)
        acc_ref[g] = jnp.zeros((DH + ONES_ROWS, HQ), f32)
        m_ref[g] = jnp.full((1, HQ), NEG_INF, f32)

    @pl.when(odd == 1)
    def _():
        step(0, 1, 0)

    has_pair = (last // 2) % 2

    @pl.when(has_pair == 1)
    def _():
        step(odd, 0, 1)
        step(odd + 1, 1, 0)

    def quad(qq, carry):
        c = odd + 2 * has_pair + 4 * qq
        step(c, 0, 1)
        step(c + 1, 1, 0)
        step(c + 2, 0, 1)
        step(c + 3, 1, 0)
        return carry

    lax.fori_loop(0, last // 4, quad, 0)

    kp_d = last * SEL_CHUNK + lax.broadcasted_iota(jnp.int32, (SEL_CHUNK, Q_BLOCK), 0)
    bias_d = per_head(jnp.where(kp_d <= t_row, 0.0, NEG_INF))
    for g in GROUPS:
        s_d = s_ref[g, 0] + bias_d
        attend(g, last, s_d, jnp.max(s_d, axis=0, keepdims=True))

    gate = _sigmoid(gT_ref[...])
    for g in GROUPS:
        osel = acc_ref[g, 0:DH, :] * (1.0 / acc_ref[g, DH:DH + 1, :])
        for hp in range(NSA_HPG):
            cols = slice(hp * Q_BLOCK, (hp + 1) * Q_BLOCK)
            gr = g * GATE_ROWS + hp
            o = (gate[gr:gr + 1, :] * oc[g][:, cols] + gate[gr + 4:gr + 5, :] * osel[:, cols]
                 + gate[gr + 8:gr + 9, :] * ow[g][:, cols])
            o_ref[(g * NSA_HPG + hp) * DH:(g * NSA_HPG + hp + 1) * DH, :] = o.astype(bf16)


def _nsa(qT, gT, kcmp, vcmpT, ks, vsT, kw, vwT, covT, eblk, B, S):
    n_q = S // Q_BLOCK
    n_slc = S // SLC_BLOCK
    ncmp = S // CMP_STRIDE
    G = NSA_KV_HEADS
    HQ = NSA_HPG * Q_BLOCK
    tok = lambda r: pl.BlockSpec((r, Q_BLOCK), lambda b, i: (0, b * n_q + i))
    per_b_nat = pl.BlockSpec((S, 128), lambda b, i: (b, 0))
    per_b_tr = pl.BlockSpec((128, S), lambda b, i: (0, b))
    return pl.pallas_call(
        functools.partial(_nsa_kernel, n_slc=n_slc, n_top=min(N_SELECT, n_slc)),
        grid=(B, n_q),
        in_specs=[tok(NSA_HEADS * NSA_HEAD_DIM), tok(G * GATE_ROWS),
                  pl.BlockSpec((1, ncmp, 128), lambda b, i: (b, 0, 0)),
                  pl.BlockSpec((1, 128, ncmp), lambda b, i: (b, 0, 0)),
                  per_b_nat, per_b_tr, per_b_nat, per_b_tr,
                  _const_spec((n_slc, ncmp)), _const_spec((S, 128))],
        out_specs=tok(NSA_HEADS * NSA_HEAD_DIM),
        out_shape=jax.ShapeDtypeStruct((NSA_HEADS * NSA_HEAD_DIM, B * S), bf16),
        scratch_shapes=[pltpu.VMEM((G, 2 * NSA_HEAD_DIM + 128, HQ), bf16),
                        pltpu.VMEM((G, 2, SEL_CHUNK, HQ), f32),
                        pltpu.VMEM((G, NSA_HEAD_DIM + ONES_ROWS, HQ), f32),
                        pltpu.VMEM((G, 1, HQ), f32), pltpu.VMEM((G, 1, HQ), f32)],
        compiler_params=_params("parallel", "arbitrary"),
        name="nsa_attn",
    )(qT, gT, kcmp, vcmpT, ks, vsT, kw, vwT, covT, eblk)


def _memkv_kernel(mem_ref, wk_ref, wvT_ref, k_ref, vT_ref):
    m = mem_ref[0].astype(bf16)
    k_ref[0] = jnp.dot(m, wk_ref[...], preferred_element_type=f32).astype(bf16)
    vT_ref[0] = _nt(wvT_ref[...], m).astype(bf16)


def _memkv(mem, wk, wvT):
    B, M, _ = mem.shape
    return pl.pallas_call(
        _memkv_kernel,
        grid=(B,),
        in_specs=[pl.BlockSpec((1, M, D_MODEL), lambda b: (b, 0, 0)),
                  _const_spec((D_MODEL, 512)), _const_spec((512, D_MODEL))],
        out_specs=[pl.BlockSpec((1, M, 512), lambda b: (b, 0, 0)),
                   pl.BlockSpec((1, 512, M), lambda b: (b, 0, 0))],
        out_shape=[jax.ShapeDtypeStruct((B, M, 512), bf16), jax.ShapeDtypeStruct((B, 512, M), bf16)],
        compiler_params=_params("parallel"),
        name="mem_kv",
    )(mem, wk, wvT)


def _memattn_kernel(qT_ref, k_ref, vT_ref, o_ref):
    for h in range(MEM_HEADS):
        rows = slice(h * MEM_HEAD_DIM, (h + 1) * MEM_HEAD_DIM)
        s = jnp.dot(k_ref[0, :, rows], qT_ref[rows, :], preferred_element_type=f32) * (MEM_HEAD_DIM ** -0.5)
        p = jnp.exp(s - jnp.max(s, axis=0, keepdims=True))
        l = jnp.sum(p, axis=0, keepdims=True)
        o = jnp.dot(vT_ref[0, rows, :], p.astype(bf16), preferred_element_type=f32) * (1.0 / l)
        o_ref[rows, :] = o.astype(bf16)


def _memattn(mqT, mk, mvT, B, S, tq):
    M = mk.shape[1]
    nq = S // tq
    return pl.pallas_call(
        _memattn_kernel,
        grid=(B, nq),
        in_specs=[pl.BlockSpec((512, tq), lambda b, i: (0, b * nq + i)),
                  pl.BlockSpec((1, M, 512), lambda b, i: (b, 0, 0)),
                  pl.BlockSpec((1, 512, M), lambda b, i: (b, 0, 0))],
        out_specs=pl.BlockSpec((512, tq), lambda b, i: (0, b * nq + i)),
        out_shape=jax.ShapeDtypeStruct((512, B * S), bf16),
        compiler_params=_params("parallel", "parallel"),
        name="mem_attn",
    )(mqT, mk, mvT)


def _outproj_kernel(h0_ref, ol_ref, onT_ref, omT_ref, wo_ref, g1_ref, b1_ref, h1_ref, *, tm):
    for r0 in range(0, tm, OUT_SUB_ROWS):
        rows = slice(r0, r0 + OUT_SUB_ROWS)
        mixed = jnp.dot(ol_ref[rows, :], wo_ref[0:1024, :], preferred_element_type=f32)
        mixed = mixed + _tn(onT_ref[:, rows], wo_ref[1024:1536, :])
        mixed = mixed + _tn(omT_ref[:, rows], wo_ref[1536:2048, :])
        h1_ref[rows, :] = _ln(ALPHA * h0_ref[rows, :] + mixed, g1_ref[...], b1_ref[...])


def _outproj(h0, ol, onT, omT, wo, g1, b1, tm):
    T = h0.shape[0]
    row = lambda w: pl.BlockSpec((tm, w), lambda i: (i, 0))
    trs = pl.BlockSpec((512, tm), lambda i: (0, i))
    vec = _const_spec((1, D_MODEL))
    return pl.pallas_call(
        functools.partial(_outproj_kernel, tm=tm),
        grid=(T // tm,),
        in_specs=[row(D_MODEL), row(1024), trs, trs, _const_spec((D_MODEL, D_MODEL)), vec, vec],
        out_specs=row(D_MODEL),
        out_shape=jax.ShapeDtypeStruct((T, D_MODEL), f32),
        compiler_params=_params("parallel"),
        name="out_proj_ln1",
    )(h0, ol, onT, omT, wo, g1, b1)


def _ffn_kernel(h_ref, halo_ref, wg_ref, wu_ref, cwg_ref, cwu_ref, cbg_ref, cbu_ref, wd_ref,
                g2_ref, b2_ref, o_ref, lhs_ref, *, tm, blocks_per_seq):
    i = pl.program_id(0)
    j = pl.program_id(1)

    @pl.when(j == 0)
    def _():
        first = (i % blocks_per_seq) == 0
        lhs_ref[0:8, :] = jnp.where(first, 0.0, halo_ref[...]).astype(bf16)
        h = h_ref[...]
        lhs_ref[8:8 + tm, :] = h.astype(bf16)
        o_ref[...] = ALPHA * h

    lhs = lhs_ref[...]

    def conv(w_ref, cw_ref, cb_ref):
        up = jnp.dot(lhs, w_ref[...], preferred_element_type=f32)
        y = cb_ref[...] + pltpu.roll(up, 2, 0)[8:8 + tm, :] * cw_ref[0:1, :]
        y = y + pltpu.roll(up, 1, 0)[8:8 + tm, :] * cw_ref[1:2, :]
        return y + up[8:8 + tm, :] * cw_ref[2:3, :]

    act = (_gelu(conv(wg_ref, cwg_ref, cbg_ref)) * conv(wu_ref, cwu_ref, cbu_ref)).astype(bf16)
    o_ref[...] += jnp.dot(act, wd_ref[...], preferred_element_type=f32)

    @pl.when(j == pl.num_programs(1) - 1)
    def _():
        o_ref[...] = _ln(o_ref[...], g2_ref[...], b2_ref[...])


def _ffn(h1, wup, cw, cb, wd, g2, b2, tm, tf, blocks_per_seq):
    T = h1.shape[0]
    nf = D_FF // tf
    vec = _const_spec((1, D_MODEL))
    return pl.pallas_call(
        functools.partial(_ffn_kernel, tm=tm, blocks_per_seq=blocks_per_seq),
        grid=(T // tm, nf),
        in_specs=[pl.BlockSpec((tm, D_MODEL), lambda i, j: (i, 0)),
                  pl.BlockSpec((8, D_MODEL), lambda i, j: (jnp.maximum(i * (tm // 8) - 1, 0), 0)),
                  pl.BlockSpec((D_MODEL, tf), lambda i, j: (0, j)),
                  pl.BlockSpec((D_MODEL, tf), lambda i, j: (0, nf + j)),
                  pl.BlockSpec((FFN_CONV_WIDTH, tf), lambda i, j: (0, j)),
                  pl.BlockSpec((FFN_CONV_WIDTH, tf), lambda i, j: (0, nf + j)),
                  pl.BlockSpec((1, tf), lambda i, j: (0, j)),
                  pl.BlockSpec((1, tf), lambda i, j: (0, nf + j)),
                  pl.BlockSpec((tf, D_MODEL), lambda i, j: (j, 0)),
                  vec, vec],
        out_specs=pl.BlockSpec((tm, D_MODEL), lambda i, j: (i, 0)),
        out_shape=jax.ShapeDtypeStruct((T, D_MODEL), f32),
        scratch_shapes=[pltpu.VMEM((tm + 8, D_MODEL), bf16)],
        compiler_params=_params("parallel", "arbitrary", vmem_limit=V7X_VMEM_LIMIT_FFN_BYTES),
        name="conv_ffn_ln2",
    )(h1, h1, wup, wup, cw, cw, cb, cb, wd, g2, b2)


def _layer(x, mem, ln_in_g, ln_in_b, w_in, lru_conv_w, lru_conv_b, lru_wa, lru_ba, lru_wx, lru_bx,
           lru_lam, cmp_pe_k, cmp_w1_k, cmp_w2_k, cmp_pe_v, cmp_w1_v, cmp_w2_v, w_mem_kv, w_out,
           ln1_g, ln1_b, ffn_w_up, ffn_conv_w, ffn_conv_b, ffn_w_down, ln2_g, ln2_b,
           *, tm_proj, tb_lru, tq_mem, tm_out, tm_ffn, tf_ffn):
    B, S, _ = x.shape
    T = B * S
    G, Dh = NSA_KV_HEADS, NSA_HEAD_DIM
    row = lambda v: v.reshape(1, -1)

    c_q = 2 * LRU_WIDTH
    c_kv = c_q + NSA_HEADS * Dh
    c_gate = c_kv + 6 * G * Dh
    c_mq = c_gate + 3 * NSA_HEADS
    assert (c_q, c_kv, c_gate) == (_NQ0, _NKC0, _NCOLS)
    wn = w_in[:, :c_gate].astype(bf16)
    wmq = w_in[:, c_mq:].astype(bf16)
    w_gate = w_in[:, c_gate:c_mq].reshape(D_MODEL, G, NSA_HPG, 3)
    w_gate = jnp.pad(w_gate.transpose(1, 3, 2, 0), ((0, 0), (0, 1), (0, 0), (0, 0)))
    wgT = w_gate.reshape(G * GATE_ROWS, D_MODEL).astype(bf16)

    wax = jnp.concatenate([lru_wa, lru_wx], axis=-1).astype(bf16)

    def cmp_weights(pe, w1, w2):
        pe2 = jnp.broadcast_to(pe.reshape(2, CMP_STRIDE, 1, Dh), (2, CMP_STRIDE, G, Dh)).reshape(2, -1)
        def group_diag(w, axis):
            z = jnp.zeros_like(w)
            return jnp.stack([jnp.concatenate([w if k == g else z for k in range(G)], axis=-1)
                              for g in range(G)], axis=axis)

        w1e = group_diag(w1.reshape(2, CMP_STRIDE, Dh, Dh), 2)
        w1e = w1e.reshape(2, CMP_STRIDE * G * Dh, G * Dh).astype(bf16)
        w2e = group_diag(w2, 0).reshape(G * Dh, G * Dh)
        return pe2, w1e, w2e

    pek, w1k, w2k = cmp_weights(cmp_pe_k, cmp_w1_k, cmp_w2_k)
    pev, w1v, w2v = cmp_weights(cmp_pe_v, cmp_w1_v, cmp_w2_v)

    n_slc, ncmp = S // SLC_BLOCK, S // CMP_STRIDE
    ci = jnp.arange(ncmp)[None, :] * CMP_STRIDE
    sj = jnp.arange(n_slc)[:, None] * SLC_BLOCK
    covT = ((ci <= sj + SLC_BLOCK - 1) & (ci + CMP_BLOCK - 1 >= sj)).astype(bf16)
    eblk = (jnp.arange(S)[:, None] // SLC_BLOCK == jnp.arange(128)[None, :]).astype(bf16)

    wk_mem = w_mem_kv[:, :512].astype(bf16)
    wvT_mem = w_mem_kv[:, 512:].T.astype(bf16)
    wo = w_out.astype(bf16)
    wup = ffn_w_up.astype(bf16)
    wd = ffn_w_down.astype(bf16)

    x2 = x.reshape(T, D_MODEL)
    lx, ly, kc, vc, ks, kw, qT, vsT, vwT, mqT, gT, h0 = _proj(x2, row(ln_in_g), row(ln_in_b), wn, wmq, wgT, tm_proj)

    o_lru = _lru(lx.reshape(B, S, LRU_WIDTH), ly.reshape(B, S, LRU_WIDTH), lru_conv_w, row(lru_conv_b),
                 wax, row(lru_ba), row(lru_bx), row(lru_lam), tb_lru)

    sub = lambda a: a.reshape(B, ncmp, CMP_STRIDE * G * Dh)
    kcmp, vcmpT = _compress(sub(kc), sub(vc), pek, w1k, w2k.astype(bf16), pev, w1v, w2v.T.astype(bf16))
    o_nsaT = _nsa(qT, gT, kcmp, vcmpT, ks, vsT, kw, vwT, covT, eblk, B, S)

    mk, mvT = _memkv(mem, wk_mem, wvT_mem)
    o_memT = _memattn(mqT, mk, mvT, B, S, tq_mem)

    h1 = _outproj(h0, o_lru.reshape(T, LRU_WIDTH), o_nsaT, o_memT, wo, row(ln1_g), row(ln1_b), tm_out)
    out = _ffn(h1, wup, ffn_conv_w, row(ffn_conv_b), wd, row(ln2_g), row(ln2_b), tm_ffn, tf_ffn, S // tm_ffn)
    return out.reshape(B, S, D_MODEL)


def kernel(x, mem, ln_in_g, ln_in_b, w_in, lru_conv_w, lru_conv_b, lru_wa, lru_ba, lru_wx, lru_bx,
           lru_lam, cmp_pe_k, cmp_w1_k, cmp_w2_k, cmp_pe_v, cmp_w1_v, cmp_w2_v, w_mem_kv, w_out,
           ln1_g, ln1_b, ffn_w_up, ffn_conv_w, ffn_conv_b, ffn_w_down, ln2_g, ln2_b):
    return _layer(x, mem, ln_in_g, ln_in_b, w_in[0], lru_conv_w[0], lru_conv_b[0], lru_wa[0], lru_ba[0],
                  lru_wx[0], lru_bx[0], lru_lam[0], cmp_pe_k[0], cmp_w1_k[0], cmp_w2_k[0], cmp_pe_v[0],
                  cmp_w1_v[0], cmp_w2_v[0], w_mem_kv[0], w_out[0], ln1_g[0], ln1_b[0], ffn_w_up[0],
                  ffn_conv_w[0], ffn_conv_b[0], ffn_w_down[0], ln2_g[0], ln2_b[0],
                  tm_proj=512, tb_lru=256, tq_mem=512, tm_out=1024, tm_ffn=1024, tf_ffn=512)
```

```python
import functools
import math

import jax
import jax.numpy as jnp
from jax import lax
from jax.experimental import pallas as pl
from jax.experimental.pallas import tpu as pltpu

f32 = jnp.float32
bf16 = jnp.bfloat16

D_MODEL = 2048
LRU_WIDTH = 1024
LRU_BLOCKS = 8
LRU_BLOCK_DIM = 128
LRU_CONV_WIDTH = 4
LRU_C = 8.0
LRU_TINY = 1e-30
NSA_HEADS = 8
NSA_KV_HEADS = 2
NSA_HPG = NSA_HEADS // NSA_KV_HEADS
NSA_HEAD_DIM = 64
NSA_Q_SCALE = NSA_HEAD_DIM ** -0.5 * math.log2(math.e)
CMP_STRIDE = 16
CMP_BLOCK = 32
SLC_BLOCK = 64
N_SELECT = 16
WINDOW = 512
Q_BLOCK = 128
MEM_HEADS = 4
MEM_HEAD_DIM = 128
D_FF = 5632
FFN_CONV_WIDTH = 3
LN_EPS = 1e-5
NEG_INF = -1e30
FORCE_SCORE = 1e9
TAKEN_SCORE = -3e38
ALPHA = 2.0 ** 0.25

V7X_VMEM_LIMIT_BYTES = 56 * 1024 * 1024
V7X_VMEM_LIMIT_FFN_BYTES = 61 * 1024 * 1024

_NX0, _NY0, _NQ0, _NKC0, _NKS0, _NCOLS = 0, 1024, 2048, 2560, 2816, 3328

SEL_CHUNK = 512
WIN_KEYS = WINDOW + Q_BLOCK
GATE_ROWS = 16
OUT_SUB_ROWS = 512
ONES_ROWS = 16


def _ln(x, g, b):
    mu = jnp.mean(x, axis=-1, keepdims=True)
    xc = x - mu
    var = jnp.mean(xc * xc, axis=-1, keepdims=True)
    return xc * lax.rsqrt(var + LN_EPS) * g + b


def _gelu(x):
    return jax.nn.gelu(x)


def _sigmoid(x):
    return 1.0 / (1.0 + jnp.exp(-x))


def _sigmoid_tanh(x):
    return 0.5 * jnp.tanh(0.5 * x) + 0.5


def _nt(a, b):
    return lax.dot_general(a, b, (((1,), (1,)), ((), ())), preferred_element_type=f32)


def _tn(a, b):
    return lax.dot_general(a, b, (((0,), (0,)), ((), ())), preferred_element_type=f32)


def _params(*sem, vmem_limit=V7X_VMEM_LIMIT_BYTES):
    return pltpu.CompilerParams(dimension_semantics=sem, vmem_limit_bytes=vmem_limit)


def _const_spec(shape):
    nd = len(shape)
    return pl.BlockSpec(shape, lambda *_: (0,) * nd, pipeline_mode=pl.Buffered(1))


def _proj_kernel(x_ref, g_ref, b_ref, wn_ref, wmq_ref, wgT_ref,
                 lx_ref, ly_ref, kc_ref, vc_ref, ks_ref, kw_ref,
                 qT_ref, vsT_ref, vwT_ref, mqT_ref, gT_ref, h0_ref):
    h0 = _ln(x_ref[...], g_ref[...], b_ref[...])
    h0_ref[...] = h0
    h = h0.astype(bf16)

    def nat(c0, c1):
        return jnp.dot(h, wn_ref[:, c0:c1], preferred_element_type=f32)

    lx_ref[...] = nat(_NX0, _NY0)
    ly_ref[...] = nat(_NY0, _NQ0)
    qT_ref[...] = (nat(_NQ0, _NKC0) * NSA_Q_SCALE).T.astype(bf16)
    kvc = nat(_NKC0, _NKS0)
    kc_ref[...] = kvc[:, 0:128]
    vc_ref[...] = kvc[:, 128:256]
    kvsw = nat(_NKS0, _NCOLS)
    ks_ref[...] = kvsw[:, 0:128].astype(bf16)
    vsT_ref[...] = kvsw[:, 128:256].T.astype(bf16)
    kw_ref[...] = kvsw[:, 256:384].astype(bf16)
    vwT_ref[...] = kvsw[:, 384:512].T.astype(bf16)
    mqT_ref[...] = jnp.dot(h, wmq_ref[...], preferred_element_type=f32).T.astype(bf16)
    gT_ref[...] = _nt(wgT_ref[...], h)


def _proj(x2, g, b, wn, wmq, wgT, tm):
    T = x2.shape[0]
    nat = lambda w: pl.BlockSpec((tm, w), lambda i: (i, 0))
    trs = lambda r: pl.BlockSpec((r, tm), lambda i: (0, i))
    return pl.pallas_call(
        _proj_kernel,
        grid=(T // tm,),
        in_specs=[pl.BlockSpec((tm, D_MODEL), lambda i: (i, 0)),
                  _const_spec((1, D_MODEL)), _const_spec((1, D_MODEL)),
                  _const_spec(wn.shape), _const_spec((D_MODEL, MEM_HEADS * MEM_HEAD_DIM)),
                  _const_spec((NSA_KV_HEADS * GATE_ROWS, D_MODEL))],
        out_specs=[nat(1024), nat(1024), nat(128), nat(128), nat(128), nat(128),
                   trs(512), trs(128), trs(128), trs(512), trs(32), nat(D_MODEL)],
        out_shape=[jax.ShapeDtypeStruct((T, 1024), f32), jax.ShapeDtypeStruct((T, 1024), f32),
                   jax.ShapeDtypeStruct((T, 128), f32), jax.ShapeDtypeStruct((T, 128), f32),
                   jax.ShapeDtypeStruct((T, 128), bf16), jax.ShapeDtypeStruct((T, 128), bf16),
                   jax.ShapeDtypeStruct((512, T), bf16), jax.ShapeDtypeStruct((128, T), bf16),
                   jax.ShapeDtypeStruct((128, T), bf16), jax.ShapeDtypeStruct((512, T), bf16),
                   jax.ShapeDtypeStruct((32, T), f32), jax.ShapeDtypeStruct((T, D_MODEL), f32)],
        compiler_params=_params("parallel"),
        name="proj",
    )(x2, g, b, wn, wmq, wgT)


def _lru_pitch(tb):
    seg = tb // 8
    return seg + 8 if (seg // 8) % 2 == 0 else seg


def _lru_kernel(x_ref, y_ref, cw_ref, cb_ref, wax_ref, ba_ref, bx_ref, lam_ref, o_ref,
                xext_ref, a_ref, u_ref, carry_ref, *, tb):
    s = pl.program_id(1)

    @pl.when(s == 0)
    def _():
        xext_ref[0:8, :] = jnp.zeros((8, LRU_WIDTH), f32)
        carry_ref[...] = jnp.zeros((1, LRU_WIDTH), f32)

    x = x_ref[0]
    xext_ref[8:8 + tb, :] = x
    xe = xext_ref[...]
    xc = cb_ref[...]
    for k in range(LRU_CONV_WIDTH):
        back = LRU_CONV_WIDTH - 1 - k
        shifted = pltpu.roll(xe, back, 0)[8:8 + tb, :] if back else x
        xc = xc + shifted * cw_ref[k:k + 1, :]
    xext_ref[0:8, :] = x[tb - 8:tb, :]

    xb = xc.astype(bf16)
    gates = [jnp.dot(xb[:, n * 128:(n + 1) * 128], wax_ref[n], preferred_element_type=f32)
             for n in range(LRU_BLOCKS)]
    r = _sigmoid_tanh(jnp.concatenate([gt[:, 0:128] for gt in gates], axis=1) + ba_ref[...])
    i = _sigmoid_tanh(jnp.concatenate([gt[:, 128:256] for gt in gates], axis=1) + bx_ref[...])
    nl = -lam_ref[...]
    softplus = jnp.maximum(nl, 0.0) + jnp.log1p(jnp.exp(-jnp.abs(nl)))
    log_a = (-LRU_C) * r * softplus
    a = jnp.exp(log_a)
    th = jnp.tanh(log_a)
    z = -2.0 * th / (1.0 - th)
    u = (z * lax.rsqrt(jnp.maximum(z, LRU_TINY))) * (i * xc)
    seg = tb // 8
    pitch = _lru_pitch(tb)
    for n in range(LRU_BLOCKS):
        for r8 in range(8):
            a_ref[n, r8 * pitch:r8 * pitch + seg, :] = a[r8 * seg:(r8 + 1) * seg, n * 128:(n + 1) * 128]
            u_ref[n, r8 * pitch:r8 * pitch + seg, :] = u[r8 * seg:(r8 + 1) * seg, n * 128:(n + 1) * 128]

    def sweep(k, carry):
        rows = pl.ds(k, 8, stride=pitch)
        out = []
        for n in range(LRU_BLOCKS):
            h, p = carry[n]
            ak = a_ref[n, rows, :]
            h = ak * h + u_ref[n, rows, :]
            p = ak * p
            u_ref[n, rows, :] = h
            a_ref[n, rows, :] = p
            out.append((h, p))
        return tuple(out)

    init = tuple((jnp.zeros((8, 128), f32), jnp.ones((8, 128), f32)) for _ in range(LRU_BLOCKS))
    ends = lax.fori_loop(0, seg, sweep, init, unroll=4)
    h_end = jnp.concatenate([e[0] for e in ends], axis=1)
    p_end = jnp.concatenate([e[1] for e in ends], axis=1)

    c = carry_ref[...]
    for r8 in range(8):
        rows = slice(r8 * seg, (r8 + 1) * seg)
        prow = slice(r8 * pitch, r8 * pitch + seg)
        h_loc = jnp.concatenate([u_ref[n, prow, :] for n in range(LRU_BLOCKS)], axis=1)
        p_cum = jnp.concatenate([a_ref[n, prow, :] for n in range(LRU_BLOCKS)], axis=1)
        o_ref[0, rows, :] = (_gelu(y_ref[0, rows, :]) * (h_loc + p_cum * c)).astype(bf16)
        c = h_end[r8:r8 + 1, :] + p_end[r8:r8 + 1, :] * c
    carry_ref[...] = c


def _lru(lx, ly, cw, cb, wax, ba, bx, lam, tb):
    B, S, _ = lx.shape
    blk = pl.BlockSpec((1, tb, LRU_WIDTH), lambda b, s: (b, s, 0))
    return pl.pallas_call(
        functools.partial(_lru_kernel, tb=tb),
        grid=(B, S // tb),
        in_specs=[blk, blk, _const_spec((LRU_CONV_WIDTH, LRU_WIDTH)), _const_spec((1, LRU_WIDTH)),
                  _const_spec((LRU_BLOCKS, 128, 256)), _const_spec((1, LRU_WIDTH)),
                  _const_spec((1, LRU_WIDTH)), _const_spec((1, LRU_WIDTH))],
        out_specs=blk,
        out_shape=jax.ShapeDtypeStruct((B, S, LRU_WIDTH), bf16),
        scratch_shapes=[pltpu.VMEM((tb + 8, LRU_WIDTH), f32),
                        pltpu.VMEM((LRU_BLOCKS, 8 * _lru_pitch(tb), 128), f32),
                        pltpu.VMEM((LRU_BLOCKS, 8 * _lru_pitch(tb), 128), f32),
                        pltpu.VMEM((1, LRU_WIDTH), f32)],
        compiler_params=_params("parallel", "arbitrary"),
        name="lru",
    )(lx, ly, cw, cb, wax, ba, bx, lam)


def _cmp_kernel(kc_ref, vc_ref, pek_ref, w1k_ref, w2k_ref, pev_ref, w1v_ref, w2vT_ref,
                kcmp_ref, vcmpT_ref):
    nsub = kc_ref.shape[0] // CMP_STRIDE

    def hidden(x_ref, pe_ref, w1_ref):
        y = [None, None]
        for l in range(CMP_STRIDE):
            xl = x_ref[pl.ds(l, nsub, stride=CMP_STRIDE), :]
            cols = slice(l * 128, (l + 1) * 128)
            for half in range(2):
                t = jnp.dot((xl + pe_ref[half:half + 1, cols]).astype(bf16), w1_ref[half, cols, :],
                            preferred_element_type=f32)
                y[half] = t if y[half] is None else y[half] + t
        return _gelu(y[0] + pltpu.roll(y[1], nsub - 1, 0)).astype(bf16)

    kcmp_ref[0] = jnp.dot(hidden(kc_ref, pek_ref, w1k_ref), w2k_ref[...],
                          preferred_element_type=f32).astype(bf16)
    vcmpT_ref[0] = _nt(w2vT_ref[...], hidden(vc_ref, pev_ref, w1v_ref)).astype(bf16)


def _compress(kc, vc, pek, w1k, w2k, pev, w1v, w2vT, B, S):
    nsub = S // CMP_STRIDE
    width = pek.shape[1]
    blk = pl.BlockSpec((S, 128), lambda b: (b, 0))
    return pl.pallas_call(
        _cmp_kernel,
        grid=(B,),
        in_specs=[blk, blk, _const_spec((2, width)), _const_spec((2, width, 128)), _const_spec((128, 128)),
                  _const_spec((2, width)), _const_spec((2, width, 128)), _const_spec((128, 128))],
        out_specs=[pl.BlockSpec((1, nsub, 128), lambda b: (b, 0, 0)),
                   pl.BlockSpec((1, 128, nsub), lambda b: (b, 0, 0))],
        out_shape=[jax.ShapeDtypeStruct((B, nsub, 128), bf16), jax.ShapeDtypeStruct((B, 128, nsub), bf16)],
        compiler_params=_params("parallel"),
        name="nsa_compress",
    )(kc, vc, pek, w1k, w2k, pev, w1v, w2vT)


def _nsa_kernel(qT_ref, gT_ref, kcmp_ref, vcmpT_ref, ks_ref, vsT_ref, kw_ref, vwT_ref, covT_ref, eblk_ref,
                o_ref, qaug_ref, s_ref, acc_ref, m_ref, smax_ref, *, n_slc, n_top):
    i = pl.program_id(1)
    q0 = i * Q_BLOCK
    HQ = NSA_HPG * Q_BLOCK
    GROUPS = range(NSA_KV_HEADS)
    DH = NSA_HEAD_DIM

    def per_head(m):
        return jnp.concatenate([m] * NSA_HPG, axis=1)

    def vrows(g):
        return slice(g * DH, (g + 1) * DH)

    def padded_q(g):
        base = g * NSA_HPG * DH
        qh = jnp.concatenate([qT_ref[base + hp * DH:base + (hp + 1) * DH, :] for hp in range(NSA_HPG)], axis=1)
        zq = jnp.zeros_like(qh)
        return jnp.concatenate([qh, zq] if g == 0 else [zq, qh], axis=0)

    qpad = [padded_q(g) for g in GROUPS]
    t_row = q0 + lax.broadcasted_iota(jnp.int32, (1, Q_BLOCK), 1)

    ncmp = kcmp_ref.shape[1]

    def compressed(rows):
        n_iota = lax.broadcasted_iota(jnp.int32, (rows, Q_BLOCK), 0)
        bias_c = per_head(jnp.where((n_iota * CMP_STRIDE + (CMP_BLOCK - 1)) <= t_row, 0.0, NEG_INF))
        has_key = per_head(t_row) >= CMP_BLOCK - 1
        kc = kcmp_ref[0, 0:rows, :]
        cov = covT_ref[:, 0:rows]
        out = []
        for g in GROUPS:
            sc = jnp.dot(kc, qpad[g], preferred_element_type=f32) + bias_c
            pc = jnp.exp2(sc - jnp.max(sc, axis=0, keepdims=True))
            lc = jnp.sum(pc, axis=0, keepdims=True)
            pc = pc * jnp.where(has_key, 1.0 / lc, 0.0)
            o = jnp.dot(vcmpT_ref[0, vrows(g), 0:rows], pc.astype(bf16), preferred_element_type=f32)
            psum = pc[:, 0:Q_BLOCK]
            for hp in range(1, NSA_HPG):
                psum = psum + pc[:, hp * Q_BLOCK:(hp + 1) * Q_BLOCK]
            hi = psum.astype(bf16)
            r1 = psum - hi.astype(f32)
            mid = r1.astype(bf16)
            lo = (r1 - mid.astype(f32)).astype(bf16)
            out += [o, (jnp.dot(cov, hi, preferred_element_type=f32) + jnp.dot(cov, mid, preferred_element_type=f32)
                        + jnp.dot(cov, lo, preferred_element_type=f32))]
        return tuple(out)

    row_options = [r for r in (ncmp // 4, ncmp // 2, 3 * ncmp // 4, ncmp) if r % 128 == 0]
    need = (Q_BLOCK // CMP_STRIDE) * (i + 1)
    which = sum((need > r).astype(jnp.int32) for r in row_options[:-1])
    cmp_out = lax.switch(which, [functools.partial(compressed, r) for r in row_options])
    oc = [cmp_out[2 * g] for g in GROUPS]
    imp = [cmp_out[2 * g + 1] for g in GROUPS]

    j_i = lax.broadcasted_iota(jnp.int32, (n_slc, Q_BLOCK), 0)
    j_f = j_i.astype(f32)
    cur = t_row >> 6
    forced = (j_i == 0) | (j_i == cur) | (j_i == cur - 1)
    visible = (j_i * SLC_BLOCK) <= t_row
    score = [jnp.where(forced, TAKEN_SCORE, jnp.where(visible, imp[g], NEG_INF)) for g in GROUPS]

    w0 = pl.multiple_of(jnp.maximum(q0 - WINDOW, 0), Q_BLOCK)
    kp = w0 + lax.broadcasted_iota(jnp.int32, (WIN_KEYS, Q_BLOCK), 0)
    bias_w = per_head(jnp.where((kp <= t_row) & (kp > (t_row - WINDOW)), 0.0, NEG_INF))
    kwin = kw_ref[pl.ds(w0, WIN_KEYS), :]
    ow = []
    for g in GROUPS:
        sw = jnp.dot(kwin, qpad[g], preferred_element_type=f32) + bias_w
        pw = jnp.exp2(sw - jnp.max(sw, axis=0, keepdims=True)).astype(bf16)
        vw1 = jnp.concatenate([vwT_ref[vrows(g), pl.ds(w0, WIN_KEYS)], jnp.ones((ONES_ROWS, WIN_KEYS), bf16)],
                              axis=0)
        o = jnp.dot(vw1, pw, preferred_element_type=f32)
        ow.append(o[0:DH, :] * (1.0 / o[DH:DH + 1, :]))

    sel = [jnp.where(forced, 1.0, 0.0) for g in GROUPS]
    for _ in range(n_top - 3):
        for g in GROUPS:
            mx = jnp.max(score[g], axis=0, keepdims=True)
            first = jnp.min(jnp.where(score[g] == mx, j_f, float(n_slc)), axis=0, keepdims=True)
            hit = j_f == first
            sel[g] = jnp.where(hit, 1.0, sel[g])
            score[g] = jnp.where(hit, TAKEN_SCORE, score[g])

    for g in GROUPS:
        qaug_ref[g, 0:2 * DH, :] = qpad[g]
        blk_bias = jnp.where((sel[g] > 0.0) & visible, 0.0, NEG_INF)
        qaug_ref[g, 2 * DH:2 * DH + n_slc, :] = per_head(blk_bias).astype(bf16)
        if n_slc < 128:
            qaug_ref[g, 2 * DH + n_slc:, :] = jnp.zeros((128 - n_slc, HQ), bf16)

    def scores(c):
        k0 = pl.multiple_of(c * SEL_CHUNK, SEL_CHUNK)
        kaug = jnp.concatenate([ks_ref[pl.ds(k0, SEL_CHUNK), :], eblk_ref[pl.ds(k0, SEL_CHUNK), :]], axis=1)
        return [jnp.dot(kaug, qaug_ref[g], preferred_element_type=f32) for g in GROUPS]

    def attend(g, c, s, smax):
        k0 = pl.multiple_of(c * SEL_CHUNK, SEL_CHUNK)
        m = m_ref[g]
        m_new = jnp.maximum(m, smax)
        m_ref[g] = m_new
        p = jnp.exp2(s - m_new).astype(bf16)
        v1 = jnp.concatenate([vsT_ref[vrows(g), pl.ds(k0, SEL_CHUNK)], jnp.ones((ONES_ROWS, SEL_CHUNK), bf16)],
                             axis=0)
        acc_ref[g] = jnp.exp2(m - m_new) * acc_ref[g] + jnp.dot(v1, p, preferred_element_type=f32)

    def step(c, cur, nxt):
        smax = [smax_ref[g] for g in GROUPS]
        s_next = scores(c + 1)
        for g in GROUPS:
            s_ref[g, nxt] = s_next[g]
            smax_ref[g] = jnp.max(s_next[g], axis=0, keepdims=True)
        for g in GROUPS:
            attend(g, c, s_ref[g, cur], smax[g])

    last = q0 // SEL_CHUNK
    odd = last % 2
    s0 = scores(0)
    for g in GROUPS:
        s_ref[g, odd] = s0[g]
        smax_ref[g] = jnp.max(s0[g], axis=0, keepdims=True)
        acc_ref[g] = jnp.zeros((DH + ONES_ROWS, HQ), f32)
        m_ref[g] = jnp.full((1, HQ), NEG_INF, f32)

    @pl.when(odd == 1)
    def _():
        step(0, 1, 0)

    has_pair = (last // 2) % 2

    @pl.when(has_pair == 1)
    def _():
        step(odd, 0, 1)
        step(odd + 1, 1, 0)

    def quad(qq, carry):
        c = odd + 2 * has_pair + 4 * qq
        step(c, 0, 1)
        step(c + 1, 1, 0)
        step(c + 2, 0, 1)
        step(c + 3, 1, 0)
        return carry

    lax.fori_loop(0, last // 4, quad, 0)

    kp_d = last * SEL_CHUNK + lax.broadcasted_iota(jnp.int32, (SEL_CHUNK, Q_BLOCK), 0)
    bias_d = per_head(jnp.where(kp_d <= t_row, 0.0, NEG_INF))
    for g in GROUPS:
        s_d = s_ref[g, 0] + bias_d
        attend(g, last, s_d, jnp.max(s_d, axis=0, keepdims=True))

    gate = _sigmoid(gT_ref[...])
    for g in GROUPS:
        osel = acc_ref[g, 0:DH, :] * (1.0 / acc_ref[g, DH:DH + 1, :])
        for hp in range(NSA_HPG):
            cols = slice(hp * Q_BLOCK, (hp + 1) * Q_BLOCK)
            gr = g * GATE_ROWS + hp
            o = (gate[gr:gr + 1, :] * oc[g][:, cols] + gate[gr + 4:gr + 5, :] * osel[:, cols]
                 + gate[gr + 8:gr + 9, :] * ow[g][:, cols])
            o_ref[(g * NSA_HPG + hp) * DH:(g * NSA_HPG + hp + 1) * DH, :] = o.astype(bf16)


def _nsa(qT, gT, kcmp, vcmpT, ks, vsT, kw, vwT, covT, eblk, B, S):
    n_q = S // Q_BLOCK
    n_slc = S // SLC_BLOCK
    ncmp = S // CMP_STRIDE
    G = NSA_KV_HEADS
    HQ = NSA_HPG * Q_BLOCK
    tok = lambda r: pl.BlockSpec((r, Q_BLOCK), lambda b, i: (0, b * n_q + i))
    per_b_nat = pl.BlockSpec((S, 128), lambda b, i: (b, 0))
    per_b_tr = pl.BlockSpec((128, S), lambda b, i: (0, b))
    return pl.pallas_call(
        functools.partial(_nsa_kernel, n_slc=n_slc, n_top=min(N_SELECT, n_slc)),
        grid=(B, n_q),
        in_specs=[tok(NSA_HEADS * NSA_HEAD_DIM), tok(G * GATE_ROWS),
                  pl.BlockSpec((1, ncmp, 128), lambda b, i: (b, 0, 0)),
                  pl.BlockSpec((1, 128, ncmp), lambda b, i: (b, 0, 0)),
                  per_b_nat, per_b_tr, per_b_nat, per_b_tr,
                  _const_spec((n_slc, ncmp)), _const_spec((S, 128))],
        out_specs=tok(NSA_HEADS * NSA_HEAD_DIM),
        out_shape=jax.ShapeDtypeStruct((NSA_HEADS * NSA_HEAD_DIM, B * S), bf16),
        scratch_shapes=[pltpu.VMEM((G, 2 * NSA_HEAD_DIM + 128, HQ), bf16),
                        pltpu.VMEM((G, 2, SEL_CHUNK, HQ), f32),
                        pltpu.VMEM((G, NSA_HEAD_DIM + ONES_ROWS, HQ), f32),
                        pltpu.VMEM((G, 1, HQ), f32), pltpu.VMEM((G, 1, HQ), f32)],
        compiler_params=_params("parallel", "arbitrary"),
        name="nsa_attn",
    )(qT, gT, kcmp, vcmpT, ks, vsT, kw, vwT, covT, eblk)


def _memkv_kernel(mem_ref, wk_ref, wvT_ref, k_ref, vT_ref):
    m = mem_ref[0].astype(bf16)
    k_ref[0] = jnp.dot(m, wk_ref[...], preferred_element_type=f32).astype(bf16)
    vT_ref[0] = _nt(wvT_ref[...], m).astype(bf16)


def _memkv(mem, wk, wvT):
    B, M, _ = mem.shape
    return pl.pallas_call(
        _memkv_kernel,
        grid=(B,),
        in_specs=[pl.BlockSpec((1, M, D_MODEL), lambda b: (b, 0, 0)),
                  _const_spec((D_MODEL, 512)), _const_spec((512, D_MODEL))],
        out_specs=[pl.BlockSpec((1, M, 512), lambda b: (b, 0, 0)),
                   pl.BlockSpec((1, 512, M), lambda b: (b, 0, 0))],
        out_shape=[jax.ShapeDtypeStruct((B, M, 512), bf16), jax.ShapeDtypeStruct((B, 512, M), bf16)],
        compiler_params=_params("parallel"),
        name="mem_kv",
    )(mem, wk, wvT)


def _memattn_kernel(qT_ref, k_ref, vT_ref, o_ref):
    for h in range(MEM_HEADS):
        rows = slice(h * MEM_HEAD_DIM, (h + 1) * MEM_HEAD_DIM)
        s = jnp.dot(k_ref[0, :, rows], qT_ref[rows, :], preferred_element_type=f32) * (MEM_HEAD_DIM ** -0.5)
        p = jnp.exp(s - jnp.max(s, axis=0, keepdims=True))
        l = jnp.sum(p, axis=0, keepdims=True)
        o = jnp.dot(vT_ref[0, rows, :], p.astype(bf16), preferred_element_type=f32) * (1.0 / l)
        o_ref[rows, :] = o.astype(bf16)


def _memattn(mqT, mk, mvT, B, S, tq):
    M = mk.shape[1]
    nq = S // tq
    return pl.pallas_call(
        _memattn_kernel,
        grid=(B, nq),
        in_specs=[pl.BlockSpec((512, tq), lambda b, i: (0, b * nq + i)),
                  pl.BlockSpec((1, M, 512), lambda b, i: (b, 0, 0)),
                  pl.BlockSpec((1, 512, M), lambda b, i: (b, 0, 0))],
        out_specs=pl.BlockSpec((512, tq), lambda b, i: (0, b * nq + i)),
        out_shape=jax.ShapeDtypeStruct((512, B * S), bf16),
        compiler_params=_params("parallel", "parallel"),
        name="mem_attn",
    )(mqT, mk, mvT)


def _outproj_kernel(h0_ref, ol_ref, onT_ref, omT_ref, wo_ref, g1_ref, b1_ref, h1_ref, *, tm):
    for r0 in range(0, tm, OUT_SUB_ROWS):
        rows = slice(r0, r0 + OUT_SUB_ROWS)
        mixed = jnp.dot(ol_ref[rows, :], wo_ref[0:1024, :], preferred_element_type=f32)
        mixed = mixed + _tn(onT_ref[:, rows], wo_ref[1024:1536, :])
        mixed = mixed + _tn(omT_ref[:, rows], wo_ref[1536:2048, :])
        h1_ref[rows, :] = _ln(ALPHA * h0_ref[rows, :] + mixed, g1_ref[...], b1_ref[...])


def _outproj(h0, ol, onT, omT, wo, g1, b1, tm):
    T = h0.shape[0]
    row = lambda w: pl.BlockSpec((tm, w), lambda i: (i, 0))
    trs = pl.BlockSpec((512, tm), lambda i: (0, i))
    vec = _const_spec((1, D_MODEL))
    return pl.pallas_call(
        functools.partial(_outproj_kernel, tm=tm),
        grid=(T // tm,),
        in_specs=[row(D_MODEL), row(1024), trs, trs, _const_spec((D_MODEL, D_MODEL)), vec, vec],
        out_specs=row(D_MODEL),
        out_shape=jax.ShapeDtypeStruct((T, D_MODEL), f32),
        compiler_params=_params("parallel"),
        name="out_proj_ln1",
    )(h0, ol, onT, omT, wo, g1, b1)


def _ffn_kernel(h_ref, halo_ref, wg_ref, wu_ref, cwg_ref, cwu_ref, cbg_ref, cbu_ref, wd_ref,
                g2_ref, b2_ref, o_ref, lhs_ref, *, tm, blocks_per_seq):
    i = pl.program_id(0)
    j = pl.program_id(1)

    @pl.when(j == 0)
    def _():
        first = (i % blocks_per_seq) == 0
        lhs_ref[0:8, :] = jnp.where(first, 0.0, halo_ref[...]).astype(bf16)
        h = h_ref[...]
        lhs_ref[8:8 + tm, :] = h.astype(bf16)
        o_ref[...] = ALPHA * h

    lhs = lhs_ref[...]

    def conv(w_ref, cw_ref, cb_ref):
        up = jnp.dot(lhs, w_ref[...], preferred_element_type=f32)
        y = cb_ref[...] + pltpu.roll(up, 2, 0)[8:8 + tm, :] * cw_ref[0:1, :]
        y = y + pltpu.roll(up, 1, 0)[8:8 + tm, :] * cw_ref[1:2, :]
        return y + up[8:8 + tm, :] * cw_ref[2:3, :]

    act = (_gelu(conv(wg_ref, cwg_ref, cbg_ref)) * conv(wu_ref, cwu_ref, cbu_ref)).astype(bf16)
    o_ref[...] += jnp.dot(act, wd_ref[...], preferred_element_type=f32)

    @pl.when(j == pl.num_programs(1) - 1)
    def _():
        o_ref[...] = _ln(o_ref[...], g2_ref[...], b2_ref[...])


def _ffn(h1, wup, cw, cb, wd, g2, b2, tm, tf, blocks_per_seq):
    T = h1.shape[0]
    nf = D_FF // tf
    vec = _const_spec((1, D_MODEL))
    return pl.pallas_call(
        functools.partial(_ffn_kernel, tm=tm, blocks_per_seq=blocks_per_seq),
        grid=(T // tm, nf),
        in_specs=[pl.BlockSpec((tm, D_MODEL), lambda i, j: (i, 0)),
                  pl.BlockSpec((8, D_MODEL), lambda i, j: (jnp.maximum(i * (tm // 8) - 1, 0), 0)),
                  pl.BlockSpec((D_MODEL, tf), lambda i, j: (0, j)),
                  pl.BlockSpec((D_MODEL, tf), lambda i, j: (0, nf + j)),
                  pl.BlockSpec((FFN_CONV_WIDTH, tf), lambda i, j: (0, j)),
                  pl.BlockSpec((FFN_CONV_WIDTH, tf), lambda i, j: (0, nf + j)),
                  pl.BlockSpec((1, tf), lambda i, j: (0, j)),
                  pl.BlockSpec((1, tf), lambda i, j: (0, nf + j)),
                  pl.BlockSpec((tf, D_MODEL), lambda i, j: (j, 0)),
                  vec, vec],
        out_specs=pl.BlockSpec((tm, D_MODEL), lambda i, j: (i, 0)),
        out_shape=jax.ShapeDtypeStruct((T, D_MODEL), f32),
        scratch_shapes=[pltpu.VMEM((tm + 8, D_MODEL), bf16)],
        compiler_params=_params("parallel", "arbitrary", vmem_limit=V7X_VMEM_LIMIT_FFN_BYTES),
        name="conv_ffn_ln2",
    )(h1, h1, wup, wup, cw, cw, cb, cb, wd, g2, b2)


def _layer(x, mem, ln_in_g, ln_in_b, w_in, lru_conv_w, lru_conv_b, lru_wa, lru_ba, lru_wx, lru_bx,
           lru_lam, cmp_pe_k, cmp_w1_k, cmp_w2_k, cmp_pe_v, cmp_w1_v, cmp_w2_v, w_mem_kv, w_out,
           ln1_g, ln1_b, ffn_w_up, ffn_conv_w, ffn_conv_b, ffn_w_down, ln2_g, ln2_b,
           *, tm_proj, tb_lru, tq_mem, tm_out, tm_ffn, tf_ffn):
    B, S, _ = x.shape
    T = B * S
    G, Dh = NSA_KV_HEADS, NSA_HEAD_DIM
    row = lambda v: v.reshape(1, -1)

    c_q = 2 * LRU_WIDTH
    c_kv = c_q + NSA_HEADS * Dh
    c_gate = c_kv + 6 * G * Dh
    c_mq = c_gate + 3 * NSA_HEADS
    assert (c_q, c_kv, c_gate) == (_NQ0, _NKC0, _NCOLS)
    wn = w_in.astype(bf16)
    wmq = wn[:, c_mq:]
    w_gate = w_in[:, c_gate:c_mq].reshape(D_MODEL, G, NSA_HPG, 3)
    w_gate = jnp.pad(w_gate.transpose(1, 3, 2, 0), ((0, 0), (0, 1), (0, 0), (0, 0)))
    wgT = w_gate.reshape(G * GATE_ROWS, D_MODEL).astype(bf16)

    wax = jnp.concatenate([lru_wa, lru_wx], axis=-1).astype(bf16)

    def cmp_weights(pe, w1, w2):
        pe2 = jnp.broadcast_to(pe.reshape(2, CMP_STRIDE, 1, Dh), (2, CMP_STRIDE, G, Dh)).reshape(2, -1)
        def group_diag(w, axis):
            z = jnp.zeros_like(w)
            return jnp.stack([jnp.concatenate([w if k == g else z for k in range(G)], axis=-1)
                              for g in range(G)], axis=axis)

        w1e = group_diag(w1.reshape(2, CMP_STRIDE, Dh, Dh), 2)
        w1e = w1e.reshape(2, CMP_STRIDE * G * Dh, G * Dh).astype(bf16)
        w2e = group_diag(w2, 0).reshape(G * Dh, G * Dh)
        return pe2, w1e, w2e

    pek, w1k, w2k = cmp_weights(cmp_pe_k, cmp_w1_k, cmp_w2_k)
    pev, w1v, w2v = cmp_weights(cmp_pe_v, cmp_w1_v, cmp_w2_v)

    n_slc, ncmp = S // SLC_BLOCK, S // CMP_STRIDE
    ci = jnp.arange(ncmp)[None, :] * CMP_STRIDE
    sj = jnp.arange(n_slc)[:, None] * SLC_BLOCK
    covT = ((ci <= sj + SLC_BLOCK - 1) & (ci + CMP_BLOCK - 1 >= sj)).astype(bf16)
    eblk = (jnp.arange(S)[:, None] // SLC_BLOCK == jnp.arange(128)[None, :]).astype(bf16)

    wk_mem = w_mem_kv[:, :512].astype(bf16)
    wvT_mem = w_mem_kv[:, 512:].T.astype(bf16)
    wo = w_out.astype(bf16)
    wup = ffn_w_up.astype(bf16)
    wd = ffn_w_down.astype(bf16)

    x2 = x.reshape(T, D_MODEL)
    lx, ly, kc, vc, ks, kw, qT, vsT, vwT, mqT, gT, h0 = _proj(x2, row(ln_in_g), row(ln_in_b), wn, wmq, wgT, tm_proj)

    o_lru = _lru(lx.reshape(B, S, LRU_WIDTH), ly.reshape(B, S, LRU_WIDTH), lru_conv_w, row(lru_conv_b),
                 wax, row(lru_ba), row(lru_bx), row(lru_lam), tb_lru)

    kcmp, vcmpT = _compress(kc, vc, pek, w1k, w2k.astype(bf16), pev, w1v, w2v.T.astype(bf16), B, S)
    o_nsaT = _nsa(qT, gT, kcmp, vcmpT, ks, vsT, kw, vwT, covT, eblk, B, S)

    mk, mvT = _memkv(mem, wk_mem, wvT_mem)
    o_memT = _memattn(mqT, mk, mvT, B, S, tq_mem)

    h1 = _outproj(h0, o_lru.reshape(T, LRU_WIDTH), o_nsaT, o_memT, wo, row(ln1_g), row(ln1_b), tm_out)
    out = _ffn(h1, wup, ffn_conv_w, row(ffn_conv_b), wd, row(ln2_g), row(ln2_b), tm_ffn, tf_ffn, S // tm_ffn)
    return out.reshape(B, S, D_MODEL)


def kernel(x, mem, ln_in_g, ln_in_b, w_in, lru_conv_w, lru_conv_b, lru_wa, lru_ba, lru_wx, lru_bx,
           lru_lam, cmp_pe_k, cmp_w1_k, cmp_w2_k, cmp_pe_v, cmp_w1_v, cmp_w2_v, w_mem_kv, w_out,
           ln1_g, ln1_b, ffn_w_up, ffn_conv_w, ffn_conv_b, ffn_w_down, ln2_g, ln2_b):
    return _layer(x, mem, ln_in_g, ln_in_b, w_in[0], lru_conv_w[0], lru_conv_b[0], lru_wa[0], lru_ba[0],
                  lru_wx[0], lru_bx[0], lru_lam[0], cmp_pe_k[0], cmp_w1_k[0], cmp_w2_k[0], cmp_pe_v[0],
                  cmp_w1_v[0], cmp_w2_v[0], w_mem_kv[0], w_out[0], ln1_g[0], ln1_b[0], ffn_w_up[0],
                  ffn_conv_w[0], ffn_conv_b[0], ffn_w_down[0], ln2_g[0], ln2_b[0],
                  tm_proj=512, tb_lru=256, tq_mem=1024, tm_out=1024, tm_ffn=1024, tf_ffn=512)
```

```python
import functools
import math

import jax
import jax.numpy as jnp
from jax import lax
from jax.experimental import pallas as pl
from jax.experimental.pallas import tpu as pltpu

f32 = jnp.float32
bf16 = jnp.bfloat16

D_MODEL = 2048
LRU_WIDTH = 1024
LRU_BLOCKS = 8
LRU_BLOCK_DIM = 128
LRU_CONV_WIDTH = 4
LRU_C = 8.0
LRU_TINY = 1e-30
NSA_HEADS = 8
NSA_KV_HEADS = 2
NSA_HPG = NSA_HEADS // NSA_KV_HEADS
NSA_HEAD_DIM = 64
NSA_Q_SCALE = NSA_HEAD_DIM ** -0.5 * math.log2(math.e)
CMP_STRIDE = 16
CMP_BLOCK = 32
SLC_BLOCK = 64
N_SELECT = 16
WINDOW = 512
Q_BLOCK = 128
MEM_HEADS = 4
MEM_HEAD_DIM = 128
D_FF = 5632
FFN_CONV_WIDTH = 3
LN_EPS = 1e-5
NEG_INF = -1e30
FORCE_SCORE = 1e9
TAKEN_SCORE = -3e38
ALPHA = 2.0 ** 0.25

V7X_VMEM_LIMIT_BYTES = 56 * 1024 * 1024
V7X_VMEM_LIMIT_FFN_BYTES = 61 * 1024 * 1024

_NX0, _NY0, _NQ0, _NKC0, _NKS0, _NCOLS = 0, 1024, 2048, 2560, 2816, 3328

SEL_CHUNK = 512
WIN_KEYS = WINDOW + Q_BLOCK
GATE_ROWS = 16
OUT_SUB_ROWS = 512
ONES_ROWS = 16


def _ln(x, g, b):
    mu = jnp.mean(x, axis=-1, keepdims=True)
    xc = x - mu
    var = jnp.mean(xc * xc, axis=-1, keepdims=True)
    return xc * lax.rsqrt(var + LN_EPS) * g + b


def _gelu(x):
    return jax.nn.gelu(x)


def _sigmoid(x):
    return 1.0 / (1.0 + jnp.exp(-x))


def _sigmoid_tanh(x):
    return 0.5 * jnp.tanh(0.5 * x) + 0.5


def _nt(a, b):
    return lax.dot_general(a, b, (((1,), (1,)), ((), ())), preferred_element_type=f32)


def _tn(a, b):
    return lax.dot_general(a, b, (((0,), (0,)), ((), ())), preferred_element_type=f32)


def _params(*sem, vmem_limit=V7X_VMEM_LIMIT_BYTES):
    return pltpu.CompilerParams(dimension_semantics=sem, vmem_limit_bytes=vmem_limit)


def _const_spec(shape):
    nd = len(shape)
    return pl.BlockSpec(shape, lambda *_: (0,) * nd, pipeline_mode=pl.Buffered(1))


def _proj_kernel(x_ref, g_ref, b_ref, wn_ref, wmq_ref, wgT_ref,
                 lx_ref, ly_ref, kc_ref, vc_ref, ks_ref, kw_ref,
                 qT_ref, vsT_ref, vwT_ref, mqT_ref, gT_ref, h0_ref):
    h0 = _ln(x_ref[...], g_ref[...], b_ref[...])
    h0_ref[...] = h0
    h = h0.astype(bf16)

    def nat(c0, c1):
        return jnp.dot(h, wn_ref[:, c0:c1], preferred_element_type=f32)

    lx_ref[...] = nat(_NX0, _NY0)
    ly_ref[...] = nat(_NY0, _NQ0)
    qT_ref[...] = (nat(_NQ0, _NKC0) * NSA_Q_SCALE).T.astype(bf16)
    kvc = nat(_NKC0, _NKS0)
    kc_ref[...] = kvc[:, 0:128]
    vc_ref[...] = kvc[:, 128:256]
    kvsw = nat(_NKS0, _NCOLS)
    ks_ref[...] = kvsw[:, 0:128].astype(bf16)
    vsT_ref[...] = kvsw[:, 128:256].T.astype(bf16)
    kw_ref[...] = kvsw[:, 256:384].astype(bf16)
    vwT_ref[...] = kvsw[:, 384:512].T.astype(bf16)
    mqT_ref[...] = jnp.dot(h, wmq_ref[...], preferred_element_type=f32).T.astype(bf16)
    gT_ref[...] = _nt(wgT_ref[...], h)


def _proj(x2, g, b, wn, wmq, wgT, tm):
    T = x2.shape[0]
    nat = lambda w: pl.BlockSpec((tm, w), lambda i: (i, 0))
    trs = lambda r: pl.BlockSpec((r, tm), lambda i: (0, i))
    return pl.pallas_call(
        _proj_kernel,
        grid=(T // tm,),
        in_specs=[pl.BlockSpec((tm, D_MODEL), lambda i: (i, 0)),
                  _const_spec((1, D_MODEL)), _const_spec((1, D_MODEL)),
                  _const_spec(wn.shape), _const_spec((D_MODEL, MEM_HEADS * MEM_HEAD_DIM)),
                  _const_spec((NSA_KV_HEADS * GATE_ROWS, D_MODEL))],
        out_specs=[nat(1024), nat(1024), nat(128), nat(128), nat(128), nat(128),
                   trs(512), trs(128), trs(128), trs(512), trs(32), nat(D_MODEL)],
        out_shape=[jax.ShapeDtypeStruct((T, 1024), f32), jax.ShapeDtypeStruct((T, 1024), f32),
                   jax.ShapeDtypeStruct((T, 128), f32), jax.ShapeDtypeStruct((T, 128), f32),
                   jax.ShapeDtypeStruct((T, 128), bf16), jax.ShapeDtypeStruct((T, 128), bf16),
                   jax.ShapeDtypeStruct((512, T), bf16), jax.ShapeDtypeStruct((128, T), bf16),
                   jax.ShapeDtypeStruct((128, T), bf16), jax.ShapeDtypeStruct((512, T), bf16),
                   jax.ShapeDtypeStruct((32, T), f32), jax.ShapeDtypeStruct((T, D_MODEL), f32)],
        compiler_params=_params("parallel"),
        name="proj",
    )(x2, g, b, wn, wmq, wgT)


def _lru_pitch(tb):
    seg = tb // 8
    return seg + 8 if (seg // 8) % 2 == 0 else seg


def _lru_kernel(x_ref, y_ref, cw_ref, cb_ref, wax_ref, ba_ref, bx_ref, lam_ref, o_ref,
                xext_ref, a_ref, u_ref, carry_ref, *, tb):
    s = pl.program_id(1)

    @pl.when(s == 0)
    def _():
        xext_ref[0:8, :] = jnp.zeros((8, LRU_WIDTH), f32)
        carry_ref[...] = jnp.zeros((1, LRU_WIDTH), f32)

    x = x_ref[0]
    xext_ref[8:8 + tb, :] = x
    xe = xext_ref[...]
    xc = cb_ref[...]
    for k in range(LRU_CONV_WIDTH):
        back = LRU_CONV_WIDTH - 1 - k
        shifted = pltpu.roll(xe, back, 0)[8:8 + tb, :] if back else x
        xc = xc + shifted * cw_ref[k:k + 1, :]
    xext_ref[0:8, :] = x[tb - 8:tb, :]

    xb = xc.astype(bf16)
    gates = [jnp.dot(xb[:, n * 128:(n + 1) * 128], wax_ref[n], preferred_element_type=f32)
             for n in range(LRU_BLOCKS)]
    r = _sigmoid_tanh(jnp.concatenate([gt[:, 0:128] for gt in gates], axis=1) + ba_ref[...])
    i = _sigmoid_tanh(jnp.concatenate([gt[:, 128:256] for gt in gates], axis=1) + bx_ref[...])
    nl = -lam_ref[...]
    softplus = jnp.maximum(nl, 0.0) + jnp.log1p(jnp.exp(-jnp.abs(nl)))
    log_a = (-LRU_C) * r * softplus
    a = jnp.exp(log_a)
    th = jnp.tanh(log_a)
    z = -2.0 * th / (1.0 - th)
    u = (z * lax.rsqrt(jnp.maximum(z, LRU_TINY))) * (i * xc)
    seg = tb // 8
    pitch = _lru_pitch(tb)
    for n in range(LRU_BLOCKS):
        for r8 in range(8):
            a_ref[n, r8 * pitch:r8 * pitch + seg, :] = a[r8 * seg:(r8 + 1) * seg, n * 128:(n + 1) * 128]
            u_ref[n, r8 * pitch:r8 * pitch + seg, :] = u[r8 * seg:(r8 + 1) * seg, n * 128:(n + 1) * 128]

    def sweep(k, carry):
        rows = pl.ds(k, 8, stride=pitch)
        out = []
        for n in range(LRU_BLOCKS):
            h, p = carry[n]
            ak = a_ref[n, rows, :]
            h = ak * h + u_ref[n, rows, :]
            p = ak * p
            u_ref[n, rows, :] = h
            a_ref[n, rows, :] = p
            out.append((h, p))
        return tuple(out)

    init = tuple((jnp.zeros((8, 128), f32), jnp.ones((8, 128), f32)) for _ in range(LRU_BLOCKS))
    ends = lax.fori_loop(0, seg, sweep, init, unroll=4)
    h_end = jnp.concatenate([e[0] for e in ends], axis=1)
    p_end = jnp.concatenate([e[1] for e in ends], axis=1)

    c = carry_ref[...]
    for r8 in range(8):
        rows = slice(r8 * seg, (r8 + 1) * seg)
        prow = slice(r8 * pitch, r8 * pitch + seg)
        h_loc = jnp.concatenate([u_ref[n, prow, :] for n in range(LRU_BLOCKS)], axis=1)
        p_cum = jnp.concatenate([a_ref[n, prow, :] for n in range(LRU_BLOCKS)], axis=1)
        o_ref[0, rows, :] = (_gelu(y_ref[0, rows, :]) * (h_loc + p_cum * c)).astype(bf16)
        c = h_end[r8:r8 + 1, :] + p_end[r8:r8 + 1, :] * c
    carry_ref[...] = c


def _lru(lx, ly, cw, cb, wax, ba, bx, lam, tb):
    B, S, _ = lx.shape
    blk = pl.BlockSpec((1, tb, LRU_WIDTH), lambda b, s: (b, s, 0))
    return pl.pallas_call(
        functools.partial(_lru_kernel, tb=tb),
        grid=(B, S // tb),
        in_specs=[blk, blk, _const_spec((LRU_CONV_WIDTH, LRU_WIDTH)), _const_spec((1, LRU_WIDTH)),
                  _const_spec((LRU_BLOCKS, 128, 256)), _const_spec((1, LRU_WIDTH)),
                  _const_spec((1, LRU_WIDTH)), _const_spec((1, LRU_WIDTH))],
        out_specs=blk,
        out_shape=jax.ShapeDtypeStruct((B, S, LRU_WIDTH), bf16),
        scratch_shapes=[pltpu.VMEM((tb + 8, LRU_WIDTH), f32),
                        pltpu.VMEM((LRU_BLOCKS, 8 * _lru_pitch(tb), 128), f32),
                        pltpu.VMEM((LRU_BLOCKS, 8 * _lru_pitch(tb), 128), f32),
                        pltpu.VMEM((1, LRU_WIDTH), f32)],
        compiler_params=_params("parallel", "arbitrary"),
        name="lru",
    )(lx, ly, cw, cb, wax, ba, bx, lam)


def _cmp_kernel(kc_ref, vc_ref, pek_ref, w1k_ref, w2k_ref, pev_ref, w1v_ref, w2vT_ref,
                kcmp_ref, vcmpT_ref):
    nsub = kc_ref.shape[0] // CMP_STRIDE

    def hidden(x_ref, pe_ref, w1_ref):
        sub = jnp.concatenate([x_ref[pl.ds(l, nsub, stride=CMP_STRIDE), :] for l in range(CMP_STRIDE)], axis=1)
        y0 = jnp.dot((sub + pe_ref[0:1, :]).astype(bf16), w1_ref[0], preferred_element_type=f32)
        y1 = jnp.dot((sub + pe_ref[1:2, :]).astype(bf16), w1_ref[1], preferred_element_type=f32)
        return _gelu(y0 + pltpu.roll(y1, nsub - 1, 0)).astype(bf16)

    kcmp_ref[0] = jnp.dot(hidden(kc_ref, pek_ref, w1k_ref), w2k_ref[...],
                          preferred_element_type=f32).astype(bf16)
    vcmpT_ref[0] = _nt(w2vT_ref[...], hidden(vc_ref, pev_ref, w1v_ref)).astype(bf16)


def _compress(kc, vc, pek, w1k, w2k, pev, w1v, w2vT, B, S):
    nsub = S // CMP_STRIDE
    width = pek.shape[1]
    blk = pl.BlockSpec((S, 128), lambda b: (b, 0))
    return pl.pallas_call(
        _cmp_kernel,
        grid=(B,),
        in_specs=[blk, blk, _const_spec((2, width)), _const_spec((2, width, 128)), _const_spec((128, 128)),
                  _const_spec((2, width)), _const_spec((2, width, 128)), _const_spec((128, 128))],
        out_specs=[pl.BlockSpec((1, nsub, 128), lambda b: (b, 0, 0)),
                   pl.BlockSpec((1, 128, nsub), lambda b: (b, 0, 0))],
        out_shape=[jax.ShapeDtypeStruct((B, nsub, 128), bf16), jax.ShapeDtypeStruct((B, 128, nsub), bf16)],
        compiler_params=_params("parallel"),
        name="nsa_compress",
    )(kc, vc, pek, w1k, w2k, pev, w1v, w2vT)


def _nsa_kernel(qT_ref, gT_ref, kcmp_ref, vcmpT_ref, ks_ref, vsT_ref, kw_ref, vwT_ref, covT_ref, eblk_ref,
                o_ref, qaug_ref, s_ref, acc_ref, m_ref, smax_ref, *, n_slc, n_top):
    i = pl.program_id(1)
    q0 = i * Q_BLOCK
    HQ = NSA_HPG * Q_BLOCK
    GROUPS = range(NSA_KV_HEADS)
    DH = NSA_HEAD_DIM

    def per_head(m):
        return jnp.concatenate([m] * NSA_HPG, axis=1)

    def vrows(g):
        return slice(g * DH, (g + 1) * DH)

    def padded_q(g):
        base = g * NSA_HPG * DH
        qh = jnp.concatenate([qT_ref[base + hp * DH:base + (hp + 1) * DH, :] for hp in range(NSA_HPG)], axis=1)
        zq = jnp.zeros_like(qh)
        return jnp.concatenate([qh, zq] if g == 0 else [zq, qh], axis=0)

    qpad = [padded_q(g) for g in GROUPS]
    t_row = q0 + lax.broadcasted_iota(jnp.int32, (1, Q_BLOCK), 1)

    ncmp = kcmp_ref.shape[1]

    def compressed(rows):
        n_iota = lax.broadcasted_iota(jnp.int32, (rows, Q_BLOCK), 0)
        bias_c = per_head(jnp.where((n_iota * CMP_STRIDE + (CMP_BLOCK - 1)) <= t_row, 0.0, NEG_INF))
        has_key = per_head(t_row) >= CMP_BLOCK - 1
        kc = kcmp_ref[0, 0:rows, :]
        cov = covT_ref[:, 0:rows]
        out = []
        for g in GROUPS:
            sc = jnp.dot(kc, qpad[g], preferred_element_type=f32) + bias_c
            pc = jnp.exp2(sc - jnp.max(sc, axis=0, keepdims=True))
            lc = jnp.sum(pc, axis=0, keepdims=True)
            pc = pc * jnp.where(has_key, 1.0 / lc, 0.0)
            o = jnp.dot(vcmpT_ref[0, vrows(g), 0:rows], pc.astype(bf16), preferred_element_type=f32)
            psum = pc[:, 0:Q_BLOCK]
            for hp in range(1, NSA_HPG):
                psum = psum + pc[:, hp * Q_BLOCK:(hp + 1) * Q_BLOCK]
            hi = psum.astype(bf16)
            r1 = psum - hi.astype(f32)
            mid = r1.astype(bf16)
            lo = (r1 - mid.astype(f32)).astype(bf16)
            out += [o, (jnp.dot(cov, hi, preferred_element_type=f32) + jnp.dot(cov, mid, preferred_element_type=f32)
                        + jnp.dot(cov, lo, preferred_element_type=f32))]
        return tuple(out)

    row_options = [r for r in (ncmp // 4, ncmp // 2, 3 * ncmp // 4, ncmp) if r % 128 == 0]
    need = (Q_BLOCK // CMP_STRIDE) * (i + 1)
    which = sum((need > r).astype(jnp.int32) for r in row_options[:-1])
    cmp_out = lax.switch(which, [functools.partial(compressed, r) for r in row_options])
    oc = [cmp_out[2 * g] for g in GROUPS]
    imp = [cmp_out[2 * g + 1] for g in GROUPS]

    j_i = lax.broadcasted_iota(jnp.int32, (n_slc, Q_BLOCK), 0)
    j_f = j_i.astype(f32)
    cur = t_row >> 6
    forced = (j_i == 0) | (j_i == cur) | (j_i == cur - 1)
    visible = (j_i * SLC_BLOCK) <= t_row
    score = [jnp.where(forced, TAKEN_SCORE, jnp.where(visible, imp[g], NEG_INF)) for g in GROUPS]

    w0 = pl.multiple_of(jnp.maximum(q0 - WINDOW, 0), Q_BLOCK)
    kp = w0 + lax.broadcasted_iota(jnp.int32, (WIN_KEYS, Q_BLOCK), 0)
    bias_w = per_head(jnp.where((kp <= t_row) & (kp > (t_row - WINDOW)), 0.0, NEG_INF))
    kwin = kw_ref[pl.ds(w0, WIN_KEYS), :]
    ow = []
    for g in GROUPS:
        sw = jnp.dot(kwin, qpad[g], preferred_element_type=f32) + bias_w
        pw = jnp.exp2(sw - jnp.max(sw, axis=0, keepdims=True)).astype(bf16)
        vw1 = jnp.concatenate([vwT_ref[vrows(g), pl.ds(w0, WIN_KEYS)], jnp.ones((ONES_ROWS, WIN_KEYS), bf16)],
                              axis=0)
        o = jnp.dot(vw1, pw, preferred_element_type=f32)
        ow.append(o[0:DH, :] * (1.0 / o[DH:DH + 1, :]))

    sel = [jnp.where(forced, 1.0, 0.0) for g in GROUPS]
    for _ in range(n_top - 3):
        for g in GROUPS:
            mx = jnp.max(score[g], axis=0, keepdims=True)
            first = jnp.min(jnp.where(score[g] == mx, j_f, float(n_slc)), axis=0, keepdims=True)
            hit = j_f == first
            sel[g] = jnp.where(hit, 1.0, sel[g])
            score[g] = jnp.where(hit, TAKEN_SCORE, score[g])

    for g in GROUPS:
        qaug_ref[g, 0:2 * DH, :] = qpad[g]
        blk_bias = jnp.where((sel[g] > 0.0) & visible, 0.0, NEG_INF)
        qaug_ref[g, 2 * DH:2 * DH + n_slc, :] = per_head(blk_bias).astype(bf16)
        if n_slc < 128:
            qaug_ref[g, 2 * DH + n_slc:, :] = jnp.zeros((128 - n_slc, HQ), bf16)

    def scores(c):
        k0 = pl.multiple_of(c * SEL_CHUNK, SEL_CHUNK)
        kaug = jnp.concatenate([ks_ref[pl.ds(k0, SEL_CHUNK), :], eblk_ref[pl.ds(k0, SEL_CHUNK), :]], axis=1)
        return [jnp.dot(kaug, qaug_ref[g], preferred_element_type=f32) for g in GROUPS]

    def attend(g, c, s, smax):
        k0 = pl.multiple_of(c * SEL_CHUNK, SEL_CHUNK)
        m = m_ref[g]
        m_new = jnp.maximum(m, smax)
        m_ref[g] = m_new
        p = jnp.exp2(s - m_new).astype(bf16)
        v1 = jnp.concatenate([vsT_ref[vrows(g), pl.ds(k0, SEL_CHUNK)], jnp.ones((ONES_ROWS, SEL_CHUNK), bf16)],
                             axis=0)
        acc_ref[g] = jnp.exp2(m - m_new) * acc_ref[g] + jnp.dot(v1, p, preferred_element_type=f32)

    def step(c, cur, nxt):
        smax = [smax_ref[g] for g in GROUPS]
        s_next = scores(c + 1)
        for g in GROUPS:
            s_ref[g, nxt] = s_next[g]
            smax_ref[g] = jnp.max(s_next[g], axis=0, keepdims=True)
        for g in GROUPS:
            attend(g, c, s_ref[g, cur], smax[g])

    last = q0 // SEL_CHUNK
    odd = last % 2
    s0 = scores(0)
    for g in GROUPS:
        s_ref[g, odd] = s0[g]
        smax_ref[g] = jnp.max(s0[g], axis=0, keepdims=True)
        acc_ref[g] = jnp.zeros((DH + ONES_ROWS, HQ), f32)
        m_ref[g] = jnp.full((1, HQ), NEG_INF, f32)

    @pl.when(odd == 1)
    def _():
        step(0, 1, 0)

    has_pair = (last // 2) % 2

    @pl.when(has_pair == 1)
    def _():
        step(odd, 0, 1)
        step(odd + 1, 1, 0)

    def quad(qq, carry):
        c = odd + 2 * has_pair + 4 * qq
        step(c, 0, 1)
        step(c + 1, 1, 0)
        step(c + 2, 0, 1)
        step(c + 3, 1, 0)
        return carry

    lax.fori_loop(0, last // 4, quad, 0)

    kp_d = last * SEL_CHUNK + lax.broadcasted_iota(jnp.int32, (SEL_CHUNK, Q_BLOCK), 0)
    bias_d = per_head(jnp.where(kp_d <= t_row, 0.0, NEG_INF))
    for g in GROUPS:
        s_d = s_ref[g, 0] + bias_d
        attend(g, last, s_d, jnp.max(s_d, axis=0, keepdims=True))

    gate = _sigmoid(gT_ref[...])
    for g in GROUPS:
        osel = acc_ref[g, 0:DH, :] * (1.0 / acc_ref[g, DH:DH + 1, :])
        for hp in range(NSA_HPG):
            cols = slice(hp * Q_BLOCK, (hp + 1) * Q_BLOCK)
            gr = g * GATE_ROWS + hp
            o = (gate[gr:gr + 1, :] * oc[g][:, cols] + gate[gr + 4:gr + 5, :] * osel[:, cols]
                 + gate[gr + 8:gr + 9, :] * ow[g][:, cols])
            o_ref[(g * NSA_HPG + hp) * DH:(g * NSA_HPG + hp + 1) * DH, :] = o.astype(bf16)


def _nsa(qT, gT, kcmp, vcmpT, ks, vsT, kw, vwT, covT, eblk, B, S):
    n_q = S // Q_BLOCK
    n_slc = S // SLC_BLOCK
    ncmp = S // CMP_STRIDE
    G = NSA_KV_HEADS
    HQ = NSA_HPG * Q_BLOCK
    tok = lambda r: pl.BlockSpec((r, Q_BLOCK), lambda b, i: (0, b * n_q + i))
    per_b_nat = pl.BlockSpec((S, 128), lambda b, i: (b, 0))
    per_b_tr = pl.BlockSpec((128, S), lambda b, i: (0, b))
    return pl.pallas_call(
        functools.partial(_nsa_kernel, n_slc=n_slc, n_top=min(N_SELECT, n_slc)),
        grid=(B, n_q),
        in_specs=[tok(NSA_HEADS * NSA_HEAD_DIM), tok(G * GATE_ROWS),
                  pl.BlockSpec((1, ncmp, 128), lambda b, i: (b, 0, 0)),
                  pl.BlockSpec((1, 128, ncmp), lambda b, i: (b, 0, 0)),
                  per_b_nat, per_b_tr, per_b_nat, per_b_tr,
                  _const_spec((n_slc, ncmp)), _const_spec((S, 128))],
        out_specs=tok(NSA_HEADS * NSA_HEAD_DIM),
        out_shape=jax.ShapeDtypeStruct((NSA_HEADS * NSA_HEAD_DIM, B * S), bf16),
        scratch_shapes=[pltpu.VMEM((G, 2 * NSA_HEAD_DIM + 128, HQ), bf16),
                        pltpu.VMEM((G, 2, SEL_CHUNK, HQ), f32),
                        pltpu.VMEM((G, NSA_HEAD_DIM + ONES_ROWS, HQ), f32),
                        pltpu.VMEM((G, 1, HQ), f32), pltpu.VMEM((G, 1, HQ), f32)],
        compiler_params=_params("parallel", "arbitrary"),
        name="nsa_attn",
    )(qT, gT, kcmp, vcmpT, ks, vsT, kw, vwT, covT, eblk)


def _memkv_kernel(mem_ref, wk_ref, wvT_ref, k_ref, vT_ref):
    m = mem_ref[0].astype(bf16)
    k_ref[0] = jnp.dot(m, wk_ref[...], preferred_element_type=f32).astype(bf16)
    vT_ref[0] = _nt(wvT_ref[...], m).astype(bf16)


def _memkv(mem, wk, wvT):
    B, M, _ = mem.shape
    return pl.pallas_call(
        _memkv_kernel,
        grid=(B,),
        in_specs=[pl.BlockSpec((1, M, D_MODEL), lambda b: (b, 0, 0)),
                  _const_spec((D_MODEL, 512)), _const_spec((512, D_MODEL))],
        out_specs=[pl.BlockSpec((1, M, 512), lambda b: (b, 0, 0)),
                   pl.BlockSpec((1, 512, M), lambda b: (b, 0, 0))],
        out_shape=[jax.ShapeDtypeStruct((B, M, 512), bf16), jax.ShapeDtypeStruct((B, 512, M), bf16)],
        compiler_params=_params("parallel"),
        name="mem_kv",
    )(mem, wk, wvT)


def _memattn_kernel(qT_ref, k_ref, vT_ref, o_ref):
    for h in range(MEM_HEADS):
        rows = slice(h * MEM_HEAD_DIM, (h + 1) * MEM_HEAD_DIM)
        s = jnp.dot(k_ref[0, :, rows], qT_ref[rows, :], preferred_element_type=f32) * (MEM_HEAD_DIM ** -0.5)
        p = jnp.exp(s - jnp.max(s, axis=0, keepdims=True))
        l = jnp.sum(p, axis=0, keepdims=True)
        o = jnp.dot(vT_ref[0, rows, :], p.astype(bf16), preferred_element_type=f32) * (1.0 / l)
        o_ref[rows, :] = o.astype(bf16)


def _memattn(mqT, mk, mvT, B, S, tq):
    M = mk.shape[1]
    nq = S // tq
    return pl.pallas_call(
        _memattn_kernel,
        grid=(B, nq),
        in_specs=[pl.BlockSpec((512, tq), lambda b, i: (0, b * nq + i)),
                  pl.BlockSpec((1, M, 512), lambda b, i: (b, 0, 0)),
                  pl.BlockSpec((1, 512, M), lambda b, i: (b, 0, 0))],
        out_specs=pl.BlockSpec((512, tq), lambda b, i: (0, b * nq + i)),
        out_shape=jax.ShapeDtypeStruct((512, B * S), bf16),
        compiler_params=_params("parallel", "parallel"),
        name="mem_attn",
    )(mqT, mk, mvT)


def _outproj_kernel(h0_ref, ol_ref, onT_ref, omT_ref, wo_ref, g1_ref, b1_ref, h1_ref, *, tm):
    for r0 in range(0, tm, OUT_SUB_ROWS):
        rows = slice(r0, r0 + OUT_SUB_ROWS)
        mixed = jnp.dot(ol_ref[rows, :], wo_ref[0:1024, :], preferred_element_type=f32)
        mixed = mixed + _tn(onT_ref[:, rows], wo_ref[1024:1536, :])
        mixed = mixed + _tn(omT_ref[:, rows], wo_ref[1536:2048, :])
        h1_ref[rows, :] = _ln(ALPHA * h0_ref[rows, :] + mixed, g1_ref[...], b1_ref[...])


def _outproj(h0, ol, onT, omT, wo, g1, b1, tm):
    T = h0.shape[0]
    row = lambda w: pl.BlockSpec((tm, w), lambda i: (i, 0))
    trs = pl.BlockSpec((512, tm), lambda i: (0, i))
    vec = _const_spec((1, D_MODEL))
    return pl.pallas_call(
        functools.partial(_outproj_kernel, tm=tm),
        grid=(T // tm,),
        in_specs=[row(D_MODEL), row(1024), trs, trs, _const_spec((D_MODEL, D_MODEL)), vec, vec],
        out_specs=row(D_MODEL),
        out_shape=jax.ShapeDtypeStruct((T, D_MODEL), f32),
        compiler_params=_params("parallel"),
        name="out_proj_ln1",
    )(h0, ol, onT, omT, wo, g1, b1)


def _ffn_kernel(h_ref, halo_ref, wg_ref, wu_ref, cwg_ref, cwu_ref, cbg_ref, cbu_ref, wd_ref,
                g2_ref, b2_ref, o_ref, lhs_ref, *, tm, blocks_per_seq):
    i = pl.program_id(0)
    j = pl.program_id(1)

    @pl.when(j == 0)
    def _():
        first = (i % blocks_per_seq) == 0
        lhs_ref[0:8, :] = jnp.where(first, 0.0, halo_ref[...]).astype(bf16)
        h = h_ref[...]
        lhs_ref[8:8 + tm, :] = h.astype(bf16)
        o_ref[...] = ALPHA * h

    lhs = lhs_ref[...]

    def conv(w_ref, cw_ref, cb_ref):
        up = jnp.dot(lhs, w_ref[...], preferred_element_type=f32)
        y = cb_ref[...] + pltpu.roll(up, 2, 0)[8:8 + tm, :] * cw_ref[0:1, :]
        y = y + pltpu.roll(up, 1, 0)[8:8 + tm, :] * cw_ref[1:2, :]
        return y + up[8:8 + tm, :] * cw_ref[2:3, :]

    act = (_gelu(conv(wg_ref, cwg_ref, cbg_ref)) * conv(wu_ref, cwu_ref, cbu_ref)).astype(bf16)
    o_ref[...] += jnp.dot(act, wd_ref[...], preferred_element_type=f32)

    @pl.when(j == pl.num_programs(1) - 1)
    def _():
        o_ref[...] = _ln(o_ref[...], g2_ref[...], b2_ref[...])


def _ffn(h1, wup, cw, cb, wd, g2, b2, tm, tf, blocks_per_seq):
    T = h1.shape[0]
    nf = D_FF // tf
    vec = _const_spec((1, D_MODEL))
    return pl.pallas_call(
        functools.partial(_ffn_kernel, tm=tm, blocks_per_seq=blocks_per_seq),
        grid=(T // tm, nf),
        in_specs=[pl.BlockSpec((tm, D_MODEL), lambda i, j: (i, 0)),
                  pl.BlockSpec((8, D_MODEL), lambda i, j: (jnp.maximum(i * (tm // 8) - 1, 0), 0)),
                  pl.BlockSpec((D_MODEL, tf), lambda i, j: (0, j)),
                  pl.BlockSpec((D_MODEL, tf), lambda i, j: (0, nf + j)),
                  pl.BlockSpec((FFN_CONV_WIDTH, tf), lambda i, j: (0, j)),
                  pl.BlockSpec((FFN_CONV_WIDTH, tf), lambda i, j: (0, nf + j)),
                  pl.BlockSpec((1, tf), lambda i, j: (0, j)),
                  pl.BlockSpec((1, tf), lambda i, j: (0, nf + j)),
                  pl.BlockSpec((tf, D_MODEL), lambda i, j: (j, 0)),
                  vec, vec],
        out_specs=pl.BlockSpec((tm, D_MODEL), lambda i, j: (i, 0)),
        out_shape=jax.ShapeDtypeStruct((T, D_MODEL), f32),
        scratch_shapes=[pltpu.VMEM((tm + 8, D_MODEL), bf16)],
        compiler_params=_params("parallel", "arbitrary", vmem_limit=V7X_VMEM_LIMIT_FFN_BYTES),
        name="conv_ffn_ln2",
    )(h1, h1, wup, wup, cw, cw, cb, cb, wd, g2, b2)


def _layer(x, mem, ln_in_g, ln_in_b, w_in, lru_conv_w, lru_conv_b, lru_wa, lru_ba, lru_wx, lru_bx,
           lru_lam, cmp_pe_k, cmp_w1_k, cmp_w2_k, cmp_pe_v, cmp_w1_v, cmp_w2_v, w_mem_kv, w_out,
           ln1_g, ln1_b, ffn_w_up, ffn_conv_w, ffn_conv_b, ffn_w_down, ln2_g, ln2_b,
           *, tm_proj, tb_lru, tq_mem, tm_out, tm_ffn, tf_ffn):
    B, S, _ = x.shape
    T = B * S
    G, Dh = NSA_KV_HEADS, NSA_HEAD_DIM
    row = lambda v: v.reshape(1, -1)

    c_q = 2 * LRU_WIDTH
    c_kv = c_q + NSA_HEADS * Dh
    c_gate = c_kv + 6 * G * Dh
    c_mq = c_gate + 3 * NSA_HEADS
    assert (c_q, c_kv, c_gate) == (_NQ0, _NKC0, _NCOLS)
    wn = w_in.astype(bf16)
    wmq = wn[:, c_mq:]
    w_gate = w_in[:, c_gate:c_mq].reshape(D_MODEL, G, NSA_HPG, 3)
    w_gate = jnp.pad(w_gate.transpose(1, 3, 2, 0), ((0, 0), (0, 1), (0, 0), (0, 0)))
    wgT = w_gate.reshape(G * GATE_ROWS, D_MODEL).astype(bf16)

    wax = jnp.concatenate([lru_wa, lru_wx], axis=-1).astype(bf16)

    def cmp_weights(pe, w1, w2):
        pe2 = jnp.broadcast_to(pe.reshape(2, CMP_STRIDE, 1, Dh), (2, CMP_STRIDE, G, Dh)).reshape(2, -1)
        def group_diag(w, axis):
            z = jnp.zeros_like(w)
            return jnp.stack([jnp.concatenate([w if k == g else z for k in range(G)], axis=-1)
                              for g in range(G)], axis=axis)

        w1e = group_diag(w1.reshape(2, CMP_STRIDE, Dh, Dh), 2)
        w1e = w1e.reshape(2, CMP_STRIDE * G * Dh, G * Dh).astype(bf16)
        w2e = group_diag(w2, 0).reshape(G * Dh, G * Dh)
        return pe2, w1e, w2e

    pek, w1k, w2k = cmp_weights(cmp_pe_k, cmp_w1_k, cmp_w2_k)
    pev, w1v, w2v = cmp_weights(cmp_pe_v, cmp_w1_v, cmp_w2_v)

    n_slc, ncmp = S // SLC_BLOCK, S // CMP_STRIDE
    ci = jnp.arange(ncmp)[None, :] * CMP_STRIDE
    sj = jnp.arange(n_slc)[:, None] * SLC_BLOCK
    covT = ((ci <= sj + SLC_BLOCK - 1) & (ci + CMP_BLOCK - 1 >= sj)).astype(bf16)
    eblk = (jnp.arange(S)[:, None] // SLC_BLOCK == jnp.arange(128)[None, :]).astype(bf16)

    wk_mem = w_mem_kv[:, :512].astype(bf16)
    wvT_mem = w_mem_kv[:, 512:].T.astype(bf16)
    wo = w_out.astype(bf16)
    wup = ffn_w_up.astype(bf16)
    wd = ffn_w_down.astype(bf16)

    x2 = x.reshape(T, D_MODEL)
    lx, ly, kc, vc, ks, kw, qT, vsT, vwT, mqT, gT, h0 = _proj(x2, row(ln_in_g), row(ln_in_b), wn, wmq, wgT, tm_proj)

    o_lru = _lru(lx.reshape(B, S, LRU_WIDTH), ly.reshape(B, S, LRU_WIDTH), lru_conv_w, row(lru_conv_b),
                 wax, row(lru_ba), row(lru_bx), row(lru_lam), tb_lru)

    kcmp, vcmpT = _compress(kc, vc, pek, w1k, w2k.astype(bf16), pev, w1v, w2v.T.astype(bf16), B, S)
    o_nsaT = _nsa(qT, gT, kcmp, vcmpT, ks, vsT, kw, vwT, covT, eblk, B, S)

    mk, mvT = _memkv(mem, wk_mem, wvT_mem)
    o_memT = _memattn(mqT, mk, mvT, B, S, tq_mem)

    h1 = _outproj(h0, o_lru.reshape(T, LRU_WIDTH), o_nsaT, o_memT, wo, row(ln1_g), row(ln1_b), tm_out)
    out = _ffn(h1, wup, ffn_conv_w, row(ffn_conv_b), wd, row(ln2_g), row(ln2_b), tm_ffn, tf_ffn, S // tm_ffn)
    return out.reshape(B, S, D_MODEL)


def kernel(x, mem, ln_in_g, ln_in_b, w_in, lru_conv_w, lru_conv_b, lru_wa, lru_ba, lru_wx, lru_bx,
           lru_lam, cmp_pe_k, cmp_w1_k, cmp_w2_k, cmp_pe_v, cmp_w1_v, cmp_w2_v, w_mem_kv, w_out,
           ln1_g, ln1_b, ffn_w_up, ffn_conv_w, ffn_conv_b, ffn_w_down, ln2_g, ln2_b):
    return _layer(x, mem, ln_in_g, ln_in_b, w_in[0], lru_conv_w[0], lru_conv_b[0], lru_wa[0], lru_ba[0],
                  lru_wx[0], lru_bx[0], lru_lam[0], cmp_pe_k[0], cmp_w1_k[0], cmp_w2_k[0], cmp_pe_v[0],
                  cmp_w1_v[0], cmp_w2_v[0], w_mem_kv[0], w_out[0], ln1_g[0], ln1_b[0], ffn_w_up[0],
                  ffn_conv_w[0], ffn_conv_b[0], ffn_w_down[0], ln2_g[0], ln2_b[0],
                  tm_proj=512, tb_lru=256, tq_mem=1024, tm_out=1024, tm_ffn=1024, tf_ffn=512)
```

```python
import functools
import math

import jax
import jax.numpy as jnp
from jax import lax
from jax.experimental import pallas as pl
from jax.experimental.pallas import tpu as pltpu

f32 = jnp.float32
bf16 = jnp.bfloat16

D_MODEL = 2048
LRU_WIDTH = 1024
LRU_BLOCKS = 8
LRU_CONV_WIDTH = 4
LRU_C = 8.0
LRU_TINY = 1e-30
NSA_HEADS = 8
NSA_KV_HEADS = 2
NSA_HPG = NSA_HEADS // NSA_KV_HEADS
NSA_HEAD_DIM = 64
NSA_Q_SCALE = NSA_HEAD_DIM ** -0.5 * math.log2(math.e)
CMP_STRIDE = 16
CMP_BLOCK = 32
SLC_BLOCK = 64
N_SELECT = 16
WINDOW = 512
Q_BLOCK = 128
MEM_HEADS = 4
MEM_HEAD_DIM = 128
D_FF = 5632
FFN_CONV_WIDTH = 3
LN_EPS = 1e-5
NEG_INF = -1e30
TAKEN_SCORE = -3e38
ALPHA = 2.0 ** 0.25

V7X_VMEM_LIMIT_BYTES = 56 * 1024 * 1024
V7X_VMEM_LIMIT_FFN_BYTES = 61 * 1024 * 1024

_NX0, _NY0, _NQ0, _NKC0, _NKS0, _NCOLS = 0, 1024, 2048, 2560, 2816, 3328

SEL_CHUNK = 512
WIN_KEYS = WINDOW + Q_BLOCK
GATE_ROWS = 16
OUT_SUB_ROWS = 256
ONES_ROWS = 16


def _ln(x, g, b):
    mu = jnp.mean(x, axis=-1, keepdims=True)
    xc = x - mu
    var = jnp.mean(xc * xc, axis=-1, keepdims=True)
    return xc * lax.rsqrt(var + LN_EPS) * g + b


def _gelu(x):
    return jax.nn.gelu(x)


def _sigmoid(x):
    return 1.0 / (1.0 + jnp.exp(-x))


def _sigmoid_tanh(x):
    return 0.5 * jnp.tanh(0.5 * x) + 0.5


def _nt(a, b):
    return lax.dot_general(a, b, (((1,), (1,)), ((), ())), preferred_element_type=f32)


def _tn(a, b):
    return lax.dot_general(a, b, (((0,), (0,)), ((), ())), preferred_element_type=f32)


def _params(*sem, vmem_limit=V7X_VMEM_LIMIT_BYTES):
    return pltpu.CompilerParams(dimension_semantics=sem, vmem_limit_bytes=vmem_limit)


def _const_spec(shape):
    nd = len(shape)
    return pl.BlockSpec(shape, lambda *_: (0,) * nd, pipeline_mode=pl.Buffered(1))


def _proj_kernel(x_ref, g_ref, b_ref, wn_ref, wmq_ref, wgT_ref,
                 lx_ref, ly_ref, kc_ref, vc_ref, ks_ref, kw_ref,
                 qT_ref, vsT_ref, vwT_ref, mqT_ref, gT_ref, h0_ref):
    half = x_ref.shape[0] // 2
    hs = []
    for r0 in (0, half):
        h0 = _ln(x_ref[r0:r0 + half, :], g_ref[...], b_ref[...])
        h0_ref[r0:r0 + half, :] = h0
        hs.append(h0.astype(bf16))
        lx_ref[r0:r0 + half, :] = jnp.dot(hs[-1], wn_ref[:, _NX0:_NY0], preferred_element_type=f32)
    h = jnp.concatenate(hs, axis=0)

    def nat(c0, c1):
        return jnp.dot(h, wn_ref[:, c0:c1], preferred_element_type=f32)

    ly_ref[...] = nat(_NY0, _NQ0)
    qT_ref[...] = (nat(_NQ0, _NKC0) * NSA_Q_SCALE).T.astype(bf16)
    kvc = nat(_NKC0, _NKS0)
    kc_ref[...] = kvc[:, 0:128]
    vc_ref[...] = kvc[:, 128:256]
    kvsw = nat(_NKS0, _NCOLS)
    ks_ref[...] = kvsw[:, 0:128].astype(bf16)
    vsT_ref[...] = kvsw[:, 128:256].T.astype(bf16)
    kw_ref[...] = kvsw[:, 256:384].astype(bf16)
    vwT_ref[...] = kvsw[:, 384:512].T.astype(bf16)
    mqT_ref[...] = jnp.dot(h, wmq_ref[...], preferred_element_type=f32).T.astype(bf16)
    gT_ref[...] = _nt(wgT_ref[...], h)


def _proj(x2, g, b, wn, wmq, wgT, tm):
    T = x2.shape[0]
    nat = lambda w: pl.BlockSpec((tm, w), lambda i: (i, 0))
    trs = lambda r: pl.BlockSpec((r, tm), lambda i: (0, i))
    return pl.pallas_call(
        _proj_kernel,
        grid=(T // tm,),
        in_specs=[pl.BlockSpec((tm, D_MODEL), lambda i: (i, 0)),
                  _const_spec((1, D_MODEL)), _const_spec((1, D_MODEL)),
                  _const_spec(wn.shape), _const_spec((D_MODEL, MEM_HEADS * MEM_HEAD_DIM)),
                  _const_spec((NSA_KV_HEADS * GATE_ROWS, D_MODEL))],
        out_specs=[nat(1024), nat(1024), nat(128), nat(128), nat(128), nat(128),
                   trs(512), trs(128), trs(128), trs(512), trs(32), nat(D_MODEL)],
        out_shape=[jax.ShapeDtypeStruct((T, 1024), f32), jax.ShapeDtypeStruct((T, 1024), f32),
                   jax.ShapeDtypeStruct((T, 128), f32), jax.ShapeDtypeStruct((T, 128), f32),
                   jax.ShapeDtypeStruct((T, 128), bf16), jax.ShapeDtypeStruct((T, 128), bf16),
                   jax.ShapeDtypeStruct((512, T), bf16), jax.ShapeDtypeStruct((128, T), bf16),
                   jax.ShapeDtypeStruct((128, T), bf16), jax.ShapeDtypeStruct((512, T), bf16),
                   jax.ShapeDtypeStruct((32, T), f32), jax.ShapeDtypeStruct((T, D_MODEL), f32)],
        compiler_params=_params("parallel"),
        name="proj",
    )(x2, g, b, wn, wmq, wgT)


def _lru_pitch(tb):
    seg = tb // 8
    return seg + 8 if (seg // 8) % 2 == 0 else seg


def _lru_kernel(x_ref, y_ref, cw_ref, cb_ref, wax_ref, ba_ref, bx_ref, lam_ref, o_ref,
                xext_ref, a_ref, u_ref, carry_ref, *, tb):
    s = pl.program_id(1)

    @pl.when(s == 0)
    def _():
        xext_ref[0:8, :] = jnp.zeros((8, LRU_WIDTH), f32)
        carry_ref[...] = jnp.zeros((1, LRU_WIDTH), f32)

    x = x_ref[0]
    xext_ref[8:8 + tb, :] = x
    xe = xext_ref[...]
    xc = cb_ref[...]
    for k in range(LRU_CONV_WIDTH):
        back = LRU_CONV_WIDTH - 1 - k
        shifted = pltpu.roll(xe, back, 0)[8:8 + tb, :] if back else x
        xc = xc + shifted * cw_ref[k:k + 1, :]
    xext_ref[0:8, :] = x[tb - 8:tb, :]

    xb = xc.astype(bf16)
    gates = [jnp.dot(xb[:, n * 128:(n + 1) * 128], wax_ref[n], preferred_element_type=f32)
             for n in range(LRU_BLOCKS)]
    r = _sigmoid_tanh(jnp.concatenate([gt[:, 0:128] for gt in gates], axis=1) + ba_ref[...])
    i = _sigmoid_tanh(jnp.concatenate([gt[:, 128:256] for gt in gates], axis=1) + bx_ref[...])
    nl = -lam_ref[...]
    softplus = jnp.maximum(nl, 0.0) + jnp.log1p(jnp.exp(-jnp.abs(nl)))
    log_a = (-LRU_C) * r * softplus
    a = jnp.exp(log_a)
    th = jnp.tanh(log_a)
    z = -2.0 * th / (1.0 - th)
    u = (z * lax.rsqrt(jnp.maximum(z, LRU_TINY))) * (i * xc)
    seg = tb // 8
    pitch = _lru_pitch(tb)
    for n in range(LRU_BLOCKS):
        for r8 in range(8):
            a_ref[n, r8 * pitch:r8 * pitch + seg, :] = a[r8 * seg:(r8 + 1) * seg, n * 128:(n + 1) * 128]
            u_ref[n, r8 * pitch:r8 * pitch + seg, :] = u[r8 * seg:(r8 + 1) * seg, n * 128:(n + 1) * 128]

    def sweep(k, carry):
        rows = pl.ds(k, 8, stride=pitch)
        out = []
        for n in range(LRU_BLOCKS):
            h, p = carry[n]
            ak = a_ref[n, rows, :]
            h = ak * h + u_ref[n, rows, :]
            p = ak * p
            u_ref[n, rows, :] = h
            a_ref[n, rows, :] = p
            out.append((h, p))
        return tuple(out)

    init = tuple((jnp.zeros((8, 128), f32), jnp.ones((8, 128), f32)) for _ in range(LRU_BLOCKS))
    ends = lax.fori_loop(0, seg, sweep, init, unroll=4)
    h_end = jnp.concatenate([e[0] for e in ends], axis=1)
    p_end = jnp.concatenate([e[1] for e in ends], axis=1)

    c = carry_ref[...]
    for r8 in range(8):
        rows = slice(r8 * seg, (r8 + 1) * seg)
        prow = slice(r8 * pitch, r8 * pitch + seg)
        h_loc = jnp.concatenate([u_ref[n, prow, :] for n in range(LRU_BLOCKS)], axis=1)
        p_cum = jnp.concatenate([a_ref[n, prow, :] for n in range(LRU_BLOCKS)], axis=1)
        o_ref[0, rows, :] = (_gelu(y_ref[0, rows, :]) * (h_loc + p_cum * c)).astype(bf16)
        c = h_end[r8:r8 + 1, :] + p_end[r8:r8 + 1, :] * c
    carry_ref[...] = c


def _lru(lx, ly, cw, cb, wax, ba, bx, lam, tb):
    B, S, _ = lx.shape
    blk = pl.BlockSpec((1, tb, LRU_WIDTH), lambda b, s: (b, s, 0))
    return pl.pallas_call(
        functools.partial(_lru_kernel, tb=tb),
        grid=(B, S // tb),
        in_specs=[blk, blk, _const_spec((LRU_CONV_WIDTH, LRU_WIDTH)), _const_spec((1, LRU_WIDTH)),
                  _const_spec((LRU_BLOCKS, 128, 256)), _const_spec((1, LRU_WIDTH)),
                  _const_spec((1, LRU_WIDTH)), _const_spec((1, LRU_WIDTH))],
        out_specs=blk,
        out_shape=jax.ShapeDtypeStruct((B, S, LRU_WIDTH), bf16),
        scratch_shapes=[pltpu.VMEM((tb + 8, LRU_WIDTH), f32),
                        pltpu.VMEM((LRU_BLOCKS, 8 * _lru_pitch(tb), 128), f32),
                        pltpu.VMEM((LRU_BLOCKS, 8 * _lru_pitch(tb), 128), f32),
                        pltpu.VMEM((1, LRU_WIDTH), f32)],
        compiler_params=_params("parallel", "arbitrary"),
        name="lru",
    )(lx, ly, cw, cb, wax, ba, bx, lam)


def _cmp_kernel(kc_ref, vc_ref, pek_ref, w1k_ref, w2k_ref, pev_ref, w1v_ref, w2vT_ref,
                kcmp_ref, vcmpT_ref):
    nsub = kc_ref.shape[0] // CMP_STRIDE

    def hidden(x_ref, pe_ref, w1_ref):
        sub = jnp.concatenate([x_ref[pl.ds(l, nsub, stride=CMP_STRIDE), :] for l in range(CMP_STRIDE)], axis=1)
        y0 = jnp.dot((sub + pe_ref[0:1, :]).astype(bf16), w1_ref[0], preferred_element_type=f32)
        y1 = jnp.dot((sub + pe_ref[1:2, :]).astype(bf16), w1_ref[1], preferred_element_type=f32)
        return _gelu(y0 + pltpu.roll(y1, nsub - 1, 0)).astype(bf16)

    kcmp_ref[0] = jnp.dot(hidden(kc_ref, pek_ref, w1k_ref), w2k_ref[...],
                          preferred_element_type=f32).astype(bf16)
    vcmpT_ref[0] = _nt(w2vT_ref[...], hidden(vc_ref, pev_ref, w1v_ref)).astype(bf16)


def _compress(kc, vc, pek, w1k, w2k, pev, w1v, w2vT, B, S):
    nsub = S // CMP_STRIDE
    width = pek.shape[1]
    blk = pl.BlockSpec((S, 128), lambda b: (b, 0))
    return pl.pallas_call(
        _cmp_kernel,
        grid=(B,),
        in_specs=[blk, blk, _const_spec((2, width)), _const_spec((2, width, 128)), _const_spec((128, 128)),
                  _const_spec((2, width)), _const_spec((2, width, 128)), _const_spec((128, 128))],
        out_specs=[pl.BlockSpec((1, nsub, 128), lambda b: (b, 0, 0)),
                   pl.BlockSpec((1, 128, nsub), lambda b: (b, 0, 0))],
        out_shape=[jax.ShapeDtypeStruct((B, nsub, 128), bf16), jax.ShapeDtypeStruct((B, 128, nsub), bf16)],
        compiler_params=_params("parallel"),
        name="nsa_compress",
    )(kc, vc, pek, w1k, w2k, pev, w1v, w2vT)


def _nsa_kernel(qT_ref, gT_ref, kcmp_ref, vcmpT_ref, ks_ref, vsT_ref, kw_ref, vwT_ref, covT_ref, eblk_ref,
                o_ref, qaug_ref, s_ref, acc_ref, m_ref, smax_ref, *, n_slc, n_top):
    i = pl.program_id(1)
    q0 = i * Q_BLOCK
    HQ = NSA_HPG * Q_BLOCK
    GROUPS = range(NSA_KV_HEADS)
    DH = NSA_HEAD_DIM

    def per_head(m):
        return jnp.concatenate([m] * NSA_HPG, axis=1)

    def vrows(g):
        return slice(g * DH, (g + 1) * DH)

    def padded_q(g):
        base = g * NSA_HPG * DH
        qh = jnp.concatenate([qT_ref[base + hp * DH:base + (hp + 1) * DH, :] for hp in range(NSA_HPG)], axis=1)
        zq = jnp.zeros_like(qh)
        return jnp.concatenate([qh, zq] if g == 0 else [zq, qh], axis=0)

    qpad = [padded_q(g) for g in GROUPS]
    t_row = q0 + lax.broadcasted_iota(jnp.int32, (1, Q_BLOCK), 1)

    ncmp = kcmp_ref.shape[1]

    def compressed(rows):
        n_iota = lax.broadcasted_iota(jnp.int32, (rows, Q_BLOCK), 0)
        bias_c = per_head(jnp.where((n_iota * CMP_STRIDE + (CMP_BLOCK - 1)) <= t_row, 0.0, NEG_INF))
        has_key = per_head(t_row) >= CMP_BLOCK - 1
        kc = kcmp_ref[0, 0:rows, :]
        cov = covT_ref[:, 0:rows]
        out = []
        for g in GROUPS:
            sc = jnp.dot(kc, qpad[g], preferred_element_type=f32) + bias_c
            pc = jnp.exp2(sc - jnp.max(sc, axis=0, keepdims=True))
            lc = jnp.sum(pc, axis=0, keepdims=True)
            pc = pc * jnp.where(has_key, 1.0 / lc, 0.0)
            o = jnp.dot(vcmpT_ref[0, vrows(g), 0:rows], pc.astype(bf16), preferred_element_type=f32)
            psum = pc[:, 0:Q_BLOCK]
            for hp in range(1, NSA_HPG):
                psum = psum + pc[:, hp * Q_BLOCK:(hp + 1) * Q_BLOCK]
            hi = psum.astype(bf16)
            r1 = psum - hi.astype(f32)
            mid = r1.astype(bf16)
            lo = (r1 - mid.astype(f32)).astype(bf16)
            out += [o, (jnp.dot(cov, hi, preferred_element_type=f32) + jnp.dot(cov, mid, preferred_element_type=f32)
                        + jnp.dot(cov, lo, preferred_element_type=f32))]
        return tuple(out)

    row_options = [r for r in (ncmp // 4, ncmp // 2, 3 * ncmp // 4, ncmp) if r % 128 == 0]
    need = (Q_BLOCK // CMP_STRIDE) * (i + 1)
    which = sum((need > r).astype(jnp.int32) for r in row_options[:-1])
    cmp_out = lax.switch(which, [functools.partial(compressed, r) for r in row_options])
    oc = [cmp_out[2 * g] for g in GROUPS]
    imp = [cmp_out[2 * g + 1] for g in GROUPS]

    j_i = lax.broadcasted_iota(jnp.int32, (n_slc, Q_BLOCK), 0)
    j_f = j_i.astype(f32)
    cur = t_row >> 6
    forced = (j_i == 0) | (j_i == cur) | (j_i == cur - 1)
    visible = (j_i * SLC_BLOCK) <= t_row
    score = [jnp.where(forced, TAKEN_SCORE, jnp.where(visible, imp[g], NEG_INF)) for g in GROUPS]

    w0 = pl.multiple_of(jnp.maximum(q0 - WINDOW, 0), Q_BLOCK)
    kp = w0 + lax.broadcasted_iota(jnp.int32, (WIN_KEYS, Q_BLOCK), 0)
    bias_w = per_head(jnp.where((kp <= t_row) & (kp > (t_row - WINDOW)), 0.0, NEG_INF))
    kwin = kw_ref[pl.ds(w0, WIN_KEYS), :]
    ow = []
    for g in GROUPS:
        sw = jnp.dot(kwin, qpad[g], preferred_element_type=f32) + bias_w
        pw = jnp.exp2(sw - jnp.max(sw, axis=0, keepdims=True)).astype(bf16)
        vw1 = jnp.concatenate([vwT_ref[vrows(g), pl.ds(w0, WIN_KEYS)], jnp.ones((ONES_ROWS, WIN_KEYS), bf16)],
                              axis=0)
        o = jnp.dot(vw1, pw, preferred_element_type=f32)
        ow.append(o[0:DH, :] * (1.0 / o[DH:DH + 1, :]))

    sel = [jnp.where(forced, 1.0, 0.0) for g in GROUPS]
    for _ in range(n_top - 3):
        for g in GROUPS:
            mx = jnp.max(score[g], axis=0, keepdims=True)
            first = jnp.min(jnp.where(score[g] == mx, j_f, float(n_slc)), axis=0, keepdims=True)
            hit = j_f == first
            sel[g] = jnp.where(hit, 1.0, sel[g])
            score[g] = jnp.where(hit, TAKEN_SCORE, score[g])

    for g in GROUPS:
        qaug_ref[g, 0:2 * DH, :] = qpad[g]
        blk_bias = jnp.where((sel[g] > 0.0) & visible, 0.0, NEG_INF)
        qaug_ref[g, 2 * DH:2 * DH + n_slc, :] = per_head(blk_bias).astype(bf16)
        if n_slc < 128:
            qaug_ref[g, 2 * DH + n_slc:, :] = jnp.zeros((128 - n_slc, HQ), bf16)

    def scores(c):
        k0 = pl.multiple_of(c * SEL_CHUNK, SEL_CHUNK)
        kaug = jnp.concatenate([ks_ref[pl.ds(k0, SEL_CHUNK), :], eblk_ref[pl.ds(k0, SEL_CHUNK), :]], axis=1)
        return [jnp.dot(kaug, qaug_ref[g], preferred_element_type=f32) for g in GROUPS]

    def attend(g, c, s, smax):
        k0 = pl.multiple_of(c * SEL_CHUNK, SEL_CHUNK)
        m = m_ref[g]
        m_new = jnp.maximum(m, smax)
        m_ref[g] = m_new
        p = jnp.exp2(s - m_new).astype(bf16)
        v1 = jnp.concatenate([vsT_ref[vrows(g), pl.ds(k0, SEL_CHUNK)], jnp.ones((ONES_ROWS, SEL_CHUNK), bf16)],
                             axis=0)
        acc_ref[g] = jnp.exp2(m - m_new) * acc_ref[g] + jnp.dot(v1, p, preferred_element_type=f32)

    def step(c, cur, nxt):
        smax = [smax_ref[g] for g in GROUPS]
        s_next = scores(c + 1)
        for g in GROUPS:
            s_ref[g, nxt] = s_next[g]
            smax_ref[g] = jnp.max(s_next[g], axis=0, keepdims=True)
        for g in GROUPS:
            attend(g, c, s_ref[g, cur], smax[g])

    last = q0 // SEL_CHUNK
    odd = last % 2
    s0 = scores(0)
    for g in GROUPS:
        s_ref[g, odd] = s0[g]
        smax_ref[g] = jnp.max(s0[g], axis=0, keepdims=True)
        acc_ref[g] = jnp.zeros((DH + ONES_ROWS, HQ), f32)
        m_ref[g] = jnp.full((1, HQ), NEG_INF, f32)

    @pl.when(odd == 1)
    def _():
        step(0, 1, 0)

    has_pair = (last // 2) % 2

    @pl.when(has_pair == 1)
    def _():
        step(odd, 0, 1)
        step(odd + 1, 1, 0)

    def quad(qq, carry):
        c = odd + 2 * has_pair + 4 * qq
        step(c, 0, 1)
        step(c + 1, 1, 0)
        step(c + 2, 0, 1)
        step(c + 3, 1, 0)
        return carry

    lax.fori_loop(0, last // 4, quad, 0)

    kp_d = last * SEL_CHUNK + lax.broadcasted_iota(jnp.int32, (SEL_CHUNK, Q_BLOCK), 0)
    bias_d = per_head(jnp.where(kp_d <= t_row, 0.0, NEG_INF))
    for g in GROUPS:
        s_d = s_ref[g, 0] + bias_d
        attend(g, last, s_d, jnp.max(s_d, axis=0, keepdims=True))

    gate = _sigmoid(gT_ref[...])
    for g in GROUPS:
        osel = acc_ref[g, 0:DH, :] * (1.0 / acc_ref[g, DH:DH + 1, :])
        for hp in range(NSA_HPG):
            cols = slice(hp * Q_BLOCK, (hp + 1) * Q_BLOCK)
            gr = g * GATE_ROWS + hp
            o = (gate[gr:gr + 1, :] * oc[g][:, cols] + gate[gr + 4:gr + 5, :] * osel[:, cols]
                 + gate[gr + 8:gr + 9, :] * ow[g][:, cols])
            o_ref[(g * NSA_HPG + hp) * DH:(g * NSA_HPG + hp + 1) * DH, :] = o.astype(bf16)


def _nsa(qT, gT, kcmp, vcmpT, ks, vsT, kw, vwT, covT, eblk, B, S):
    n_q = S // Q_BLOCK
    n_slc = S // SLC_BLOCK
    ncmp = S // CMP_STRIDE
    G = NSA_KV_HEADS
    HQ = NSA_HPG * Q_BLOCK
    tok = lambda r: pl.BlockSpec((r, Q_BLOCK), lambda b, i: (0, b * n_q + i))
    per_b_nat = pl.BlockSpec((S, 128), lambda b, i: (b, 0))
    per_b_tr = pl.BlockSpec((128, S), lambda b, i: (0, b))
    return pl.pallas_call(
        functools.partial(_nsa_kernel, n_slc=n_slc, n_top=min(N_SELECT, n_slc)),
        grid=(B, n_q),
        in_specs=[tok(NSA_HEADS * NSA_HEAD_DIM), tok(G * GATE_ROWS),
                  pl.BlockSpec((1, ncmp, 128), lambda b, i: (b, 0, 0)),
                  pl.BlockSpec((1, 128, ncmp), lambda b, i: (b, 0, 0)),
                  per_b_nat, per_b_tr, per_b_nat, per_b_tr,
                  _const_spec((n_slc, ncmp)), _const_spec((S, 128))],
        out_specs=tok(NSA_HEADS * NSA_HEAD_DIM),
        out_shape=jax.ShapeDtypeStruct((NSA_HEADS * NSA_HEAD_DIM, B * S), bf16),
        scratch_shapes=[pltpu.VMEM((G, 2 * NSA_HEAD_DIM + 128, HQ), bf16),
                        pltpu.VMEM((G, 2, SEL_CHUNK, HQ), f32),
                        pltpu.VMEM((G, NSA_HEAD_DIM + ONES_ROWS, HQ), f32),
                        pltpu.VMEM((G, 1, HQ), f32), pltpu.VMEM((G, 1, HQ), f32)],
        compiler_params=_params("parallel", "arbitrary"),
        name="nsa_attn",
    )(qT, gT, kcmp, vcmpT, ks, vsT, kw, vwT, covT, eblk)


def _memkv_kernel(mem_ref, wk_ref, wvT_ref, k_ref, vT_ref):
    m = mem_ref[0].astype(bf16)
    k_ref[0] = jnp.dot(m, wk_ref[...], preferred_element_type=f32).astype(bf16)
    vT_ref[0] = _nt(wvT_ref[...], m).astype(bf16)


def _memkv(mem, wk, wvT):
    B, M, _ = mem.shape
    return pl.pallas_call(
        _memkv_kernel,
        grid=(B,),
        in_specs=[pl.BlockSpec((1, M, D_MODEL), lambda b: (b, 0, 0)),
                  _const_spec((D_MODEL, 512)), _const_spec((512, D_MODEL))],
        out_specs=[pl.BlockSpec((1, M, 512), lambda b: (b, 0, 0)),
                   pl.BlockSpec((1, 512, M), lambda b: (b, 0, 0))],
        out_shape=[jax.ShapeDtypeStruct((B, M, 512), bf16), jax.ShapeDtypeStruct((B, 512, M), bf16)],
        compiler_params=_params("parallel"),
        name="mem_kv",
    )(mem, wk, wvT)


def _memattn_kernel(qT_ref, k_ref, vT_ref, o_ref):
    for h in range(MEM_HEADS):
        rows = slice(h * MEM_HEAD_DIM, (h + 1) * MEM_HEAD_DIM)
        s = jnp.dot(k_ref[0, :, rows], qT_ref[rows, :], preferred_element_type=f32) * (MEM_HEAD_DIM ** -0.5)
        p = jnp.exp(s - jnp.max(s, axis=0, keepdims=True))
        l = jnp.sum(p, axis=0, keepdims=True)
        o = jnp.dot(vT_ref[0, rows, :], p.astype(bf16), preferred_element_type=f32) * (1.0 / l)
        o_ref[rows, :] = o.astype(bf16)


def _memattn(mqT, mk, mvT, B, S, tq):
    M = mk.shape[1]
    nq = S // tq
    return pl.pallas_call(
        _memattn_kernel,
        grid=(B, nq),
        in_specs=[pl.BlockSpec((512, tq), lambda b, i: (0, b * nq + i)),
                  pl.BlockSpec((1, M, 512), lambda b, i: (b, 0, 0)),
                  pl.BlockSpec((1, 512, M), lambda b, i: (b, 0, 0))],
        out_specs=pl.BlockSpec((512, tq), lambda b, i: (0, b * nq + i)),
        out_shape=jax.ShapeDtypeStruct((512, B * S), bf16),
        compiler_params=_params("parallel", "parallel"),
        name="mem_attn",
    )(mqT, mk, mvT)


def _outproj_kernel(h0_ref, ol_ref, onT_ref, omT_ref, wo_ref, g1_ref, b1_ref, h1_ref, *, tm):
    for r0 in range(0, tm, OUT_SUB_ROWS):
        rows = slice(r0, r0 + OUT_SUB_ROWS)
        mixed = jnp.dot(ol_ref[rows, :], wo_ref[0:1024, :], preferred_element_type=f32)
        mixed = mixed + _tn(onT_ref[:, rows], wo_ref[1024:1536, :])
        mixed = mixed + _tn(omT_ref[:, rows], wo_ref[1536:2048, :])
        h1_ref[rows, :] = _ln(ALPHA * h0_ref[rows, :] + mixed, g1_ref[...], b1_ref[...])


def _outproj(h0, ol, onT, omT, wo, g1, b1, tm):
    T = h0.shape[0]
    row = lambda w: pl.BlockSpec((tm, w), lambda i: (i, 0))
    trs = pl.BlockSpec((512, tm), lambda i: (0, i))
    vec = _const_spec((1, D_MODEL))
    return pl.pallas_call(
        functools.partial(_outproj_kernel, tm=tm),
        grid=(T // tm,),
        in_specs=[row(D_MODEL), row(1024), trs, trs, _const_spec((D_MODEL, D_MODEL)), vec, vec],
        out_specs=row(D_MODEL),
        out_shape=jax.ShapeDtypeStruct((T, D_MODEL), f32),
        compiler_params=_params("parallel"),
        name="out_proj_ln1",
    )(h0, ol, onT, omT, wo, g1, b1)


def _ffn_kernel(h_ref, halo_ref, wg_ref, wu_ref, cwg_ref, cwu_ref, cbg_ref, cbu_ref, wd_ref,
                g2_ref, b2_ref, o_ref, lhs_ref, *, tm, blocks_per_seq):
    i = pl.program_id(0)
    j = pl.program_id(1)

    @pl.when(j == 0)
    def _():
        first = (i % blocks_per_seq) == 0
        lhs_ref[0:8, :] = jnp.where(first, 0.0, halo_ref[...]).astype(bf16)
        h = h_ref[...]
        lhs_ref[8:8 + tm, :] = h.astype(bf16)
        o_ref[...] = ALPHA * h

    lhs = lhs_ref[...]

    def conv(w_ref, cw_ref, cb_ref):
        up = jnp.dot(lhs, w_ref[...], preferred_element_type=f32)
        y = cb_ref[...] + pltpu.roll(up, 2, 0)[8:8 + tm, :] * cw_ref[0:1, :]
        y = y + pltpu.roll(up, 1, 0)[8:8 + tm, :] * cw_ref[1:2, :]
        return y + up[8:8 + tm, :] * cw_ref[2:3, :]

    act = (_gelu(conv(wg_ref, cwg_ref, cbg_ref)) * conv(wu_ref, cwu_ref, cbu_ref)).astype(bf16)
    o_ref[...] += jnp.dot(act, wd_ref[...], preferred_element_type=f32)

    @pl.when(j == pl.num_programs(1) - 1)
    def _():
        o_ref[...] = _ln(o_ref[...], g2_ref[...], b2_ref[...])


def _ffn(h1, wup, cw, cb, wd, g2, b2, tm, tf, blocks_per_seq):
    T = h1.shape[0]
    nf = D_FF // tf
    vec = _const_spec((1, D_MODEL))
    return pl.pallas_call(
        functools.partial(_ffn_kernel, tm=tm, blocks_per_seq=blocks_per_seq),
        grid=(T // tm, nf),
        in_specs=[pl.BlockSpec((tm, D_MODEL), lambda i, j: (i, 0)),
                  pl.BlockSpec((8, D_MODEL), lambda i, j: (jnp.maximum(i * (tm // 8) - 1, 0), 0)),
                  pl.BlockSpec((D_MODEL, tf), lambda i, j: (0, j)),
                  pl.BlockSpec((D_MODEL, tf), lambda i, j: (0, nf + j)),
                  pl.BlockSpec((FFN_CONV_WIDTH, tf), lambda i, j: (0, j)),
                  pl.BlockSpec((FFN_CONV_WIDTH, tf), lambda i, j: (0, nf + j)),
                  pl.BlockSpec((1, tf), lambda i, j: (0, j)),
                  pl.BlockSpec((1, tf), lambda i, j: (0, nf + j)),
                  pl.BlockSpec((tf, D_MODEL), lambda i, j: (j, 0)),
                  vec, vec],
        out_specs=pl.BlockSpec((tm, D_MODEL), lambda i, j: (i, 0)),
        out_shape=jax.ShapeDtypeStruct((T, D_MODEL), f32),
        scratch_shapes=[pltpu.VMEM((tm + 8, D_MODEL), bf16)],
        compiler_params=_params("parallel", "arbitrary", vmem_limit=V7X_VMEM_LIMIT_FFN_BYTES),
        name="conv_ffn_ln2",
    )(h1, h1, wup, wup, cw, cw, cb, cb, wd, g2, b2)


def _layer(x, mem, ln_in_g, ln_in_b, w_in, lru_conv_w, lru_conv_b, lru_wa, lru_ba, lru_wx, lru_bx,
           lru_lam, cmp_pe_k, cmp_w1_k, cmp_w2_k, cmp_pe_v, cmp_w1_v, cmp_w2_v, w_mem_kv, w_out,
           ln1_g, ln1_b, ffn_w_up, ffn_conv_w, ffn_conv_b, ffn_w_down, ln2_g, ln2_b,
           *, tm_proj, tb_lru, tq_mem, tm_out, tm_ffn, tf_ffn):
    B, S, _ = x.shape
    T = B * S
    G, Dh = NSA_KV_HEADS, NSA_HEAD_DIM
    row = lambda v: v.reshape(1, -1)

    c_q = 2 * LRU_WIDTH
    c_kv = c_q + NSA_HEADS * Dh
    c_gate = c_kv + 6 * G * Dh
    c_mq = c_gate + 3 * NSA_HEADS
    assert (c_q, c_kv, c_gate) == (_NQ0, _NKC0, _NCOLS)
    wn = w_in.astype(bf16)
    wmq = wn[:, c_mq:]
    w_gate = w_in[:, c_gate:c_mq].reshape(D_MODEL, G, NSA_HPG, 3)
    w_gate = jnp.pad(w_gate.transpose(1, 3, 2, 0), ((0, 0), (0, 1), (0, 0), (0, 0)))
    wgT = w_gate.reshape(G * GATE_ROWS, D_MODEL).astype(bf16)

    wax = jnp.concatenate([lru_wa, lru_wx], axis=-1).astype(bf16)

    def cmp_weights(pe, w1, w2):
        pe2 = jnp.broadcast_to(pe.reshape(2, CMP_STRIDE, 1, Dh), (2, CMP_STRIDE, G, Dh)).reshape(2, -1)
        def group_diag(w, axis):
            z = jnp.zeros_like(w)
            return jnp.stack([jnp.concatenate([w if k == g else z for k in range(G)], axis=-1)
                              for g in range(G)], axis=axis)

        w1e = group_diag(w1.reshape(2, CMP_STRIDE, Dh, Dh), 2)
        w1e = w1e.reshape(2, CMP_STRIDE * G * Dh, G * Dh).astype(bf16)
        w2e = group_diag(w2, 0).reshape(G * Dh, G * Dh)
        return pe2, w1e, w2e

    pek, w1k, w2k = cmp_weights(cmp_pe_k, cmp_w1_k, cmp_w2_k)
    pev, w1v, w2v = cmp_weights(cmp_pe_v, cmp_w1_v, cmp_w2_v)

    n_slc, ncmp = S // SLC_BLOCK, S // CMP_STRIDE
    ci = jnp.arange(ncmp)[None, :] * CMP_STRIDE
    sj = jnp.arange(n_slc)[:, None] * SLC_BLOCK
    covT = ((ci <= sj + SLC_BLOCK - 1) & (ci + CMP_BLOCK - 1 >= sj)).astype(bf16)
    eblk = (jnp.arange(S)[:, None] // SLC_BLOCK == jnp.arange(128)[None, :]).astype(bf16)

    wk_mem = w_mem_kv[:, :512].astype(bf16)
    wvT_mem = w_mem_kv[:, 512:].T.astype(bf16)
    wo = w_out.astype(bf16)
    wup = ffn_w_up.astype(bf16)
    wd = ffn_w_down.astype(bf16)

    x2 = x.reshape(T, D_MODEL)
    lx, ly, kc, vc, ks, kw, qT, vsT, vwT, mqT, gT, h0 = _proj(x2, row(ln_in_g), row(ln_in_b), wn, wmq, wgT, tm_proj)

    o_lru = _lru(lx.reshape(B, S, LRU_WIDTH), ly.reshape(B, S, LRU_WIDTH), lru_conv_w, row(lru_conv_b),
                 wax, row(lru_ba), row(lru_bx), row(lru_lam), tb_lru)

    kcmp, vcmpT = _compress(kc, vc, pek, w1k, w2k.astype(bf16), pev, w1v, w2v.T.astype(bf16), B, S)
    o_nsaT = _nsa(qT, gT, kcmp, vcmpT, ks, vsT, kw, vwT, covT, eblk, B, S)

    mk, mvT = _memkv(mem, wk_mem, wvT_mem)
    o_memT = _memattn(mqT, mk, mvT, B, S, tq_mem)

    h1 = _outproj(h0, o_lru.reshape(T, LRU_WIDTH), o_nsaT, o_memT, wo, row(ln1_g), row(ln1_b), tm_out)
    out = _ffn(h1, wup, ffn_conv_w, row(ffn_conv_b), wd, row(ln2_g), row(ln2_b), tm_ffn, tf_ffn, S // tm_ffn)
    return out.reshape(B, S, D_MODEL)


def kernel(x, mem, ln_in_g, ln_in_b, w_in, lru_conv_w, lru_conv_b, lru_wa, lru_ba, lru_wx, lru_bx,
           lru_lam, cmp_pe_k, cmp_w1_k, cmp_w2_k, cmp_pe_v, cmp_w1_v, cmp_w2_v, w_mem_kv, w_out,
           ln1_g, ln1_b, ffn_w_up, ffn_conv_w, ffn_conv_b, ffn_w_down, ln2_g, ln2_b):
    return _layer(x, mem, ln_in_g, ln_in_b, w_in[0], lru_conv_w[0], lru_conv_b[0], lru_wa[0], lru_ba[0],
                  lru_wx[0], lru_bx[0], lru_lam[0], cmp_pe_k[0], cmp_w1_k[0], cmp_w2_k[0], cmp_pe_v[0],
                  cmp_w1_v[0], cmp_w2_v[0], w_mem_kv[0], w_out[0], ln1_g[0], ln1_b[0], ffn_w_up[0],
                  ffn_conv_w[0], ffn_conv_b[0], ffn_w_down[0], ln2_g[0], ln2_b[0],
                  tm_proj=512, tb_lru=256, tq_mem=1024, tm_out=1024, tm_ffn=1024, tf_ffn=512)
```

```python
import functools
import math

import jax
import jax.numpy as jnp
from jax import lax
from jax.experimental import pallas as pl
from jax.experimental.pallas import tpu as pltpu

f32 = jnp.float32
bf16 = jnp.bfloat16

D_MODEL = 2048
LRU_WIDTH = 1024
LRU_BLOCKS = 8
LRU_CONV_WIDTH = 4
LRU_C = 8.0
LRU_TINY = 1e-30
NSA_HEADS = 8
NSA_KV_HEADS = 2
NSA_HPG = NSA_HEADS // NSA_KV_HEADS
NSA_HEAD_DIM = 64
NSA_Q_SCALE = NSA_HEAD_DIM ** -0.5 * math.log2(math.e)
CMP_STRIDE = 16
CMP_BLOCK = 32
SLC_BLOCK = 64
N_SELECT = 16
WINDOW = 512
Q_BLOCK = 128
MEM_HEADS = 4
MEM_HEAD_DIM = 128
D_FF = 5632
FFN_CONV_WIDTH = 3
LN_EPS = 1e-5
NEG_INF = -1e30
TAKEN_SCORE = -3e38
ALPHA = 2.0 ** 0.25

V7X_VMEM_LIMIT_BYTES = 56 * 1024 * 1024
V7X_VMEM_LIMIT_FFN_BYTES = 61 * 1024 * 1024

_NX0, _NY0, _NQ0, _NKC0, _NKS0, _NCOLS = 0, 1024, 2048, 2560, 2816, 3328

SEL_CHUNK = 512
WIN_KEYS = WINDOW + Q_BLOCK
GATE_ROWS = 16
OUT_SUB_ROWS = 256
ONES_ROWS = 16


def _ln(x, g, b):
    mu = jnp.mean(x, axis=-1, keepdims=True)
    xc = x - mu
    var = jnp.mean(xc * xc, axis=-1, keepdims=True)
    return xc * lax.rsqrt(var + LN_EPS) * g + b


def _gelu(x):
    return jax.nn.gelu(x)


def _sigmoid(x):
    return 1.0 / (1.0 + jnp.exp(-x))


def _nt(a, b):
    return lax.dot_general(a, b, (((1,), (1,)), ((), ())), preferred_element_type=f32)


def _tn(a, b):
    return lax.dot_general(a, b, (((0,), (0,)), ((), ())), preferred_element_type=f32)


def _params(*sem, vmem_limit=V7X_VMEM_LIMIT_BYTES):
    return pltpu.CompilerParams(dimension_semantics=sem, vmem_limit_bytes=vmem_limit)


def _const_spec(shape):
    nd = len(shape)
    return pl.BlockSpec(shape, lambda *_: (0,) * nd, pipeline_mode=pl.Buffered(1))


def _proj_kernel(x_ref, g_ref, b_ref, wn_ref, wmq_ref, wgT_ref,
                 lx_ref, ly_ref, kc_ref, vc_ref, ks_ref, kw_ref,
                 qT_ref, vsT_ref, vwT_ref, mqT_ref, gT_ref, h0_ref):
    half = x_ref.shape[0] // 2
    hs = []
    for r0 in (0, half):
        h0 = _ln(x_ref[r0:r0 + half, :], g_ref[...], b_ref[...])
        h0_ref[r0:r0 + half, :] = h0
        hs.append(h0.astype(bf16))
        lx_ref[r0:r0 + half, :] = jnp.dot(hs[-1], wn_ref[:, _NX0:_NY0], preferred_element_type=f32)
    h = jnp.concatenate(hs, axis=0)

    def nat(c0, c1):
        return jnp.dot(h, wn_ref[:, c0:c1], preferred_element_type=f32)

    ly_ref[...] = nat(_NY0, _NQ0)
    qT_ref[...] = (nat(_NQ0, _NKC0) * NSA_Q_SCALE).T.astype(bf16)
    kvc = nat(_NKC0, _NKS0)
    kc_ref[...] = kvc[:, 0:128]
    vc_ref[...] = kvc[:, 128:256]
    kvsw = nat(_NKS0, _NCOLS)
    ks_ref[...] = kvsw[:, 0:128].astype(bf16)
    vsT_ref[...] = kvsw[:, 128:256].T.astype(bf16)
    kw_ref[...] = kvsw[:, 256:384].astype(bf16)
    vwT_ref[...] = kvsw[:, 384:512].T.astype(bf16)
    mqT_ref[...] = jnp.dot(h, wmq_ref[...], preferred_element_type=f32).T.astype(bf16)
    gT_ref[...] = _nt(wgT_ref[...], h)


def _proj(x2, g, b, wn, wmq, wgT, tm):
    T = x2.shape[0]
    nat = lambda w: pl.BlockSpec((tm, w), lambda i: (i, 0))
    trs = lambda r: pl.BlockSpec((r, tm), lambda i: (0, i))
    return pl.pallas_call(
        _proj_kernel,
        grid=(T // tm,),
        in_specs=[pl.BlockSpec((tm, D_MODEL), lambda i: (i, 0)),
                  _const_spec((1, D_MODEL)), _const_spec((1, D_MODEL)),
                  _const_spec(wn.shape), _const_spec((D_MODEL, MEM_HEADS * MEM_HEAD_DIM)),
                  _const_spec((NSA_KV_HEADS * GATE_ROWS, D_MODEL))],
        out_specs=[nat(1024), nat(1024), nat(128), nat(128), nat(128), nat(128),
                   trs(512), trs(128), trs(128), trs(512), trs(32), nat(D_MODEL)],
        out_shape=[jax.ShapeDtypeStruct((T, 1024), f32), jax.ShapeDtypeStruct((T, 1024), f32),
                   jax.ShapeDtypeStruct((T, 128), f32), jax.ShapeDtypeStruct((T, 128), f32),
                   jax.ShapeDtypeStruct((T, 128), bf16), jax.ShapeDtypeStruct((T, 128), bf16),
                   jax.ShapeDtypeStruct((512, T), bf16), jax.ShapeDtypeStruct((128, T), bf16),
                   jax.ShapeDtypeStruct((128, T), bf16), jax.ShapeDtypeStruct((512, T), bf16),
                   jax.ShapeDtypeStruct((32, T), f32), jax.ShapeDtypeStruct((T, D_MODEL), f32)],
        compiler_params=_params("parallel"),
        name="proj",
    )(x2, g, b, wn, wmq, wgT)


def _lru_pitch(tb):
    seg = tb // 8
    return seg + 8 if (seg // 8) % 2 == 0 else seg


def _lru_kernel(x_ref, y_ref, cw_ref, cb_ref, wax_ref, ba_ref, bx_ref, lam_ref, o_ref,
                xext_ref, a_ref, u_ref, carry_ref, *, tb):
    s = pl.program_id(1)

    @pl.when(s == 0)
    def _():
        xext_ref[0:8, :] = jnp.zeros((8, LRU_WIDTH), f32)
        carry_ref[...] = jnp.zeros((1, LRU_WIDTH), f32)

    x = x_ref[0]
    xext_ref[8:8 + tb, :] = x
    xe = xext_ref[...]
    xc = cb_ref[...]
    for k in range(LRU_CONV_WIDTH):
        back = LRU_CONV_WIDTH - 1 - k
        shifted = pltpu.roll(xe, back, 0)[8:8 + tb, :] if back else x
        xc = xc + shifted * cw_ref[k:k + 1, :]
    xext_ref[0:8, :] = x[tb - 8:tb, :]

    xb = xc.astype(bf16)
    gates = [jnp.dot(xb[:, n * 128:(n + 1) * 128], wax_ref[n], preferred_element_type=f32)
             for n in range(LRU_BLOCKS)]
    r_t = jnp.tanh(jnp.concatenate([gt[:, 0:128] for gt in gates], axis=1) + ba_ref[...])
    i = 0.5 * jnp.tanh(jnp.concatenate([gt[:, 128:256] for gt in gates], axis=1) + bx_ref[...]) + 0.5
    nl = -lam_ref[...]
    softplus = jnp.maximum(nl, 0.0) + jnp.log1p(jnp.exp(-jnp.abs(nl)))
    half_c = (-0.5 * LRU_C) * softplus
    log_a = half_c * r_t + half_c
    a = jnp.exp(log_a)
    th = jnp.tanh(log_a)
    z = -2.0 * th / (1.0 - th)
    u = (z * lax.rsqrt(jnp.maximum(z, LRU_TINY))) * (i * xc)
    seg = tb // 8
    pitch = _lru_pitch(tb)
    for n in range(LRU_BLOCKS):
        for r8 in range(8):
            a_ref[n, r8 * pitch:r8 * pitch + seg, :] = a[r8 * seg:(r8 + 1) * seg, n * 128:(n + 1) * 128]
            u_ref[n, r8 * pitch:r8 * pitch + seg, :] = u[r8 * seg:(r8 + 1) * seg, n * 128:(n + 1) * 128]

    def sweep(k, carry):
        rows = pl.ds(k, 8, stride=pitch)
        out = []
        for n in range(LRU_BLOCKS):
            h, p = carry[n]
            ak = a_ref[n, rows, :]
            h = ak * h + u_ref[n, rows, :]
            p = ak * p
            u_ref[n, rows, :] = h
            a_ref[n, rows, :] = p
            out.append((h, p))
        return tuple(out)

    init = tuple((jnp.zeros((8, 128), f32), jnp.ones((8, 128), f32)) for _ in range(LRU_BLOCKS))
    ends = lax.fori_loop(0, seg, sweep, init, unroll=4)
    h_end = jnp.concatenate([e[0] for e in ends], axis=1)
    p_end = jnp.concatenate([e[1] for e in ends], axis=1)

    c = carry_ref[...]
    for r8 in range(8):
        rows = slice(r8 * seg, (r8 + 1) * seg)
        prow = slice(r8 * pitch, r8 * pitch + seg)
        h_loc = jnp.concatenate([u_ref[n, prow, :] for n in range(LRU_BLOCKS)], axis=1)
        p_cum = jnp.concatenate([a_ref[n, prow, :] for n in range(LRU_BLOCKS)], axis=1)
        o_ref[0, rows, :] = (_gelu(y_ref[0, rows, :]) * (h_loc + p_cum * c)).astype(bf16)
        c = h_end[r8:r8 + 1, :] + p_end[r8:r8 + 1, :] * c
    carry_ref[...] = c


def _lru(lx, ly, cw, cb, wax, ba, bx, lam, tb):
    B, S, _ = lx.shape
    blk = pl.BlockSpec((1, tb, LRU_WIDTH), lambda b, s: (b, s, 0))
    return pl.pallas_call(
        functools.partial(_lru_kernel, tb=tb),
        grid=(B, S // tb),
        in_specs=[blk, blk, _const_spec((LRU_CONV_WIDTH, LRU_WIDTH)), _const_spec((1, LRU_WIDTH)),
                  _const_spec((LRU_BLOCKS, 128, 256)), _const_spec((1, LRU_WIDTH)),
                  _const_spec((1, LRU_WIDTH)), _const_spec((1, LRU_WIDTH))],
        out_specs=blk,
        out_shape=jax.ShapeDtypeStruct((B, S, LRU_WIDTH), bf16),
        scratch_shapes=[pltpu.VMEM((tb + 8, LRU_WIDTH), f32),
                        pltpu.VMEM((LRU_BLOCKS, 8 * _lru_pitch(tb), 128), f32),
                        pltpu.VMEM((LRU_BLOCKS, 8 * _lru_pitch(tb), 128), f32),
                        pltpu.VMEM((1, LRU_WIDTH), f32)],
        compiler_params=_params("parallel", "arbitrary"),
        name="lru",
    )(lx, ly, cw, cb, wax, ba, bx, lam)


def _cmp_kernel(kc_ref, vc_ref, pek_ref, w1k_ref, w2k_ref, pev_ref, w1v_ref, w2vT_ref,
                kcmp_ref, vcmpT_ref):
    nsub = kc_ref.shape[0] // CMP_STRIDE

    def hidden(x_ref, pe_ref, w1_ref):
        sub = jnp.concatenate([x_ref[pl.ds(l, nsub, stride=CMP_STRIDE), :] for l in range(CMP_STRIDE)], axis=1)
        y0 = jnp.dot((sub + pe_ref[0:1, :]).astype(bf16), w1_ref[0], preferred_element_type=f32)
        y1 = jnp.dot((sub + pe_ref[1:2, :]).astype(bf16), w1_ref[1], preferred_element_type=f32)
        return _gelu(y0 + pltpu.roll(y1, nsub - 1, 0)).astype(bf16)

    kcmp_ref[0] = jnp.dot(hidden(kc_ref, pek_ref, w1k_ref), w2k_ref[...],
                          preferred_element_type=f32).astype(bf16)
    vcmpT_ref[0] = _nt(w2vT_ref[...], hidden(vc_ref, pev_ref, w1v_ref)).astype(bf16)


def _compress(kc, vc, pek, w1k, w2k, pev, w1v, w2vT, B, S):
    nsub = S // CMP_STRIDE
    width = pek.shape[1]
    blk = pl.BlockSpec((S, 128), lambda b: (b, 0))
    return pl.pallas_call(
        _cmp_kernel,
        grid=(B,),
        in_specs=[blk, blk, _const_spec((2, width)), _const_spec((2, width, 128)), _const_spec((128, 128)),
                  _const_spec((2, width)), _const_spec((2, width, 128)), _const_spec((128, 128))],
        out_specs=[pl.BlockSpec((1, nsub, 128), lambda b: (b, 0, 0)),
                   pl.BlockSpec((1, 128, nsub), lambda b: (b, 0, 0))],
        out_shape=[jax.ShapeDtypeStruct((B, nsub, 128), bf16), jax.ShapeDtypeStruct((B, 128, nsub), bf16)],
        compiler_params=_params("parallel"),
        name="nsa_compress",
    )(kc, vc, pek, w1k, w2k, pev, w1v, w2vT)


def _nsa_kernel(qT_ref, gT_ref, kcmp_ref, vcmpT_ref, ks_ref, vsT_ref, kw_ref, vwT_ref, covT_ref, eblk_ref,
                o_ref, qaug_ref, s_ref, acc_ref, m_ref, smax_ref, *, n_slc, n_top):
    i = pl.program_id(1)
    q0 = i * Q_BLOCK
    HQ = NSA_HPG * Q_BLOCK
    GROUPS = range(NSA_KV_HEADS)
    DH = NSA_HEAD_DIM

    def per_head(m):
        return jnp.concatenate([m] * NSA_HPG, axis=1)

    def vrows(g):
        return slice(g * DH, (g + 1) * DH)

    def padded_q(g):
        base = g * NSA_HPG * DH
        qh = jnp.concatenate([qT_ref[base + hp * DH:base + (hp + 1) * DH, :] for hp in range(NSA_HPG)], axis=1)
        zq = jnp.zeros_like(qh)
        return jnp.concatenate([qh, zq] if g == 0 else [zq, qh], axis=0)

    qpad = [padded_q(g) for g in GROUPS]
    t_row = q0 + lax.broadcasted_iota(jnp.int32, (1, Q_BLOCK), 1)

    ncmp = kcmp_ref.shape[1]

    def compressed(rows):
        n_iota = lax.broadcasted_iota(jnp.int32, (rows, Q_BLOCK), 0)
        bias_c = per_head(jnp.where((n_iota * CMP_STRIDE + (CMP_BLOCK - 1)) <= t_row, 0.0, NEG_INF))
        has_key = per_head(t_row) >= CMP_BLOCK - 1
        kc = kcmp_ref[0, 0:rows, :]
        cov = covT_ref[:, 0:rows]
        out = []
        for g in GROUPS:
            sc = jnp.dot(kc, qpad[g], preferred_element_type=f32) + bias_c
            pc = jnp.exp2(sc - jnp.max(sc, axis=0, keepdims=True))
            lc = jnp.sum(pc, axis=0, keepdims=True)
            pc = pc * jnp.where(has_key, 1.0 / lc, 0.0)
            o = jnp.dot(vcmpT_ref[0, vrows(g), 0:rows], pc.astype(bf16), preferred_element_type=f32)
            psum = pc[:, 0:Q_BLOCK]
            for hp in range(1, NSA_HPG):
                psum = psum + pc[:, hp * Q_BLOCK:(hp + 1) * Q_BLOCK]
            hi = psum.astype(bf16)
            r1 = psum - hi.astype(f32)
            mid = r1.astype(bf16)
            lo = (r1 - mid.astype(f32)).astype(bf16)
            out += [o, (jnp.dot(cov, hi, preferred_element_type=f32) + jnp.dot(cov, mid, preferred_element_type=f32)
                        + jnp.dot(cov, lo, preferred_element_type=f32))]
        return tuple(out)

    row_options = [r for r in (ncmp // 4, ncmp // 2, 3 * ncmp // 4, ncmp) if r % 128 == 0]
    need = (Q_BLOCK // CMP_STRIDE) * (i + 1)
    which = sum((need > r).astype(jnp.int32) for r in row_options[:-1])
    cmp_out = lax.switch(which, [functools.partial(compressed, r) for r in row_options])
    oc = [cmp_out[2 * g] for g in GROUPS]
    imp = [cmp_out[2 * g + 1] for g in GROUPS]

    j_i = lax.broadcasted_iota(jnp.int32, (n_slc, Q_BLOCK), 0)
    j_f = j_i.astype(f32)
    cur = t_row >> 6
    forced = (j_i == 0) | (j_i == cur) | (j_i == cur - 1)
    visible = (j_i * SLC_BLOCK) <= t_row
    score = [jnp.where(forced, TAKEN_SCORE, jnp.where(visible, imp[g], NEG_INF)) for g in GROUPS]

    w0 = pl.multiple_of(jnp.maximum(q0 - WINDOW, 0), Q_BLOCK)
    kp = w0 + lax.broadcasted_iota(jnp.int32, (WIN_KEYS, Q_BLOCK), 0)
    bias_w = per_head(jnp.where((kp <= t_row) & (kp > (t_row - WINDOW)), 0.0, NEG_INF))
    kwin = kw_ref[pl.ds(w0, WIN_KEYS), :]
    ow = []
    for g in GROUPS:
        sw = jnp.dot(kwin, qpad[g], preferred_element_type=f32) + bias_w
        pw = jnp.exp2(sw - jnp.max(sw, axis=0, keepdims=True)).astype(bf16)
        vw1 = jnp.concatenate([vwT_ref[vrows(g), pl.ds(w0, WIN_KEYS)], jnp.ones((ONES_ROWS, WIN_KEYS), bf16)],
                              axis=0)
        o = jnp.dot(vw1, pw, preferred_element_type=f32)
        ow.append(o[0:DH, :] * (1.0 / o[DH:DH + 1, :]))

    sel = [jnp.where(forced, 1.0, 0.0) for g in GROUPS]
    for _ in range(n_top - 3):
        for g in GROUPS:
            mx = jnp.max(score[g], axis=0, keepdims=True)
            first = jnp.min(jnp.where(score[g] == mx, j_f, float(n_slc)), axis=0, keepdims=True)
            hit = j_f == first
            sel[g] = jnp.where(hit, 1.0, sel[g])
            score[g] = jnp.where(hit, TAKEN_SCORE, score[g])

    for g in GROUPS:
        qaug_ref[g, 0:2 * DH, :] = qpad[g]
        blk_bias = jnp.where((sel[g] > 0.0) & visible, 0.0, NEG_INF)
        qaug_ref[g, 2 * DH:2 * DH + n_slc, :] = per_head(blk_bias).astype(bf16)
        if n_slc < 128:
            qaug_ref[g, 2 * DH + n_slc:, :] = jnp.zeros((128 - n_slc, HQ), bf16)

    def scores(c):
        k0 = pl.multiple_of(c * SEL_CHUNK, SEL_CHUNK)
        kaug = jnp.concatenate([ks_ref[pl.ds(k0, SEL_CHUNK), :], eblk_ref[pl.ds(k0, SEL_CHUNK), :]], axis=1)
        return [jnp.dot(kaug, qaug_ref[g], preferred_element_type=f32) for g in GROUPS]

    def attend(g, c, s, smax):
        k0 = pl.multiple_of(c * SEL_CHUNK, SEL_CHUNK)
        m = m_ref[g]
        m_new = jnp.maximum(m, smax)
        m_ref[g] = m_new
        p = jnp.exp2(s - m_new).astype(bf16)
        v1 = jnp.concatenate([vsT_ref[vrows(g), pl.ds(k0, SEL_CHUNK)], jnp.ones((ONES_ROWS, SEL_CHUNK), bf16)],
                             axis=0)
        acc_ref[g] = jnp.exp2(m - m_new) * acc_ref[g] + jnp.dot(v1, p, preferred_element_type=f32)

    def step(c, cur, nxt):
        smax = [smax_ref[g] for g in GROUPS]
        s_next = scores(c + 1)
        for g in GROUPS:
            s_ref[g, nxt] = s_next[g]
            smax_ref[g] = jnp.max(s_next[g], axis=0, keepdims=True)
        for g in GROUPS:
            attend(g, c, s_ref[g, cur], smax[g])

    last = q0 // SEL_CHUNK
    odd = last % 2
    s0 = scores(0)
    for g in GROUPS:
        s_ref[g, odd] = s0[g]
        smax_ref[g] = jnp.max(s0[g], axis=0, keepdims=True)
        acc_ref[g] = jnp.zeros((DH + ONES_ROWS, HQ), f32)
        m_ref[g] = jnp.full((1, HQ), NEG_INF, f32)

    @pl.when(odd == 1)
    def _():
        step(0, 1, 0)

    has_pair = (last // 2) % 2

    @pl.when(has_pair == 1)
    def _():
        step(odd, 0, 1)
        step(odd + 1, 1, 0)

    def quad(qq, carry):
        c = odd + 2 * has_pair + 4 * qq
        step(c, 0, 1)
        step(c + 1, 1, 0)
        step(c + 2, 0, 1)
        step(c + 3, 1, 0)
        return carry

    lax.fori_loop(0, last // 4, quad, 0)

    kp_d = last * SEL_CHUNK + lax.broadcasted_iota(jnp.int32, (SEL_CHUNK, Q_BLOCK), 0)
    bias_d = per_head(jnp.where(kp_d <= t_row, 0.0, NEG_INF))
    for g in GROUPS:
        s_d = s_ref[g, 0] + bias_d
        attend(g, last, s_d, jnp.max(s_d, axis=0, keepdims=True))

    gate = _sigmoid(gT_ref[...])
    for g in GROUPS:
        osel = acc_ref[g, 0:DH, :] * (1.0 / acc_ref[g, DH:DH + 1, :])
        for hp in range(NSA_HPG):
            cols = slice(hp * Q_BLOCK, (hp + 1) * Q_BLOCK)
            gr = g * GATE_ROWS + hp
            o = (gate[gr:gr + 1, :] * oc[g][:, cols] + gate[gr + 4:gr + 5, :] * osel[:, cols]
                 + gate[gr + 8:gr + 9, :] * ow[g][:, cols])
            o_ref[(g * NSA_HPG + hp) * DH:(g * NSA_HPG + hp + 1) * DH, :] = o.astype(bf16)


def _nsa(qT, gT, kcmp, vcmpT, ks, vsT, kw, vwT, covT, eblk, B, S):
    n_q = S // Q_BLOCK
    n_slc = S // SLC_BLOCK
    ncmp = S // CMP_STRIDE
    G = NSA_KV_HEADS
    HQ = NSA_HPG * Q_BLOCK
    tok = lambda r: pl.BlockSpec((r, Q_BLOCK), lambda b, i: (0, b * n_q + i))
    per_b_nat = pl.BlockSpec((S, 128), lambda b, i: (b, 0))
    per_b_tr = pl.BlockSpec((128, S), lambda b, i: (0, b))
    return pl.pallas_call(
        functools.partial(_nsa_kernel, n_slc=n_slc, n_top=min(N_SELECT, n_slc)),
        grid=(B, n_q),
        in_specs=[tok(NSA_HEADS * NSA_HEAD_DIM), tok(G * GATE_ROWS),
                  pl.BlockSpec((1, ncmp, 128), lambda b, i: (b, 0, 0)),
                  pl.BlockSpec((1, 128, ncmp), lambda b, i: (b, 0, 0)),
                  per_b_nat, per_b_tr, per_b_nat, per_b_tr,
                  _const_spec((n_slc, ncmp)), _const_spec((S, 128))],
        out_specs=tok(NSA_HEADS * NSA_HEAD_DIM),
        out_shape=jax.ShapeDtypeStruct((NSA_HEADS * NSA_HEAD_DIM, B * S), bf16),
        scratch_shapes=[pltpu.VMEM((G, 2 * NSA_HEAD_DIM + 128, HQ), bf16),
                        pltpu.VMEM((G, 2, SEL_CHUNK, HQ), f32),
                        pltpu.VMEM((G, NSA_HEAD_DIM + ONES_ROWS, HQ), f32),
                        pltpu.VMEM((G, 1, HQ), f32), pltpu.VMEM((G, 1, HQ), f32)],
        compiler_params=_params("parallel", "arbitrary"),
        name="nsa_attn",
    )(qT, gT, kcmp, vcmpT, ks, vsT, kw, vwT, covT, eblk)


def _memkv_kernel(mem_ref, wk_ref, wvT_ref, k_ref, vT_ref):
    m = mem_ref[0].astype(bf16)
    k_ref[0] = jnp.dot(m, wk_ref[...], preferred_element_type=f32).astype(bf16)
    vT_ref[0] = _nt(wvT_ref[...], m).astype(bf16)


def _memkv(mem, wk, wvT):
    B, M, _ = mem.shape
    return pl.pallas_call(
        _memkv_kernel,
        grid=(B,),
        in_specs=[pl.BlockSpec((1, M, D_MODEL), lambda b: (b, 0, 0)),
                  _const_spec((D_MODEL, 512)), _const_spec((512, D_MODEL))],
        out_specs=[pl.BlockSpec((1, M, 512), lambda b: (b, 0, 0)),
                   pl.BlockSpec((1, 512, M), lambda b: (b, 0, 0))],
        out_shape=[jax.ShapeDtypeStruct((B, M, 512), bf16), jax.ShapeDtypeStruct((B, 512, M), bf16)],
        compiler_params=_params("parallel"),
        name="mem_kv",
    )(mem, wk, wvT)


def _memattn_kernel(qT_ref, k_ref, vT_ref, o_ref):
    for h in range(MEM_HEADS):
        rows = slice(h * MEM_HEAD_DIM, (h + 1) * MEM_HEAD_DIM)
        s = jnp.dot(k_ref[0, :, rows], qT_ref[rows, :], preferred_element_type=f32) * (MEM_HEAD_DIM ** -0.5)
        p = jnp.exp(s - jnp.max(s, axis=0, keepdims=True))
        l = jnp.sum(p, axis=0, keepdims=True)
        o = jnp.dot(vT_ref[0, rows, :], p.astype(bf16), preferred_element_type=f32) * (1.0 / l)
        o_ref[rows, :] = o.astype(bf16)


def _memattn(mqT, mk, mvT, B, S, tq):
    M = mk.shape[1]
    nq = S // tq
    return pl.pallas_call(
        _memattn_kernel,
        grid=(B, nq),
        in_specs=[pl.BlockSpec((512, tq), lambda b, i: (0, b * nq + i)),
                  pl.BlockSpec((1, M, 512), lambda b, i: (b, 0, 0)),
                  pl.BlockSpec((1, 512, M), lambda b, i: (b, 0, 0))],
        out_specs=pl.BlockSpec((512, tq), lambda b, i: (0, b * nq + i)),
        out_shape=jax.ShapeDtypeStruct((512, B * S), bf16),
        compiler_params=_params("parallel", "parallel"),
        name="mem_attn",
    )(mqT, mk, mvT)


def _outproj_kernel(h0_ref, ol_ref, onT_ref, omT_ref, wo_ref, g1_ref, b1_ref, h1_ref, *, tm):
    for r0 in range(0, tm, OUT_SUB_ROWS):
        rows = slice(r0, r0 + OUT_SUB_ROWS)
        mixed = jnp.dot(ol_ref[rows, :], wo_ref[0:1024, :], preferred_element_type=f32)
        mixed = mixed + _tn(onT_ref[:, rows], wo_ref[1024:1536, :])
        mixed = mixed + _tn(omT_ref[:, rows], wo_ref[1536:2048, :])
        h1_ref[rows, :] = _ln(ALPHA * h0_ref[rows, :] + mixed, g1_ref[...], b1_ref[...])


def _outproj(h0, ol, onT, omT, wo, g1, b1, tm):
    T = h0.shape[0]
    row = lambda w: pl.BlockSpec((tm, w), lambda i: (i, 0))
    trs = pl.BlockSpec((512, tm), lambda i: (0, i))
    vec = _const_spec((1, D_MODEL))
    return pl.pallas_call(
        functools.partial(_outproj_kernel, tm=tm),
        grid=(T // tm,),
        in_specs=[row(D_MODEL), row(1024), trs, trs, _const_spec((D_MODEL, D_MODEL)), vec, vec],
        out_specs=row(D_MODEL),
        out_shape=jax.ShapeDtypeStruct((T, D_MODEL), f32),
        compiler_params=_params("parallel"),
        name="out_proj_ln1",
    )(h0, ol, onT, omT, wo, g1, b1)


def _ffn_kernel(h_ref, halo_ref, wg_ref, wu_ref, cwg_ref, cwu_ref, cbg_ref, cbu_ref, wd_ref,
                g2_ref, b2_ref, o_ref, lhs_ref, *, tm, blocks_per_seq):
    i = pl.program_id(0)
    j = pl.program_id(1)

    @pl.when(j == 0)
    def _():
        first = (i % blocks_per_seq) == 0
        lhs_ref[0:8, :] = jnp.where(first, 0.0, halo_ref[...]).astype(bf16)
        h = h_ref[...]
        lhs_ref[8:8 + tm, :] = h.astype(bf16)
        o_ref[...] = ALPHA * h

    lhs = lhs_ref[...]

    def conv(w_ref, cw_ref, cb_ref):
        up = jnp.dot(lhs, w_ref[...], preferred_element_type=f32)
        y = cb_ref[...] + pltpu.roll(up, 2, 0)[8:8 + tm, :] * cw_ref[0:1, :]
        y = y + pltpu.roll(up, 1, 0)[8:8 + tm, :] * cw_ref[1:2, :]
        return y + up[8:8 + tm, :] * cw_ref[2:3, :]

    act = (_gelu(conv(wg_ref, cwg_ref, cbg_ref)) * conv(wu_ref, cwu_ref, cbu_ref)).astype(bf16)
    o_ref[...] += jnp.dot(act, wd_ref[...], preferred_element_type=f32)

    @pl.when(j == pl.num_programs(1) - 1)
    def _():
        o_ref[...] = _ln(o_ref[...], g2_ref[...], b2_ref[...])


def _ffn(h1, wup, cw, cb, wd, g2, b2, tm, tf, blocks_per_seq):
    T = h1.shape[0]
    nf = D_FF // tf
    vec = _const_spec((1, D_MODEL))
    return pl.pallas_call(
        functools.partial(_ffn_kernel, tm=tm, blocks_per_seq=blocks_per_seq),
        grid=(T // tm, nf),
        in_specs=[pl.BlockSpec((tm, D_MODEL), lambda i, j: (i, 0)),
                  pl.BlockSpec((8, D_MODEL), lambda i, j: (jnp.maximum(i * (tm // 8) - 1, 0), 0)),
                  pl.BlockSpec((D_MODEL, tf), lambda i, j: (0, j)),
                  pl.BlockSpec((D_MODEL, tf), lambda i, j: (0, nf + j)),
                  pl.BlockSpec((FFN_CONV_WIDTH, tf), lambda i, j: (0, j)),
                  pl.BlockSpec((FFN_CONV_WIDTH, tf), lambda i, j: (0, nf + j)),
                  pl.BlockSpec((1, tf), lambda i, j: (0, j)),
                  pl.BlockSpec((1, tf), lambda i, j: (0, nf + j)),
                  pl.BlockSpec((tf, D_MODEL), lambda i, j: (j, 0)),
                  vec, vec],
        out_specs=pl.BlockSpec((tm, D_MODEL), lambda i, j: (i, 0)),
        out_shape=jax.ShapeDtypeStruct((T, D_MODEL), f32),
        scratch_shapes=[pltpu.VMEM((tm + 8, D_MODEL), bf16)],
        compiler_params=_params("parallel", "arbitrary", vmem_limit=V7X_VMEM_LIMIT_FFN_BYTES),
        name="conv_ffn_ln2",
    )(h1, h1, wup, wup, cw, cw, cb, cb, wd, g2, b2)


def _layer(x, mem, ln_in_g, ln_in_b, w_in, lru_conv_w, lru_conv_b, lru_wa, lru_ba, lru_wx, lru_bx,
           lru_lam, cmp_pe_k, cmp_w1_k, cmp_w2_k, cmp_pe_v, cmp_w1_v, cmp_w2_v, w_mem_kv, w_out,
           ln1_g, ln1_b, ffn_w_up, ffn_conv_w, ffn_conv_b, ffn_w_down, ln2_g, ln2_b,
           *, tm_proj, tb_lru, tq_mem, tm_out, tm_ffn, tf_ffn):
    B, S, _ = x.shape
    T = B * S
    G, Dh = NSA_KV_HEADS, NSA_HEAD_DIM
    row = lambda v: v.reshape(1, -1)

    c_q = 2 * LRU_WIDTH
    c_kv = c_q + NSA_HEADS * Dh
    c_gate = c_kv + 6 * G * Dh
    c_mq = c_gate + 3 * NSA_HEADS
    assert (c_q, c_kv, c_gate) == (_NQ0, _NKC0, _NCOLS)
    wn = w_in.astype(bf16)
    wmq = wn[:, c_mq:]
    w_gate = w_in[:, c_gate:c_mq].reshape(D_MODEL, G, NSA_HPG, 3)
    w_gate = jnp.pad(w_gate.transpose(1, 3, 2, 0), ((0, 0), (0, 1), (0, 0), (0, 0)))
    wgT = w_gate.reshape(G * GATE_ROWS, D_MODEL).astype(bf16)

    wax = (0.5 * jnp.concatenate([lru_wa, lru_wx], axis=-1)).astype(bf16)

    def cmp_weights(pe, w1, w2):
        pe2 = jnp.broadcast_to(pe.reshape(2, CMP_STRIDE, 1, Dh), (2, CMP_STRIDE, G, Dh)).reshape(2, -1)
        def group_diag(w, axis):
            z = jnp.zeros_like(w)
            return jnp.stack([jnp.concatenate([w if k == g else z for k in range(G)], axis=-1)
                              for g in range(G)], axis=axis)

        w1e = group_diag(w1.reshape(2, CMP_STRIDE, Dh, Dh), 2)
        w1e = w1e.reshape(2, CMP_STRIDE * G * Dh, G * Dh).astype(bf16)
        w2e = group_diag(w2, 0).reshape(G * Dh, G * Dh)
        return pe2, w1e, w2e

    pek, w1k, w2k = cmp_weights(cmp_pe_k, cmp_w1_k, cmp_w2_k)
    pev, w1v, w2v = cmp_weights(cmp_pe_v, cmp_w1_v, cmp_w2_v)

    n_slc, ncmp = S // SLC_BLOCK, S // CMP_STRIDE
    ci = jnp.arange(ncmp)[None, :] * CMP_STRIDE
    sj = jnp.arange(n_slc)[:, None] * SLC_BLOCK
    covT = ((ci <= sj + SLC_BLOCK - 1) & (ci + CMP_BLOCK - 1 >= sj)).astype(bf16)
    eblk = (jnp.arange(S)[:, None] // SLC_BLOCK == jnp.arange(128)[None, :]).astype(bf16)

    wk_mem = w_mem_kv[:, :512].astype(bf16)
    wvT_mem = w_mem_kv[:, 512:].T.astype(bf16)
    wo = w_out.astype(bf16)
    wup = ffn_w_up.astype(bf16)
    wd = ffn_w_down.astype(bf16)

    x2 = x.reshape(T, D_MODEL)
    lx, ly, kc, vc, ks, kw, qT, vsT, vwT, mqT, gT, h0 = _proj(x2, row(ln_in_g), row(ln_in_b), wn, wmq, wgT, tm_proj)

    o_lru = _lru(lx.reshape(B, S, LRU_WIDTH), ly.reshape(B, S, LRU_WIDTH), lru_conv_w, row(lru_conv_b),
                 wax, row(0.5 * lru_ba), row(0.5 * lru_bx), row(lru_lam), tb_lru)

    kcmp, vcmpT = _compress(kc, vc, pek, w1k, w2k.astype(bf16), pev, w1v, w2v.T.astype(bf16), B, S)
    o_nsaT = _nsa(qT, gT, kcmp, vcmpT, ks, vsT, kw, vwT, covT, eblk, B, S)

    mk, mvT = _memkv(mem, wk_mem, wvT_mem)
    o_memT = _memattn(mqT, mk, mvT, B, S, tq_mem)

    h1 = _outproj(h0, o_lru.reshape(T, LRU_WIDTH), o_nsaT, o_memT, wo, row(ln1_g), row(ln1_b), tm_out)
    out = _ffn(h1, wup, ffn_conv_w, row(ffn_conv_b), wd, row(ln2_g), row(ln2_b), tm_ffn, tf_ffn, S // tm_ffn)
    return out.reshape(B, S, D_MODEL)


def kernel(x, mem, ln_in_g, ln_in_b, w_in, lru_conv_w, lru_conv_b, lru_wa, lru_ba, lru_wx, lru_bx,
           lru_lam, cmp_pe_k, cmp_w1_k, cmp_w2_k, cmp_pe_v, cmp_w1_v, cmp_w2_v, w_mem_kv, w_out,
           ln1_g, ln1_b, ffn_w_up, ffn_conv_w, ffn_conv_b, ffn_w_down, ln2_g, ln2_b):
    return _layer(x, mem, ln_in_g, ln_in_b, w_in[0], lru_conv_w[0], lru_conv_b[0], lru_wa[0], lru_ba[0],
                  lru_wx[0], lru_bx[0], lru_lam[0], cmp_pe_k[0], cmp_w1_k[0], cmp_w2_k[0], cmp_pe_v[0],
                  cmp_w1_v[0], cmp_w2_v[0], w_mem_kv[0], w_out[0], ln1_g[0], ln1_b[0], ffn_w_up[0],
                  ffn_conv_w[0], ffn_conv_b[0], ffn_w_down[0], ln2_g[0], ln2_b[0],
                  tm_proj=512, tb_lru=256, tq_mem=1024, tm_out=1024, tm_ffn=1024, tf_ffn=512)
```
